```python
import jax, jax.numpy as jnp
from jax import lax
import numpy as np

D_MODEL = 1024
BATCH = 32
SEQ = 2048
DEPTH = 1

HG_HEADS = 8
HG_DK = 128
HG_DV = D_MODEL // HG_HEADS
HG_KW = HG_HEADS * HG_DK
HG_VW = HG_HEADS * HG_DV
HG_CHUNK = 64

NSA_HEADS = 16
NSA_KV_HEADS = 4
NSA_GROUP = NSA_HEADS // NSA_KV_HEADS
NSA_DIM = 64
NSA_QW = NSA_HEADS * NSA_DIM
NSA_KVW = NSA_KV_HEADS * NSA_DIM
CMP_LEN = 32
CMP_STRIDE = 16
CMP_HIDDEN = 256
SEL_BLOCK = 64
SEL_TOPN = 8
SEL_FORCE = 1000.0
WINDOW = 512
NSA_Q_BLOCK = 32
ROPE_THETA = 10000.0

D_FF = ((8 * D_MODEL + 3 * 256 - 1) // (3 * 256)) * 256

DN_ALPHA = (2.0 * DEPTH) ** 0.25
DN_BETA = (8.0 * DEPTH) ** -0.25
LN_EPS = 1e-5
RMS_EPS = 1e-6

IN_SIZES = (HG_KW, HG_KW, HG_VW, HG_VW,
            NSA_QW, NSA_KVW, NSA_KVW, NSA_KVW, NSA_KVW, NSA_KVW, NSA_KVW,
            3 * NSA_HEADS, D_MODEL, D_MODEL)
IN_SPLITS = tuple(int(v) for v in np.cumsum(IN_SIZES)[:-1])
N_IN = int(sum(IN_SIZES))

kernel_name = 'hybrid_hgrn2_nsa_deepnorm_block'


def layer_norm(x, g, b):
    x = x.astype(jnp.float32)
    mu = jnp.mean(x, -1, keepdims=True)
    var = jnp.mean(jnp.square(x - mu), -1, keepdims=True)
    return (x - mu) * lax.rsqrt(var + LN_EPS) * g.astype(jnp.float32) + b.astype(jnp.float32)


def rope(x, positions):
    half = x.shape[-1] // 2
    inv_freq = ROPE_THETA ** (-jnp.arange(half, dtype=jnp.float32) / half)
    ang = positions.astype(jnp.float32)[..., None] * inv_freq
    cos = jnp.cos(ang)[:, :, None, :]
    sin = jnp.sin(ang)[:, :, None, :]
    x = x.astype(jnp.float32)
    x1, x2 = x[..., :half], x[..., half:]
    return jnp.concatenate([x1 * cos - x2 * sin, x2 * cos + x1 * sin], -1)


def masked_softmax(s, mask):
    s = jnp.where(mask, s, -1e30)
    m = jnp.max(s, -1, keepdims=True)
    e = jnp.where(mask, jnp.exp(s - m), 0.0)
    return e / jnp.maximum(jnp.sum(e, -1, keepdims=True), 1e-30)


def overlap_matrix(n_cmp, n_blk):
    cs = np.arange(n_cmp)[:, None] * CMP_STRIDE
    bs = np.arange(n_blk)[None, :] * SEL_BLOCK
    ov = np.minimum(cs + CMP_LEN, bs + SEL_BLOCK) - np.maximum(cs, bs)
    return (np.clip(ov, 0, None) / CMP_LEN).astype(np.float32)


def hgrn2_mix(q, f_logit, i, g, lb, norm_g):
    f32 = jnp.float32
    B, S, _ = q.shape
    H, C = HG_HEADS, HG_CHUNK
    nc = S // C

    def heads(t, d):
        return t.astype(f32).reshape(B, nc, C, H, d).transpose(0, 3, 1, 2, 4)

    f = lb + (1.0 - lb) * jax.nn.sigmoid(f_logit.astype(f32))
    q_h = heads(q, HG_DK)
    k_h = heads(1.0 - f, HG_DK)
    v_h = heads(i, HG_DV)
    b = jnp.cumsum(heads(jnp.log(f), HG_DK), axis=3)
    b_last = b[:, :, :, -1:, :]
    q_dec = q_h * jnp.exp(b)
    k_inv = k_h * jnp.exp(-b)
    k_end = k_h * jnp.exp(b_last - b)
    causal = jnp.tril(jnp.ones((C, C), dtype=bool))
    attn = jnp.where(causal, jnp.einsum('bhntk,bhnsk->bhnts', q_dec, k_inv), 0.0)
    o_intra = jnp.einsum('bhnts,bhnsv->bhntv', attn, v_h)

    def step(state, xs):
        qd, ke, v, decay = xs
        o = jnp.einsum('bhtk,bhkv->bhtv', qd, state)
        state = decay[..., None] * state + jnp.einsum('bhsk,bhsv->bhkv', ke, v)
        return state, o

    xs = (jnp.moveaxis(q_dec, 2, 0), jnp.moveaxis(k_end, 2, 0), jnp.moveaxis(v_h, 2, 0),
          jnp.moveaxis(jnp.exp(b_last[:, :, :, 0, :]), 2, 0))
    state0 = jnp.zeros((B, H, HG_DK, HG_DV), f32)
    _, o_inter = lax.scan(step, state0, xs)
    o = o_intra + jnp.moveaxis(o_inter, 0, 2)
    o = o.transpose(0, 2, 3, 1, 4).reshape(B, S, H, HG_DV)
    o = o * lax.rsqrt(jnp.mean(jnp.square(o), -1, keepdims=True) + RMS_EPS) * norm_g.astype(f32)
    return o.reshape(B, S, HG_VW) * jax.nn.silu(g.astype(f32))


def compress(t, pos_emb, w1, b1, w2, b2):
    B, S, H, d = t.shape
    n_cmp = (S - CMP_LEN) // CMP_STRIDE + 1
    idx = np.arange(n_cmp)[:, None] * CMP_STRIDE + np.arange(CMP_LEN)[None, :]
    blk = t[:, idx] + pos_emb[None, None, :, None, :]
    blk = blk.transpose(0, 1, 3, 2, 4).reshape(B, n_cmp, H, CMP_LEN * d)
    hdn = jax.nn.silu(blk @ w1 + b1)
    return hdn @ w2 + b2


def nsa_mix(q, k_cmp, v_cmp, k_sel, v_sel, k_win, v_win, gates):
    f32 = jnp.float32
    B, S = q.shape[:2]
    KV, G, D, QB = NSA_KV_HEADS, NSA_GROUP, NSA_DIM, NSA_Q_BLOCK
    n_blk = S // SEL_BLOCK
    top_n = min(SEL_TOPN, n_blk)
    n_cmp = k_cmp.shape[1]
    cmp_end = jnp.asarray(np.arange(n_cmp) * CMP_STRIDE + CMP_LEN - 1)
    overlap = jnp.asarray(overlap_matrix(n_cmp, n_blk))
    scale = D ** -0.5
    ks_blk = k_sel.reshape(B, n_blk, SEL_BLOCK, KV, D).transpose(0, 3, 1, 2, 4)
    vs_blk = v_sel.reshape(B, n_blk, SEL_BLOCK, KV, D).transpose(0, 3, 1, 2, 4)
    kw_pad = jnp.pad(k_win, ((0, 0), (WINDOW, 0), (0, 0), (0, 0)))
    vw_pad = jnp.pad(v_win, ((0, 0), (WINDOW, 0), (0, 0), (0, 0)))
    gates = jax.nn.sigmoid(gates.astype(f32)).reshape(B, S, 3, KV, G)
    b_idx = jnp.arange(B)[:, None, None, None]
    h_idx = jnp.arange(KV)[None, :, None, None]
    blk_ids = jnp.arange(n_blk)
    sub_pos = jnp.arange(SEL_BLOCK)

    def one_block(i):
        q0 = i * QB
        t = q0 + jnp.arange(QB)
        qb = lax.dynamic_slice_in_dim(q, q0, QB, 1).reshape(B, QB, KV, G, D)
        s_c = jnp.einsum('bqhgd,bnhd->bhgqn', qb, k_cmp) * scale
        p_c = masked_softmax(s_c, cmp_end[None, :] <= t[:, None])
        o_c = jnp.einsum('bhgqn,bnhd->bqhgd', p_c, v_cmp)
        imp = jnp.einsum('bhgqn,nk->bhqk', p_c, overlap)
        cur = t // SEL_BLOCK
        blk_ok = blk_ids[None, :] <= cur[:, None]
        forced = ((blk_ids[None, :] == 0) | (blk_ids[None, :] == cur[:, None])
                  | (blk_ids[None, :] == cur[:, None] - 1))
        score = jnp.where(blk_ok, imp + SEL_FORCE * forced, -1.0)
        _, sel = lax.top_k(score, top_n)
        sel_ok = sel <= cur[None, None, :, None]
        k_g = ks_blk[b_idx, h_idx, sel].reshape(B, KV, QB, top_n * SEL_BLOCK, D)
        v_g = vs_blk[b_idx, h_idx, sel].reshape(B, KV, QB, top_n * SEL_BLOCK, D)
        key_pos = sel[..., None] * SEL_BLOCK + sub_pos
        mask_s = (sel_ok[..., None] & (key_pos <= t[None, None, :, None, None])
                  ).reshape(B, KV, QB, top_n * SEL_BLOCK)
        s_s = jnp.einsum('bqhgd,bhqmd->bhgqm', qb, k_g) * scale
        p_s = masked_softmax(s_s, mask_s[:, :, None])
        o_s = jnp.einsum('bhgqm,bhqmd->bqhgd', p_s, v_g)
        k_wb = lax.dynamic_slice_in_dim(kw_pad, q0, WINDOW + QB, 1)
        v_wb = lax.dynamic_slice_in_dim(vw_pad, q0, WINDOW + QB, 1)
        w_pos = q0 - WINDOW + jnp.arange(WINDOW + QB)
        mask_w = ((w_pos[None, :] >= 0) & (w_pos[None, :] <= t[:, None])
                  & (w_pos[None, :] > t[:, None] - WINDOW))
        s_w = jnp.einsum('bqhgd,bkhd->bhgqk', qb, k_wb) * scale
        p_w = masked_softmax(s_w, mask_w)
        o_w = jnp.einsum('bhgqk,bkhd->bqhgd', p_w, v_wb)
        g = lax.dynamic_slice_in_dim(gates, q0, QB, 1)
        o = (g[:, :, 0, :, :, None] * o_c + g[:, :, 1, :, :, None] * o_s
             + g[:, :, 2, :, :, None] * o_w)
        return o.reshape(B, QB, NSA_QW)

    out = lax.map(one_block, jnp.arange(S // QB))
    return out.transpose(1, 0, 2, 3).reshape(B, S, NSA_QW)


def setup_inputs(seed: int = 0) -> dict:
    key = jax.random.key(seed)
    ks = jax.random.split(key, 32)
    n = lambda k, shape: jax.random.normal(k, shape, jnp.float32)
    L = DEPTH
    cin = CMP_LEN * NSA_DIM
    return {
        'x': n(ks[0], (BATCH, SEQ, D_MODEL)),
        'positions': jnp.broadcast_to(jnp.arange(SEQ, dtype=jnp.int32), (BATCH, SEQ)),
        'w_in': n(ks[1], (L, D_MODEL, N_IN)) * D_MODEL ** -0.5,
        'hg_lb_logits': n(ks[2], (DEPTH + 1, HG_KW)) * 0.5,
        'hg_norm_g': 1.0 + 0.02 * n(ks[3], (L, HG_DV)),
        'cmp_k_pos': 0.02 * n(ks[4], (L, CMP_LEN, NSA_DIM)),
        'cmp_k_w1': n(ks[5], (L, cin, CMP_HIDDEN)) * cin ** -0.5,
        'cmp_k_b1': 0.01 * n(ks[6], (L, CMP_HIDDEN)),
        'cmp_k_w2': n(ks[7], (L, CMP_HIDDEN, NSA_DIM)) * CMP_HIDDEN ** -0.5,
        'cmp_k_b2': 0.01 * n(ks[8], (L, NSA_DIM)),
        'cmp_v_pos': 0.02 * n(ks[9], (L, CMP_LEN, NSA_DIM)),
        'cmp_v_w1': n(ks[10], (L, cin, CMP_HIDDEN)) * cin ** -0.5,
        'cmp_v_b1': 0.01 * n(ks[11], (L, CMP_HIDDEN)),
        'cmp_v_w2': n(ks[12], (L, CMP_HIDDEN, NSA_DIM)) * CMP_HIDDEN ** -0.5,
        'cmp_v_b2': 0.01 * n(ks[13], (L, NSA_DIM)),
        'w_up_hg': n(ks[14], (L, HG_VW, D_MODEL)) * HG_VW ** -0.5,
        'w_up_nsa': n(ks[15], (L, NSA_QW, D_MODEL)) * NSA_QW ** -0.5,
        'w_o': n(ks[16], (L, D_MODEL, D_MODEL)) * (D_MODEL ** -0.5 * DN_BETA),
        'ln1_g': 1.0 + 0.02 * n(ks[17], (L, D_MODEL)),
        'ln1_b': 0.02 * n(ks[18], (L, D_MODEL)),
        'w_ffn_gate': n(ks[19], (L, D_MODEL, D_FF)) * D_MODEL ** -0.5,
        'w_ffn_up': n(ks[20], (L, D_MODEL, D_FF)) * D_MODEL ** -0.5,
        'w_ffn_down': n(ks[21], (L, D_FF, D_MODEL)) * (D_FF ** -0.5 * DN_BETA),
        'ln2_g': 1.0 + 0.02 * n(ks[22], (L, D_MODEL)),
        'ln2_b': 0.02 * n(ks[23], (L, D_MODEL)),
    }


def reference(x, positions, w_in, hg_lb_logits, hg_norm_g,
              cmp_k_pos, cmp_k_w1, cmp_k_b1, cmp_k_w2, cmp_k_b2,
              cmp_v_pos, cmp_v_w1, cmp_v_b1, cmp_v_w2, cmp_v_b2,
              w_up_hg, w_up_nsa, w_o, ln1_g, ln1_b,
              w_ffn_gate, w_ffn_up, w_ffn_down, ln2_g, ln2_b):
    f32 = jnp.float32
    B, S, _ = x.shape
    lb_table = jnp.cumsum(jax.nn.softmax(hg_lb_logits.astype(f32), axis=0), axis=0)

    def kv_heads(t):
        return t.astype(f32).reshape(B, S, NSA_KV_HEADS, NSA_DIM)

    h = x
    for l in range(DEPTH):
        proj = jnp.einsum('bsd,dc->bsc', h, w_in[l])
        (hg_q, hg_f, hg_i, hg_g, nsa_q, k_c, v_c, k_s, v_s, k_w, v_w,
         nsa_gate, gate_a, gate_b) = jnp.split(proj, IN_SPLITS, axis=-1)

        a = hgrn2_mix(hg_q, hg_f, hg_i, hg_g, lb_table[l], hg_norm_g[l])

        q = rope(nsa_q.reshape(B, S, NSA_HEADS, NSA_DIM), positions)
        k_cmp = compress(rope(kv_heads(k_c), positions), cmp_k_pos[l], cmp_k_w1[l],
                         cmp_k_b1[l], cmp_k_w2[l], cmp_k_b2[l])
        v_cmp = compress(kv_heads(v_c), cmp_v_pos[l], cmp_v_w1[l],
                         cmp_v_b1[l], cmp_v_w2[l], cmp_v_b2[l])
        b_out = nsa_mix(q, k_cmp, v_cmp,
                        rope(kv_heads(k_s), positions), kv_heads(v_s),
                        rope(kv_heads(k_w), positions), kv_heads(v_w), nsa_gate)

        a_d = a @ w_up_hg[l]
        b_d = b_out @ w_up_nsa[l]
        merged = (jax.nn.sigmoid(gate_a.astype(f32)) * a_d
                  + jax.nn.sigmoid(gate_b.astype(f32)) * b_d)
        mix = merged @ w_o[l]
        h = layer_norm(DN_ALPHA * h + mix, ln1_g[l], ln1_b[l]).astype(x.dtype)

        ffn = (jax.nn.silu(h @ w_ffn_gate[l]) * (h @ w_ffn_up[l])) @ w_ffn_down[l]
        h = layer_norm(DN_ALPHA * h + ffn, ln2_g[l], ln2_b[l]).astype(x.dtype)
    return h
```

```python
import functools

import numpy as np
import jax
import jax.numpy as jnp
from jax import lax
from jax.experimental import pallas as pl
from jax.experimental.pallas import tpu as pltpu

F32 = jnp.float32
BF16 = jnp.bfloat16

D_MODEL = 1024
HG_HEADS = 8
HG_DK = 128
HG_DV = 128
HG_W = HG_HEADS * HG_DK
HG_CHUNK = 64
NSA_HEADS = 16
NSA_KV = 4
NSA_G = 4
NSA_D = 64
NSA_QW = NSA_HEADS * NSA_D
NSA_KVW = NSA_KV * NSA_D
CMP_LEN = 32
CMP_STRIDE = 16
CMP_HIDDEN = 256
SEL_BLOCK = 64
SEL_TOPN = 8
SEL_FORCE = 1000.0
WINDOW = 512
ROPE_THETA = 10000.0
DEPTH = 1
DN_ALPHA = (2.0 * DEPTH) ** 0.25
LN_EPS = 1e-5
RMS_EPS = 1e-6
LOG2E = 1.4426950408889634
NEG = -1e30

LANES = 128
SLOT_W = NSA_KV * LANES
VMEM_LIMIT = 56 * 1024 * 1024


def _dot(a, b):
    return jnp.dot(a, b, preferred_element_type=F32)


def _dot_nt(a, b):
    return lax.dot_general(a, b, (((1,), (1,)), ((), ())), preferred_element_type=F32)


def _dot_tn(a, b):
    return lax.dot_general(a, b, (((0,), (0,)), ((), ())), preferred_element_type=F32)


def _sigmoid(x):
    return 1.0 / (1.0 + jnp.exp(-x))


def _layer_norm(x, g, b):
    mu = jnp.mean(x, -1, keepdims=True)
    xc = x - mu
    var = jnp.mean(xc * xc, -1, keepdims=True)
    return xc * lax.rsqrt(var + LN_EPS) * g + b


def _params(sem):
    return pltpu.CompilerParams(dimension_semantics=sem, vmem_limit_bytes=VMEM_LIMIT)


def _const_spec(shape):
    nd = len(shape)
    return pl.BlockSpec(shape, lambda *_: (0,) * nd)


def _hg_proj_kernel(x_ref, w_ref, q_ref, f_ref, i_ref, g_ref):
    x = x_ref[...].astype(BF16)
    q_ref[...] = _dot(x, w_ref[:, 0:HG_W]).astype(BF16)
    f_ref[...] = _dot(x, w_ref[:, HG_W:2 * HG_W])
    i_ref[...] = _dot(x, w_ref[:, 2 * HG_W:3 * HG_W]).astype(BF16)
    g = _dot(x, w_ref[:, 3 * HG_W:4 * HG_W])
    g_ref[...] = (g * _sigmoid(g)).astype(BF16)


def _hg_proj(x2, w_hg, tm):
    T = x2.shape[0]
    row = lambda i: (i, 0)
    out_spec = pl.BlockSpec((tm, HG_W), row)
    return pl.pallas_call(
        _hg_proj_kernel,
        grid=(T // tm,),
        in_specs=[pl.BlockSpec((tm, D_MODEL), row), _const_spec((D_MODEL, 4 * HG_W))],
        out_specs=[out_spec] * 4,
        out_shape=[jax.ShapeDtypeStruct((T, HG_W), BF16), jax.ShapeDtypeStruct((T, HG_W), F32),
                   jax.ShapeDtypeStruct((T, HG_W), BF16), jax.ShapeDtypeStruct((T, HG_W), BF16)],
        compiler_params=_params(("parallel",)),
        name="hg_proj",
    )(x2, w_hg)


_NQ = NSA_QW
_NKC = NSA_KVW
_NSL = SLOT_W
_ROPE_W = _NQ + _NKC + 2 * _NSL
_O_KC = _NQ
_O_KS = _NQ + _NKC
_O_KW = _O_KS + _NSL
_O_VC = _ROPE_W
_O_VS = _O_VC + _NKC
_O_VW = _O_VS + _NSL
_O_GT = _O_VW + _NSL
_NSA_W = _O_GT + LANES


def _nsa_proj_kernel(x_ref, w_ref, cos_ref, sin_ref, e_ref,
                     q_ref, kc_ref, ks_ref, kw_ref, vc_ref, vs_ref, vw_ref, gt_ref):
    x = x_ref[...].astype(BF16)
    cos = cos_ref[...]
    sin = sin_ref[...]
    half = NSA_D // 2

    def rope(y):
        w = y.shape[1]
        reps = w // LANES
        lane = lax.broadcasted_iota(jnp.int32, y.shape, 1)
        fwd = pltpu.roll(y, w - half, 1)
        bwd = pltpu.roll(y, half, 1)
        rot = jnp.where((lane % NSA_D) < half, fwd, bwd)
        return y * jnp.tile(cos, (1, reps)) + rot * jnp.tile(sin, (1, reps))

    yq = rope(_dot(x, w_ref[:, 0:_NQ]))
    q_ref[...] = (yq * (NSA_D ** -0.5 * LOG2E)).astype(BF16)
    kc_ref[...] = rope(_dot(x, w_ref[:, _O_KC:_O_KC + _NKC])).astype(BF16)
    ks_ref[...] = rope(_dot(x, w_ref[:, _O_KS:_O_KS + _NSL])).astype(BF16) + e_ref[...]
    kw_ref[...] = rope(_dot(x, w_ref[:, _O_KW:_O_KW + _NSL])).astype(BF16)
    vc_ref[...] = _dot(x, w_ref[:, _O_VC:_O_VC + _NKC]).astype(BF16)
    vs_ref[...] = _dot(x, w_ref[:, _O_VS:_O_VS + _NSL]).astype(BF16)
    vw_ref[...] = _dot(x, w_ref[:, _O_VW:_O_VW + _NSL]).astype(BF16)
    gt_ref[...] = _sigmoid(_dot(x, w_ref[:, _O_GT:_O_GT + LANES]))


def _nsa_proj(x2, w_nsa, cos_t, sin_t, e_tab, tm, S):
    T = x2.shape[0]
    row = lambda i: (i, 0)
    tiles_per_seq = S // tm
    widths = [_NQ, _NKC, _NSL, _NSL, _NKC, _NSL, _NSL]
    return pl.pallas_call(
        _nsa_proj_kernel,
        grid=(T // tm,),
        in_specs=[pl.BlockSpec((tm, D_MODEL), row), _const_spec((D_MODEL, _NSA_W)),
                  pl.BlockSpec((tm, LANES), row), pl.BlockSpec((tm, LANES), row),
                  pl.BlockSpec((tm, _NSL), lambda i: (i % tiles_per_seq, 0))],
        out_specs=[pl.BlockSpec((tm, w), row) for w in widths] + [pl.BlockSpec((tm, LANES), row)],
        out_shape=[jax.ShapeDtypeStruct((T, w), BF16) for w in widths]
                  + [jax.ShapeDtypeStruct((T, LANES), F32)],
        compiler_params=_params(("parallel",)),
        name="nsa_proj",
    )(x2, w_nsa, cos_t, sin_t, e_tab)


def _hgrn_kernel(q_ref, f_ref, i_ref, g_ref, lb_ref, ng_ref, o_ref, *, n_chunks):
    C = HG_CHUNK
    lb = lb_ref[...]
    ng = ng_ref[...]
    r = lax.broadcasted_iota(jnp.int32, (C, C), 0)
    c = lax.broadcasted_iota(jnp.int32, (C, C), 1)
    tril = r >= c
    tril_f = tril.astype(F32)

    def body(ci, st_t):
        sl = pl.ds(pl.multiple_of(ci * C, C), C)
        f = lb + (1.0 - lb) * _sigmoid(f_ref[sl, :])
        b = jnp.dot(tril_f, jnp.log(f), precision=lax.Precision.HIGHEST,
                    preferred_element_type=F32)
        b_last = b[C - 1:C, :]
        k = 1.0 - f
        q_dec = (q_ref[sl, :].astype(F32) * jnp.exp(b)).astype(BF16)
        k_inv = (k * jnp.exp(-b)).astype(BF16)
        k_end = (k * jnp.exp(b_last - b)).astype(BF16)
        v = i_ref[sl, :]
        attn = jnp.where(tril, _dot_nt(q_dec, k_inv), 0.0).astype(BF16)
        o = _dot(attn, v) + _dot_nt(q_dec, st_t.astype(BF16))
        st_t = st_t * jnp.exp(b_last) + _dot_tn(v, k_end)
        ms = jnp.mean(o * o, -1, keepdims=True)
        on = o * lax.rsqrt(ms + RMS_EPS) * ng
        o_ref[sl, :] = (on * g_ref[sl, :].astype(F32)).astype(BF16)
        return st_t

    lax.fori_loop(0, n_chunks, body, jnp.zeros((HG_DV, HG_DK), F32))


def _hgrn(q, f, i, g, lb, ng, B, S):
    T = B * S
    blk = pl.BlockSpec((S, HG_DK), lambda b, h: (b, h))
    return pl.pallas_call(
        functools.partial(_hgrn_kernel, n_chunks=S // HG_CHUNK),
        grid=(B, HG_HEADS),
        in_specs=[blk, blk, blk, blk,
                  pl.BlockSpec((1, HG_DK), lambda b, h: (0, h)),
                  pl.BlockSpec((1, HG_DV), lambda b, h: (0, 0))],
        out_specs=blk,
        out_shape=jax.ShapeDtypeStruct((T, HG_W), BF16),
        compiler_params=_params(("parallel", "parallel")),
        name="hgrn",
    )(q, f, i, g, lb, ng)


def _compress_kernel(uk_ref, uv_ref, pk_ref, w1k_ref, b1k_ref, w2k_ref, b2k_ref,
                     pv_ref, w1v_ref, b1v_ref, w2v_ref, b2v_ref, ko_ref, vo_ref):
    half = CMP_STRIDE * NSA_D

    def mlp(u_ref, pos_ref, w1_ref, b1_ref, w2_ref, b2_ref, o_ref):
        w_top = w1_ref[0:half, :]
        w_bot = w1_ref[half:2 * half, :]
        base = _dot(pos_ref[...], w1_ref[...])[0:1, :] + b1_ref[...]
        for h in range(NSA_KV):
            u = u_ref[0, h]
            top = _dot(u, w_top)
            bot = _dot(u, w_bot)
            n = top.shape[0]
            bot_next = jnp.concatenate([bot[1:n, :], jnp.zeros((1, CMP_HIDDEN), F32)], axis=0)
            hid = top + bot_next + base
            hid = hid * _sigmoid(hid)
            o_ref[0, h] = (_dot(hid.astype(BF16), w2_ref[...]) + b2_ref[...]).astype(BF16)

    mlp(uk_ref, pk_ref, w1k_ref, b1k_ref, w2k_ref, b2k_ref, ko_ref)
    mlp(uv_ref, pv_ref, w1v_ref, b1v_ref, w2v_ref, b2v_ref, vo_ref)


def _compress(uk, uv, pk, w1k, b1k, w2k, b2k, pv, w1v, b1v, w2v, b2v):
    B, _, nh, hw = uk.shape
    u_spec = pl.BlockSpec((1, NSA_KV, nh, hw), lambda b: (b, 0, 0, 0))
    o_spec = pl.BlockSpec((1, NSA_KV, nh, LANES), lambda b: (b, 0, 0, 0))
    wspecs = [_const_spec(a.shape) for a in (pk, w1k, b1k, w2k, b2k)]
    return pl.pallas_call(
        _compress_kernel,
        grid=(B,),
        in_specs=[u_spec, u_spec] + wspecs + wspecs,
        out_specs=[o_spec, o_spec],
        out_shape=[jax.ShapeDtypeStruct((B, NSA_KV, nh, LANES), BF16)] * 2,
        compiler_params=_params(("parallel",)),
        name="compress",
    )(uk, uv, pk, w1k, b1k, w2k, b2k, pv, w1v, b1v, w2v, b2v)


def _nsa_kernel(q_ref, ks_ref, vs_ref, kw_ref, vw_ref, kc_ref, vc_ref, gt_ref, ovl_ref,
                o_ref, sbuf, acc_ref, lt_ref, *, TQ, CK, WSLAB, n_cmp, n_blk, top_n):
    R = NSA_G * TQ
    q0 = pl.program_id(1) * TQ
    lane_q = lax.broadcasted_iota(jnp.int32, (TQ, LANES), 1)
    t_row = q0 + (lax.broadcasted_iota(jnp.int32, (R, 1), 0) & (TQ - 1))
    t_lane = q0 + (lax.broadcasted_iota(jnp.int32, (1, R), 1) & (TQ - 1))

    n_l = lax.broadcasted_iota(jnp.int32, (R, LANES), 1)
    mask_c = (n_l * CMP_STRIDE + (CMP_LEN - 1) <= t_row) & (n_l < n_cmp)
    n_s = lax.broadcasted_iota(jnp.int32, (LANES, R), 0)
    mask_ct = (n_s * CMP_STRIDE + (CMP_LEN - 1) <= t_lane) & (n_s < n_cmp)

    j_blk = lax.broadcasted_iota(jnp.int32, (n_blk, TQ), 0)
    cur = (q0 + lax.broadcasted_iota(jnp.int32, (n_blk, TQ), 1)) // SEL_BLOCK
    blk_ok = j_blk <= cur
    forced = (j_blk == 0) | (j_blk == cur) | (j_blk == cur - 1)

    n_sel_chunks = (q0 + TQ + CK - 1) // CK
    w_start = pl.multiple_of(jnp.maximum(q0 + TQ - WSLAB, 0), TQ)
    key_w = w_start + lax.broadcasted_iota(jnp.int32, (R, WSLAB), 1)
    mask_w = (key_w <= t_row) & (key_w > t_row - WINDOW)
    key_l = lax.broadcasted_iota(jnp.int32, (R, CK), 1)

    def fold(x):
        return [x[:, u * LANES:(u + 1) * LANES] for u in range(x.shape[1] // LANES)]

    for kvh in range(NSA_KV):
        hs = slice(kvh * LANES, (kvh + 1) * LANES)
        q_slabs = []
        for pair in range(NSA_G // 2):
            col = (kvh * NSA_G + 2 * pair) * NSA_D
            x = q_ref[:, col:col + LANES].astype(F32)
            q_slabs.append(x)
            q_slabs.append(pltpu.roll(x, NSA_D, 1))
        q_plain = jnp.concatenate(
            [jnp.where(lane_q < NSA_D, x, 0.0).astype(BF16) for x in q_slabs], axis=0)

        kc = kc_ref[0, kvh]
        vc = vc_ref[0, kvh]
        s_c = jnp.where(mask_c, _dot_nt(q_plain, kc), NEG)
        m_c = jnp.max(s_c, -1, keepdims=True)
        e_c = jnp.where(mask_c, jnp.exp2(s_c - m_c), 0.0)
        l_c = jnp.maximum(jnp.sum(e_c, -1, keepdims=True), 1e-30)
        acc_c = _dot(e_c.astype(BF16), vc)

        s_ct = jnp.where(mask_ct, _dot_nt(kc, q_plain), NEG)
        m_ct = jnp.max(s_ct, 0, keepdims=True)
        e_ct = jnp.where(mask_ct, jnp.exp2(s_ct - m_ct), 0.0)
        p_ct = e_ct / jnp.maximum(jnp.sum(e_ct, 0, keepdims=True), 1e-30)
        imp4 = _dot(ovl_ref[...], p_ct.astype(BF16))
        imp = imp4[:, 0:TQ]
        for g in range(1, NSA_G):
            imp = imp + imp4[:, g * TQ:(g + 1) * TQ]
        score = jnp.where(blk_ok, imp + jnp.where(forced, SEL_FORCE, 0.0), -1.0)
        rank = jnp.zeros((n_blk, TQ), jnp.int32)
        for i in range(n_blk):
            ri = score[i:i + 1, :]
            beats = (ri > score) | ((ri == score) & (j_blk > i))
            rank = rank + beats.astype(jnp.int32)
        sel_t = ((rank < top_n) & blk_ok).astype(F32)
        if n_blk < LANES:
            sel_t = jnp.concatenate([sel_t, jnp.zeros((LANES - n_blk, TQ), F32)], axis=0)
        sel = pltpu.roll(sel_t.T, NSA_D, 1)
        bias = jnp.where((lane_q >= NSA_D) & (lane_q < NSA_D + n_blk) & (sel < 0.5), NEG, 0.0)
        q_aug = jnp.concatenate(
            [jnp.where(lane_q < NSA_D, x, bias).astype(BF16) for x in q_slabs], axis=0)

        def scores(ci, masked):
            kch = ks_ref[pl.ds(pl.multiple_of(ci * CK, CK), CK), hs]
            s = _dot_nt(q_aug, kch)
            if masked:
                s = jnp.where(ci * CK + key_l <= t_row, s, NEG)
            sbuf[ci] = s
            return functools.reduce(jnp.maximum, fold(s))

        def p1(ci, mt):
            return jnp.maximum(mt, scores(ci, False))

        mt = lax.fori_loop(0, n_sel_chunks - 1, p1, jnp.full((R, LANES), -3e38, F32))
        mt = jnp.maximum(mt, scores(n_sel_chunks - 1, True))
        m_s = jnp.max(mt, -1, keepdims=True)

        acc_ref[...] = jnp.zeros((R, LANES), F32)
        lt_ref[...] = jnp.zeros((R, LANES), F32)

        def p2(ci, carry):
            p = jnp.exp2(sbuf[ci] - m_s)
            lt_ref[...] += functools.reduce(jnp.add, fold(p))
            vch = vs_ref[pl.ds(pl.multiple_of(ci * CK, CK), CK), hs]
            acc_ref[...] += _dot(p.astype(BF16), vch)
            return carry

        lax.fori_loop(0, n_sel_chunks, p2, 0)
        l_s = jnp.sum(lt_ref[...], -1, keepdims=True)
        acc_s = acc_ref[...]

        kws = kw_ref[pl.ds(w_start, WSLAB), hs]
        vws = vw_ref[pl.ds(w_start, WSLAB), hs]
        s_w = jnp.where(mask_w, _dot_nt(q_aug, kws), NEG)
        m_w = jnp.max(s_w, -1, keepdims=True)
        e_w = jnp.where(mask_w, jnp.exp2(s_w - m_w), 0.0)
        l_w = jnp.sum(e_w, -1, keepdims=True)
        acc_w = _dot(e_w.astype(BF16), vws)

        outs = []
        for g in range(NSA_G):
            rs = slice(g * TQ, (g + 1) * TQ)
            hq = kvh * NSA_G + g
            g_c = gt_ref[:, hq:hq + 1]
            g_s = gt_ref[:, NSA_HEADS + hq:NSA_HEADS + hq + 1]
            g_w = gt_ref[:, 2 * NSA_HEADS + hq:2 * NSA_HEADS + hq + 1]
            outs.append(acc_c[rs] * (g_c / l_c[rs]) + acc_s[rs] * (g_s / l_s[rs])
                        + acc_w[rs] * (g_w / l_w[rs]))
        for pair in range(NSA_G // 2):
            col = (kvh * NSA_G + 2 * pair) * NSA_D
            both = jnp.where(lane_q < NSA_D, outs[2 * pair], pltpu.roll(outs[2 * pair + 1], NSA_D, 1))
            o_ref[:, col:col + LANES] = both.astype(BF16)


def _nsa(q, ks, vs, kw, vw, kc, vc, gt, ovl_t, B, S, TQ, CK):
    T = B * S
    n_blk = S // SEL_BLOCK
    n_cmp = (S - CMP_LEN) // CMP_STRIDE + 1
    WSLAB = min(WINDOW + TQ, S)
    R = NSA_G * TQ
    seq = lambda w: pl.BlockSpec((S, w), lambda b, i: (b, 0))
    qrow = lambda w: pl.BlockSpec((TQ, w), lambda b, i: (b * (S // TQ) + i, 0))
    cmp_spec = pl.BlockSpec((1, NSA_KV, kc.shape[2], LANES), lambda b, i: (b, 0, 0, 0))
    kern = functools.partial(_nsa_kernel, TQ=TQ, CK=CK, WSLAB=WSLAB, n_cmp=n_cmp, n_blk=n_blk,
                             top_n=min(SEL_TOPN, n_blk))
    return pl.pallas_call(
        kern,
        grid=(B, S // TQ),
        in_specs=[qrow(NSA_QW), seq(SLOT_W), seq(SLOT_W), seq(SLOT_W), seq(SLOT_W),
                  cmp_spec, cmp_spec, qrow(LANES), _const_spec(ovl_t.shape)],
        out_specs=qrow(NSA_QW),
        out_shape=jax.ShapeDtypeStruct((T, NSA_QW), BF16),
        scratch_shapes=[pltpu.VMEM((S // CK, R, CK), F32), pltpu.VMEM((R, LANES), F32),
                        pltpu.VMEM((R, LANES), F32)],
        compiler_params=_params(("parallel", "parallel")),
        name="nsa",
    )(q, ks, vs, kw, vw, kc, vc, gt, ovl_t)


def _merge_kernel(x_ref, a_ref, b_ref, wga_ref, wgb_ref, wua_ref, wub_ref, wo_ref, g_ref, bb_ref, o_ref):
    x = x_ref[...]
    xb = x.astype(BF16)
    m = (_sigmoid(_dot(xb, wga_ref[...])) * _dot(a_ref[...], wua_ref[...])
         + _sigmoid(_dot(xb, wgb_ref[...])) * _dot(b_ref[...], wub_ref[...]))
    mix = _dot(m.astype(BF16), wo_ref[...])
    o_ref[...] = _layer_norm(DN_ALPHA * x + mix, g_ref[...], bb_ref[...])


def _merge(x2, a, b, wga, wgb, wua, wub, wo, g, bb, tm):
    T = x2.shape[0]
    row = pl.BlockSpec((tm, D_MODEL), lambda i: (i, 0))
    wsp = _const_spec((D_MODEL, D_MODEL))
    vec = _const_spec((1, D_MODEL))
    return pl.pallas_call(
        _merge_kernel,
        grid=(T // tm,),
        in_specs=[row, row, row, wsp, wsp, wsp, wsp, wsp, vec, vec],
        out_specs=row,
        out_shape=jax.ShapeDtypeStruct((T, D_MODEL), F32),
        compiler_params=_params(("parallel",)),
        name="merge",
    )(x2, a, b, wga, wgb, wua, wub, wo, g, bb)


def _ffn_kernel(h_ref, wg_ref, wu_ref, wd_ref, g_ref, b_ref, o_ref):
    h = h_ref[...]
    hb = h.astype(BF16)
    gate = _dot(hb, wg_ref[...])
    act = (gate * _sigmoid(gate) * _dot(hb, wu_ref[...])).astype(BF16)
    ffn = _dot(act, wd_ref[...])
    o_ref[...] = _layer_norm(DN_ALPHA * h + ffn, g_ref[...], b_ref[...])


def _ffn(h, wg, wu, wd, g, b, tm):
    T = h.shape[0]
    d_ff = wg.shape[1]
    row = pl.BlockSpec((tm, D_MODEL), lambda i: (i, 0))
    vec = _const_spec((1, D_MODEL))
    once = pl.Buffered(1)
    return pl.pallas_call(
        _ffn_kernel,
        grid=(T // tm,),
        in_specs=[row,
                  pl.BlockSpec((D_MODEL, d_ff), lambda i: (0, 0), pipeline_mode=once),
                  pl.BlockSpec((D_MODEL, d_ff), lambda i: (0, 0), pipeline_mode=once),
                  pl.BlockSpec((d_ff, D_MODEL), lambda i: (0, 0), pipeline_mode=once),
                  vec, vec],
        out_specs=row,
        out_shape=jax.ShapeDtypeStruct((T, D_MODEL), F32),
        compiler_params=_params(("parallel",)),
        name="ffn",
    )(h, wg, wu, wd, g, b)


def _pad_slots(w):
    d = w.shape[0]
    w = w.reshape(d, NSA_KV, NSA_D)
    return jnp.pad(w, ((0, 0), (0, 0), (0, LANES - NSA_D))).reshape(d, SLOT_W)


def _overlap_t(n_cmp, n_blk):
    cs = np.arange(n_cmp)[None, :] * CMP_STRIDE
    bs = np.arange(n_blk)[:, None] * SEL_BLOCK
    ov = np.minimum(cs + CMP_LEN, bs + SEL_BLOCK) - np.maximum(cs, bs)
    out = np.zeros((n_blk, LANES), np.float32)
    out[:, :n_cmp] = np.clip(ov, 0, None) / CMP_LEN
    return out


def kernel(x, positions, w_in, hg_lb_logits, hg_norm_g, cmp_k_pos, cmp_k_w1, cmp_k_b1, cmp_k_w2, cmp_k_b2, cmp_v_pos, cmp_v_w1, cmp_v_b1, cmp_v_w2, cmp_v_b2, w_up_hg, w_up_nsa, w_o, ln1_g, ln1_b, w_ffn_gate, w_ffn_up, w_ffn_down, ln2_g, ln2_b):
    B, S, _ = x.shape
    T = B * S
    tm = min(512, S)
    TQ = 128
    CK = min(256, S)
    n_blk = S // SEL_BLOCK
    n_cmp = (S - CMP_LEN) // CMP_STRIDE + 1
    n_half = S // CMP_STRIDE

    lb_table = jnp.cumsum(jax.nn.softmax(hg_lb_logits.astype(F32), axis=0), axis=0)
    x2 = x.reshape(T, D_MODEL)

    w = w_in[0]
    o = 0
    hg_cols = w[:, o:o + 4 * HG_W]; o += 4 * HG_W
    wq = w[:, o:o + NSA_QW]; o += NSA_QW
    kv = []
    for _ in range(6):
        kv.append(w[:, o:o + NSA_KVW]); o += NSA_KVW
    w_kc, w_vc, w_ks, w_vs, w_kw, w_vw = kv
    w_gt = jnp.pad(w[:, o:o + 3 * NSA_HEADS], ((0, 0), (0, LANES - 3 * NSA_HEADS))); o += 3 * NSA_HEADS
    w_ga = w[:, o:o + D_MODEL]; o += D_MODEL
    w_gb = w[:, o:o + D_MODEL]
    w_nsa = jnp.concatenate([wq, w_kc, _pad_slots(w_ks), _pad_slots(w_kw),
                             w_vc, _pad_slots(w_vs), _pad_slots(w_vw), w_gt], axis=1).astype(BF16)

    half = NSA_D // 2
    inv_freq = ROPE_THETA ** (-jnp.arange(half, dtype=F32) / half)
    ang = positions.astype(F32).reshape(T, 1) * inv_freq[None, :]
    cos_t = jnp.tile(jnp.cos(ang), (1, LANES // half))
    sin_h = jnp.sin(ang)
    sin_t = jnp.tile(jnp.concatenate([-sin_h, sin_h], axis=1), (1, LANES // NSA_D))

    e_np = np.zeros((S, NSA_KV, LANES), np.float32)
    e_np[np.arange(S), :, NSA_D + np.arange(S) // SEL_BLOCK] = 1.0
    e_tab = jnp.asarray(e_np.reshape(S, SLOT_W), BF16)

    hq, hf, hi, hg = _hg_proj(x2, hg_cols.astype(BF16), tm)
    a = _hgrn(hq, hf, hi, hg, lb_table[0:1], hg_norm_g[0:1].astype(F32), B, S)

    q, kc_tok, ks, kw, vc_tok, vs, vw, gt = _nsa_proj(x2, w_nsa, cos_t, sin_t, e_tab, tm, S)

    def half_blocks(t):
        t = t.reshape(B, n_half, CMP_STRIDE, NSA_KV, NSA_D)
        return t.transpose(0, 3, 1, 2, 4).reshape(B, NSA_KV, n_half, CMP_STRIDE * NSA_D)

    def pos_rows(p):
        return jnp.broadcast_to(p.reshape(1, CMP_LEN * NSA_D), (8, CMP_LEN * NSA_D)).astype(BF16)

    def pad_out(w2, b2):
        return (jnp.pad(w2, ((0, 0), (0, LANES - NSA_D))).astype(BF16),
                jnp.pad(b2, (0, LANES - NSA_D)).reshape(1, LANES).astype(F32))

    w2k, b2k = pad_out(cmp_k_w2[0], cmp_k_b2[0])
    w2v, b2v = pad_out(cmp_v_w2[0], cmp_v_b2[0])
    kc, vc = _compress(
        half_blocks(kc_tok), half_blocks(vc_tok),
        pos_rows(cmp_k_pos[0]), cmp_k_w1[0].astype(BF16), cmp_k_b1[0].reshape(1, -1).astype(F32), w2k, b2k,
        pos_rows(cmp_v_pos[0]), cmp_v_w1[0].astype(BF16), cmp_v_b1[0].reshape(1, -1).astype(F32), w2v, b2v)

    ovl_t = jnp.asarray(_overlap_t(n_cmp, n_blk), BF16)
    b_out = _nsa(q, ks, vs, kw, vw, kc, vc, gt, ovl_t, B, S, TQ, CK)

    h1 = _merge(x2, a, b_out, w_ga.astype(BF16), w_gb.astype(BF16),
                w_up_hg[0].astype(BF16), w_up_nsa[0].astype(BF16), w_o[0].astype(BF16),
                ln1_g[0].reshape(1, -1).astype(F32), ln1_b[0].reshape(1, -1).astype(F32), tm)
    out = _ffn(h1, w_ffn_gate[0].astype(BF16), w_ffn_up[0].astype(BF16), w_ffn_down[0].astype(BF16),
               ln2_g[0].reshape(1, -1).astype(F32), ln2_b[0].reshape(1, -1).astype(F32), tm)
    return out.reshape(B, S, D_MODEL)
```

```python
import functools

import numpy as np
import jax
import jax.numpy as jnp
from jax import lax
from jax.experimental import pallas as pl
from jax.experimental.pallas import tpu as pltpu

F32 = jnp.float32
BF16 = jnp.bfloat16

D_MODEL = 1024
HG_HEADS = 8
HG_DK = 128
HG_DV = 128
HG_W = HG_HEADS * HG_DK
HG_CHUNK = 64
NSA_HEADS = 16
NSA_KV = 4
NSA_G = 4
NSA_D = 64
NSA_QW = NSA_HEADS * NSA_D
NSA_KVW = NSA_KV * NSA_D
CMP_LEN = 32
CMP_STRIDE = 16
CMP_HIDDEN = 256
SEL_BLOCK = 64
SEL_TOPN = 8
SEL_FORCE = 1000.0
WINDOW = 512
ROPE_THETA = 10000.0
DEPTH = 1
DN_ALPHA = (2.0 * DEPTH) ** 0.25
LN_EPS = 1e-5
RMS_EPS = 1e-6
LOG2E = 1.4426950408889634
NEG = -1e30

LANES = 128
SLOT_W = NSA_KV * LANES
VMEM_LIMIT = 56 * 1024 * 1024


def _dot(a, b):
    return jnp.dot(a, b, preferred_element_type=F32)


def _dot_nt(a, b):
    return lax.dot_general(a, b, (((1,), (1,)), ((), ())), preferred_element_type=F32)


def _dot_tn(a, b):
    return lax.dot_general(a, b, (((0,), (0,)), ((), ())), preferred_element_type=F32)


def _sigmoid(x):
    return 1.0 / (1.0 + jnp.exp(-x))


def _layer_norm(x, g, b):
    mu = jnp.mean(x, -1, keepdims=True)
    xc = x - mu
    var = jnp.mean(xc * xc, -1, keepdims=True)
    return xc * lax.rsqrt(var + LN_EPS) * g + b


def _params(sem):
    return pltpu.CompilerParams(dimension_semantics=sem, vmem_limit_bytes=VMEM_LIMIT)


def _const_spec(shape):
    nd = len(shape)
    return pl.BlockSpec(shape, lambda *_: (0,) * nd)


def _hg_proj_kernel(x_ref, w_ref, q_ref, f_ref, i_ref, g_ref):
    x = x_ref[...].astype(BF16)
    q_ref[...] = _dot(x, w_ref[:, 0:HG_W]).astype(BF16)
    f_ref[...] = _dot(x, w_ref[:, HG_W:2 * HG_W])
    i_ref[...] = _dot(x, w_ref[:, 2 * HG_W:3 * HG_W]).astype(BF16)
    g = _dot(x, w_ref[:, 3 * HG_W:4 * HG_W])
    g_ref[...] = (g * _sigmoid(g)).astype(BF16)


def _hg_proj(x2, w_hg, tm):
    T = x2.shape[0]
    row = lambda i: (i, 0)
    out_spec = pl.BlockSpec((tm, HG_W), row)
    return pl.pallas_call(
        _hg_proj_kernel,
        grid=(T // tm,),
        in_specs=[pl.BlockSpec((tm, D_MODEL), row), _const_spec((D_MODEL, 4 * HG_W))],
        out_specs=[out_spec] * 4,
        out_shape=[jax.ShapeDtypeStruct((T, HG_W), BF16), jax.ShapeDtypeStruct((T, HG_W), F32),
                   jax.ShapeDtypeStruct((T, HG_W), BF16), jax.ShapeDtypeStruct((T, HG_W), BF16)],
        compiler_params=_params(("parallel",)),
        name="hg_proj",
    )(x2, w_hg)


_NQ = NSA_QW
_NKC = NSA_KVW
_NSL = SLOT_W
_ROPE_W = _NQ + _NKC + 2 * _NSL
_O_KC = _NQ
_O_KS = _NQ + _NKC
_O_KW = _O_KS + _NSL
_O_VC = _ROPE_W
_O_VS = _O_VC + _NKC
_O_VW = _O_VS + _NSL
_O_GT = _O_VW + _NSL
_NSA_W = _O_GT + LANES


def _nsa_proj_kernel(x_ref, w_ref, cos_ref, sin_ref, e_ref,
                     q_ref, kc_ref, ks_ref, kw_ref, vc_ref, vs_ref, vw_ref, gt_ref):
    x = x_ref[...].astype(BF16)
    cos = cos_ref[...]
    sin = sin_ref[...]
    half = NSA_D // 2

    def rope(y):
        w = y.shape[1]
        reps = w // LANES
        lane = lax.broadcasted_iota(jnp.int32, y.shape, 1)
        fwd = pltpu.roll(y, w - half, 1)
        bwd = pltpu.roll(y, half, 1)
        rot = jnp.where((lane % NSA_D) < half, fwd, bwd)
        return y * jnp.tile(cos, (1, reps)) + rot * jnp.tile(sin, (1, reps))

    yq = rope(_dot(x, w_ref[:, 0:_NQ]))
    q_ref[...] = (yq * (NSA_D ** -0.5 * LOG2E)).astype(BF16)
    kc_ref[...] = rope(_dot(x, w_ref[:, _O_KC:_O_KC + _NKC])).astype(BF16)
    ks_ref[...] = rope(_dot(x, w_ref[:, _O_KS:_O_KS + _NSL])).astype(BF16) + e_ref[...]
    kw_ref[...] = rope(_dot(x, w_ref[:, _O_KW:_O_KW + _NSL])).astype(BF16)
    vc_ref[...] = _dot(x, w_ref[:, _O_VC:_O_VC + _NKC]).astype(BF16)
    vs_ref[...] = _dot(x, w_ref[:, _O_VS:_O_VS + _NSL]).astype(BF16)
    vw_ref[...] = _dot(x, w_ref[:, _O_VW:_O_VW + _NSL]).astype(BF16)
    gt_ref[...] = _sigmoid(_dot(x, w_ref[:, _O_GT:_O_GT + LANES]))


def _nsa_proj(x2, w_nsa, cos_t, sin_t, e_tab, tm, S):
    T = x2.shape[0]
    row = lambda i: (i, 0)
    tiles_per_seq = S // tm
    widths = [_NQ, _NKC, _NSL, _NSL, _NKC, _NSL, _NSL]
    return pl.pallas_call(
        _nsa_proj_kernel,
        grid=(T // tm,),
        in_specs=[pl.BlockSpec((tm, D_MODEL), row), _const_spec((D_MODEL, _NSA_W)),
                  pl.BlockSpec((tm, LANES), row), pl.BlockSpec((tm, LANES), row),
                  pl.BlockSpec((tm, _NSL), lambda i: (i % tiles_per_seq, 0))],
        out_specs=[pl.BlockSpec((tm, w), row) for w in widths] + [pl.BlockSpec((tm, LANES), row)],
        out_shape=[jax.ShapeDtypeStruct((T, w), BF16) for w in widths]
                  + [jax.ShapeDtypeStruct((T, LANES), F32)],
        compiler_params=_params(("parallel",)),
        name="nsa_proj",
    )(x2, w_nsa, cos_t, sin_t, e_tab)


def _hgrn_kernel(q_ref, f_ref, i_ref, g_ref, lb_ref, ng_ref, o_ref, *, n_chunks, heads, unroll):
    C = HG_CHUNK
    ng = ng_ref[...]
    r = lax.broadcasted_iota(jnp.int32, (C, C), 0)
    c = lax.broadcasted_iota(jnp.int32, (C, C), 1)
    tril = r >= c
    row = lax.broadcasted_iota(jnp.int32, (C, HG_DK), 0)

    def chunk_cumsum(x):
        d = 1
        while d < C:
            x = x + jnp.where(row >= d, pltpu.roll(x, d, 0), 0.0)
            d *= 2
        return x

    def one(ci, h, st_t):
        sl = pl.ds(pl.multiple_of(ci * C, C), C)
        hs = slice(h * HG_DK, (h + 1) * HG_DK)
        lb = lb_ref[:, hs]
        f = lb + (1.0 - lb) * _sigmoid(f_ref[sl, hs])
        b = chunk_cumsum(jnp.log(f))
        k = 1.0 - f
        eb = jnp.exp(b)
        decay = eb[C - 1:C, :]
        q_dec = (q_ref[sl, hs].astype(F32) * eb).astype(BF16)
        k_inv_f = k * (1.0 / eb)
        k_inv = k_inv_f.astype(BF16)
        k_end = (k_inv_f * decay).astype(BF16)
        v = i_ref[sl, hs]
        attn = jnp.where(tril, _dot_nt(q_dec, k_inv), 0.0).astype(BF16)
        o = _dot(attn, v) + _dot_nt(q_dec, st_t.astype(BF16))
        st_t = st_t * decay + _dot_tn(v, k_end)
        ms = jnp.mean(o * o, -1, keepdims=True)
        on = o * lax.rsqrt(ms + RMS_EPS) * ng
        o_ref[sl, hs] = (on * g_ref[sl, hs].astype(F32)).astype(BF16)
        return st_t

    def body(cj, states):
        for u in range(unroll):
            states = tuple(one(cj * unroll + u, h, states[h]) for h in range(heads))
        return states

    init = tuple(jnp.zeros((HG_DV, HG_DK), F32) for _ in range(heads))
    lax.fori_loop(0, n_chunks // unroll, body, init)


def _hgrn(q, f, i, g, lb, ng, B, S, heads=4, unroll=2):
    T = B * S
    w = heads * HG_DK
    blk = pl.BlockSpec((S, w), lambda b, h: (b, h))
    return pl.pallas_call(
        functools.partial(_hgrn_kernel, n_chunks=S // HG_CHUNK, heads=heads, unroll=unroll),
        grid=(B, HG_HEADS // heads),
        in_specs=[blk, blk, blk, blk,
                  pl.BlockSpec((1, w), lambda b, h: (0, h)),
                  pl.BlockSpec((1, HG_DV), lambda b, h: (0, 0))],
        out_specs=blk,
        out_shape=jax.ShapeDtypeStruct((T, HG_W), BF16),
        compiler_params=_params(("parallel", "parallel")),
        name="hgrn",
    )(q, f, i, g, lb, ng)


def _compress_kernel(uk_ref, uv_ref, pk_ref, w1k_ref, b1k_ref, w2k_ref, b2k_ref,
                     pv_ref, w1v_ref, b1v_ref, w2v_ref, b2v_ref, ko_ref, vo_ref):
    half = CMP_STRIDE * NSA_D

    def mlp(u_ref, pos_ref, w1_ref, b1_ref, w2_ref, b2_ref, o_ref):
        w_top = w1_ref[0:half, :]
        w_bot = w1_ref[half:2 * half, :]
        base = _dot(pos_ref[...], w1_ref[...])[0:1, :] + b1_ref[...]
        for h in range(NSA_KV):
            u = u_ref[0, h]
            top = _dot(u, w_top)
            bot = _dot(u, w_bot)
            n = top.shape[0]
            bot_next = jnp.concatenate([bot[1:n, :], jnp.zeros((1, CMP_HIDDEN), F32)], axis=0)
            hid = top + bot_next + base
            hid = hid * _sigmoid(hid)
            o_ref[0, h] = (_dot(hid.astype(BF16), w2_ref[...]) + b2_ref[...]).astype(BF16)

    mlp(uk_ref, pk_ref, w1k_ref, b1k_ref, w2k_ref, b2k_ref, ko_ref)
    mlp(uv_ref, pv_ref, w1v_ref, b1v_ref, w2v_ref, b2v_ref, vo_ref)


def _compress(uk, uv, pk, w1k, b1k, w2k, b2k, pv, w1v, b1v, w2v, b2v):
    B, _, nh, hw = uk.shape
    u_spec = pl.BlockSpec((1, NSA_KV, nh, hw), lambda b: (b, 0, 0, 0))
    o_spec = pl.BlockSpec((1, NSA_KV, nh, LANES), lambda b: (b, 0, 0, 0))
    wspecs = [_const_spec(a.shape) for a in (pk, w1k, b1k, w2k, b2k)]
    return pl.pallas_call(
        _compress_kernel,
        grid=(B,),
        in_specs=[u_spec, u_spec] + wspecs + wspecs,
        out_specs=[o_spec, o_spec],
        out_shape=[jax.ShapeDtypeStruct((B, NSA_KV, nh, LANES), BF16)] * 2,
        compiler_params=_params(("parallel",)),
        name="compress",
    )(uk, uv, pk, w1k, b1k, w2k, b2k, pv, w1v, b1v, w2v, b2v)


def _nsa_kernel(q_ref, ks_ref, vs_ref, kw_ref, vw_ref, kc_ref, vc_ref, gt_ref, ovl_ref,
                o_ref, sbuf, acc_ref, lt_ref, *, TQ, CK, WSLAB, n_cmp, n_blk, top_n):
    R = NSA_G * TQ
    q0 = pl.program_id(1) * TQ
    lane_q = lax.broadcasted_iota(jnp.int32, (TQ, LANES), 1)
    t_row = q0 + (lax.broadcasted_iota(jnp.int32, (R, 1), 0) & (TQ - 1))
    t_lane = q0 + (lax.broadcasted_iota(jnp.int32, (1, R), 1) & (TQ - 1))

    n_l = lax.broadcasted_iota(jnp.int32, (R, LANES), 1)
    mask_c = (n_l * CMP_STRIDE + (CMP_LEN - 1) <= t_row) & (n_l < n_cmp)
    n_s = lax.broadcasted_iota(jnp.int32, (LANES, R), 0)
    mask_ct = (n_s * CMP_STRIDE + (CMP_LEN - 1) <= t_lane) & (n_s < n_cmp)

    j_blk = lax.broadcasted_iota(jnp.int32, (n_blk, TQ), 0)
    cur = (q0 + lax.broadcasted_iota(jnp.int32, (n_blk, TQ), 1)) // SEL_BLOCK
    blk_ok = j_blk <= cur
    forced = (j_blk == 0) | (j_blk == cur) | (j_blk == cur - 1)

    n_sel_chunks = (q0 + TQ + CK - 1) // CK
    w_start = pl.multiple_of(jnp.maximum(q0 + TQ - WSLAB, 0), TQ)
    key_w = w_start + lax.broadcasted_iota(jnp.int32, (R, WSLAB), 1)
    mask_w = (key_w <= t_row) & (key_w > t_row - WINDOW)
    key_l = lax.broadcasted_iota(jnp.int32, (R, CK), 1)

    def fold(x):
        return [x[:, u * LANES:(u + 1) * LANES] for u in range(x.shape[1] // LANES)]

    for kvh in range(NSA_KV):
        hs = slice(kvh * LANES, (kvh + 1) * LANES)
        q_slabs = []
        for pair in range(NSA_G // 2):
            col = (kvh * NSA_G + 2 * pair) * NSA_D
            x = q_ref[:, col:col + LANES].astype(F32)
            q_slabs.append(x)
            q_slabs.append(pltpu.roll(x, NSA_D, 1))
        q_plain = jnp.concatenate(
            [jnp.where(lane_q < NSA_D, x, 0.0).astype(BF16) for x in q_slabs], axis=0)

        kc = kc_ref[0, kvh]
        vc = vc_ref[0, kvh]
        s_c = jnp.where(mask_c, _dot_nt(q_plain, kc), NEG)
        m_c = jnp.max(s_c, -1, keepdims=True)
        e_c = jnp.where(mask_c, jnp.exp2(s_c - m_c), 0.0)
        l_c = jnp.maximum(jnp.sum(e_c, -1, keepdims=True), 1e-30)
        acc_c = _dot(e_c.astype(BF16), vc)

        s_ct = jnp.where(mask_ct, _dot_nt(kc, q_plain), NEG)
        m_ct = jnp.max(s_ct, 0, keepdims=True)
        e_ct = jnp.where(mask_ct, jnp.exp2(s_ct - m_ct), 0.0)
        p_ct = e_ct / jnp.maximum(jnp.sum(e_ct, 0, keepdims=True), 1e-30)
        imp4 = _dot(ovl_ref[...], p_ct.astype(BF16))
        imp = imp4[:, 0:TQ]
        for g in range(1, NSA_G):
            imp = imp + imp4[:, g * TQ:(g + 1) * TQ]
        score = jnp.where(blk_ok, imp + jnp.where(forced, SEL_FORCE, 0.0), -1.0)
        rank = jnp.zeros((n_blk, TQ), jnp.int32)
        for i in range(n_blk):
            ri = score[i:i + 1, :]
            beats = (ri > score) | ((ri == score) & (j_blk > i))
            rank = rank + beats.astype(jnp.int32)
        sel_t = ((rank < top_n) & blk_ok).astype(F32)
        if n_blk < LANES:
            sel_t = jnp.concatenate([sel_t, jnp.zeros((LANES - n_blk, TQ), F32)], axis=0)
        sel = pltpu.roll(sel_t.T, NSA_D, 1)
        bias = jnp.where((lane_q >= NSA_D) & (lane_q < NSA_D + n_blk) & (sel < 0.5), NEG, 0.0)
        q_aug = jnp.concatenate(
            [jnp.where(lane_q < NSA_D, x, bias).astype(BF16) for x in q_slabs], axis=0)

        def scores(ci, masked):
            kch = ks_ref[pl.ds(pl.multiple_of(ci * CK, CK), CK), hs]
            s = _dot_nt(q_aug, kch)
            if masked:
                s = jnp.where(ci * CK + key_l <= t_row, s, NEG)
            sbuf[ci] = s
            return functools.reduce(jnp.maximum, fold(s))

        def p1(ci, mt):
            return jnp.maximum(mt, scores(ci, False))

        mt = lax.fori_loop(0, n_sel_chunks - 1, p1, jnp.full((R, LANES), -3e38, F32))
        mt = jnp.maximum(mt, scores(n_sel_chunks - 1, True))
        m_s = jnp.max(mt, -1, keepdims=True)

        acc_ref[...] = jnp.zeros((R, LANES), F32)
        lt_ref[...] = jnp.zeros((R, LANES), F32)

        def p2(ci, carry):
            p = jnp.exp2(sbuf[ci] - m_s)
            lt_ref[...] += functools.reduce(jnp.add, fold(p))
            vch = vs_ref[pl.ds(pl.multiple_of(ci * CK, CK), CK), hs]
            acc_ref[...] += _dot(p.astype(BF16), vch)
            return carry

        lax.fori_loop(0, n_sel_chunks, p2, 0)
        l_s = jnp.sum(lt_ref[...], -1, keepdims=True)
        acc_s = acc_ref[...]

        kws = kw_ref[pl.ds(w_start, WSLAB), hs]
        vws = vw_ref[pl.ds(w_start, WSLAB), hs]
        s_w = jnp.where(mask_w, _dot_nt(q_aug, kws), NEG)
        m_w = jnp.max(s_w, -1, keepdims=True)
        e_w = jnp.where(mask_w, jnp.exp2(s_w - m_w), 0.0)
        l_w = jnp.sum(e_w, -1, keepdims=True)
        acc_w = _dot(e_w.astype(BF16), vws)

        outs = []
        for g in range(NSA_G):
            rs = slice(g * TQ, (g + 1) * TQ)
            hq = kvh * NSA_G + g
            g_c = gt_ref[:, hq:hq + 1]
            g_s = gt_ref[:, NSA_HEADS + hq:NSA_HEADS + hq + 1]
            g_w = gt_ref[:, 2 * NSA_HEADS + hq:2 * NSA_HEADS + hq + 1]
            outs.append(acc_c[rs] * (g_c / l_c[rs]) + acc_s[rs] * (g_s / l_s[rs])
                        + acc_w[rs] * (g_w / l_w[rs]))
        for pair in range(NSA_G // 2):
            col = (kvh * NSA_G + 2 * pair) * NSA_D
            both = jnp.where(lane_q < NSA_D, outs[2 * pair], pltpu.roll(outs[2 * pair + 1], NSA_D, 1))
            o_ref[:, col:col + LANES] = both.astype(BF16)


def _nsa(q, ks, vs, kw, vw, kc, vc, gt, ovl_t, B, S, TQ, CK):
    T = B * S
    n_blk = S // SEL_BLOCK
    n_cmp = (S - CMP_LEN) // CMP_STRIDE + 1
    WSLAB = min(WINDOW + TQ, S)
    R = NSA_G * TQ
    seq = lambda w: pl.BlockSpec((S, w), lambda b, i: (b, 0))
    qrow = lambda w: pl.BlockSpec((TQ, w), lambda b, i: (b * (S // TQ) + i, 0))
    cmp_spec = pl.BlockSpec((1, NSA_KV, kc.shape[2], LANES), lambda b, i: (b, 0, 0, 0))
    kern = functools.partial(_nsa_kernel, TQ=TQ, CK=CK, WSLAB=WSLAB, n_cmp=n_cmp, n_blk=n_blk,
                             top_n=min(SEL_TOPN, n_blk))
    return pl.pallas_call(
        kern,
        grid=(B, S // TQ),
        in_specs=[qrow(NSA_QW), seq(SLOT_W), seq(SLOT_W), seq(SLOT_W), seq(SLOT_W),
                  cmp_spec, cmp_spec, qrow(LANES), _const_spec(ovl_t.shape)],
        out_specs=qrow(NSA_QW),
        out_shape=jax.ShapeDtypeStruct((T, NSA_QW), BF16),
        scratch_shapes=[pltpu.VMEM((S // CK, R, CK), F32), pltpu.VMEM((R, LANES), F32),
                        pltpu.VMEM((R, LANES), F32)],
        compiler_params=_params(("parallel", "parallel")),
        name="nsa",
    )(q, ks, vs, kw, vw, kc, vc, gt, ovl_t)


def _merge_kernel(x_ref, a_ref, b_ref, wga_ref, wgb_ref, wua_ref, wub_ref, wo_ref, g_ref, bb_ref, o_ref):
    x = x_ref[...]
    xb = x.astype(BF16)
    m = (_sigmoid(_dot(xb, wga_ref[...])) * _dot(a_ref[...], wua_ref[...])
         + _sigmoid(_dot(xb, wgb_ref[...])) * _dot(b_ref[...], wub_ref[...]))
    mix = _dot(m.astype(BF16), wo_ref[...])
    o_ref[...] = _layer_norm(DN_ALPHA * x + mix, g_ref[...], bb_ref[...])


def _merge(x2, a, b, wga, wgb, wua, wub, wo, g, bb, tm):
    T = x2.shape[0]
    row = pl.BlockSpec((tm, D_MODEL), lambda i: (i, 0))
    wsp = _const_spec((D_MODEL, D_MODEL))
    vec = _const_spec((1, D_MODEL))
    return pl.pallas_call(
        _merge_kernel,
        grid=(T // tm,),
        in_specs=[row, row, row, wsp, wsp, wsp, wsp, wsp, vec, vec],
        out_specs=row,
        out_shape=jax.ShapeDtypeStruct((T, D_MODEL), F32),
        compiler_params=_params(("parallel",)),
        name="merge",
    )(x2, a, b, wga, wgb, wua, wub, wo, g, bb)


def _ffn_kernel(h_ref, wg_ref, wu_ref, wd_ref, g_ref, b_ref, o_ref):
    h = h_ref[...]
    hb = h.astype(BF16)
    gate = _dot(hb, wg_ref[...])
    act = (gate * _sigmoid(gate) * _dot(hb, wu_ref[...])).astype(BF16)
    ffn = _dot(act, wd_ref[...])
    o_ref[...] = _layer_norm(DN_ALPHA * h + ffn, g_ref[...], b_ref[...])


def _ffn(h, wg, wu, wd, g, b, tm):
    T = h.shape[0]
    d_ff = wg.shape[1]
    row = pl.BlockSpec((tm, D_MODEL), lambda i: (i, 0))
    vec = _const_spec((1, D_MODEL))
    once = pl.Buffered(1)
    return pl.pallas_call(
        _ffn_kernel,
        grid=(T // tm,),
        in_specs=[row,
                  pl.BlockSpec((D_MODEL, d_ff), lambda i: (0, 0), pipeline_mode=once),
                  pl.BlockSpec((D_MODEL, d_ff), lambda i: (0, 0), pipeline_mode=once),
                  pl.BlockSpec((d_ff, D_MODEL), lambda i: (0, 0), pipeline_mode=once),
                  vec, vec],
        out_specs=row,
        out_shape=jax.ShapeDtypeStruct((T, D_MODEL), F32),
        compiler_params=_params(("parallel",)),
        name="ffn",
    )(h, wg, wu, wd, g, b)


def _pad_slots(w):
    d = w.shape[0]
    w = w.reshape(d, NSA_KV, NSA_D)
    return jnp.pad(w, ((0, 0), (0, 0), (0, LANES - NSA_D))).reshape(d, SLOT_W)


def _overlap_t(n_cmp, n_blk):
    cs = np.arange(n_cmp)[None, :] * CMP_STRIDE
    bs = np.arange(n_blk)[:, None] * SEL_BLOCK
    ov = np.minimum(cs + CMP_LEN, bs + SEL_BLOCK) - np.maximum(cs, bs)
    out = np.zeros((n_blk, LANES), np.float32)
    out[:, :n_cmp] = np.clip(ov, 0, None) / CMP_LEN
    return out


def kernel(x, positions, w_in, hg_lb_logits, hg_norm_g, cmp_k_pos, cmp_k_w1, cmp_k_b1, cmp_k_w2, cmp_k_b2, cmp_v_pos, cmp_v_w1, cmp_v_b1, cmp_v_w2, cmp_v_b2, w_up_hg, w_up_nsa, w_o, ln1_g, ln1_b, w_ffn_gate, w_ffn_up, w_ffn_down, ln2_g, ln2_b):
    B, S, _ = x.shape
    T = B * S
    tm = min(512, S)
    TQ = 128
    CK = min(256, S)
    n_blk = S // SEL_BLOCK
    n_cmp = (S - CMP_LEN) // CMP_STRIDE + 1
    n_half = S // CMP_STRIDE

    lb_table = jnp.cumsum(jax.nn.softmax(hg_lb_logits.astype(F32), axis=0), axis=0)
    x2 = x.reshape(T, D_MODEL)

    w = w_in[0]
    o = 0
    hg_cols = w[:, o:o + 4 * HG_W]; o += 4 * HG_W
    wq = w[:, o:o + NSA_QW]; o += NSA_QW
    kv = []
    for _ in range(6):
        kv.append(w[:, o:o + NSA_KVW]); o += NSA_KVW
    w_kc, w_vc, w_ks, w_vs, w_kw, w_vw = kv
    w_gt = jnp.pad(w[:, o:o + 3 * NSA_HEADS], ((0, 0), (0, LANES - 3 * NSA_HEADS))); o += 3 * NSA_HEADS
    w_ga = w[:, o:o + D_MODEL]; o += D_MODEL
    w_gb = w[:, o:o + D_MODEL]
    w_nsa = jnp.concatenate([wq, w_kc, _pad_slots(w_ks), _pad_slots(w_kw),
                             w_vc, _pad_slots(w_vs), _pad_slots(w_vw), w_gt], axis=1).astype(BF16)

    half = NSA_D // 2
    inv_freq = ROPE_THETA ** (-jnp.arange(half, dtype=F32) / half)
    ang = positions.astype(F32).reshape(T, 1) * inv_freq[None, :]
    cos_t = jnp.tile(jnp.cos(ang), (1, LANES // half))
    sin_h = jnp.sin(ang)
    sin_t = jnp.tile(jnp.concatenate([-sin_h, sin_h], axis=1), (1, LANES // NSA_D))

    e_np = np.zeros((S, NSA_KV, LANES), np.float32)
    e_np[np.arange(S), :, NSA_D + np.arange(S) // SEL_BLOCK] = 1.0
    e_tab = jnp.asarray(e_np.reshape(S, SLOT_W), BF16)

    hq, hf, hi, hg = _hg_proj(x2, hg_cols.astype(BF16), tm)
    a = _hgrn(hq, hf, hi, hg, lb_table[0:1], hg_norm_g[0:1].astype(F32), B, S)

    q, kc_tok, ks, kw, vc_tok, vs, vw, gt = _nsa_proj(x2, w_nsa, cos_t, sin_t, e_tab, tm, S)

    def half_blocks(t):
        t = t.reshape(B, n_half, CMP_STRIDE, NSA_KV, NSA_D)
        return t.transpose(0, 3, 1, 2, 4).reshape(B, NSA_KV, n_half, CMP_STRIDE * NSA_D)

    def pos_rows(p):
        return jnp.broadcast_to(p.reshape(1, CMP_LEN * NSA_D), (8, CMP_LEN * NSA_D)).astype(BF16)

    def pad_out(w2, b2):
        return (jnp.pad(w2, ((0, 0), (0, LANES - NSA_D))).astype(BF16),
                jnp.pad(b2, (0, LANES - NSA_D)).reshape(1, LANES).astype(F32))

    w2k, b2k = pad_out(cmp_k_w2[0], cmp_k_b2[0])
    w2v, b2v = pad_out(cmp_v_w2[0], cmp_v_b2[0])
    kc, vc = _compress(
        half_blocks(kc_tok), half_blocks(vc_tok),
        pos_rows(cmp_k_pos[0]), cmp_k_w1[0].astype(BF16), cmp_k_b1[0].reshape(1, -1).astype(F32), w2k, b2k,
        pos_rows(cmp_v_pos[0]), cmp_v_w1[0].astype(BF16), cmp_v_b1[0].reshape(1, -1).astype(F32), w2v, b2v)

    ovl_t = jnp.asarray(_overlap_t(n_cmp, n_blk), BF16)
    b_out = _nsa(q, ks, vs, kw, vw, kc, vc, gt, ovl_t, B, S, TQ, CK)

    h1 = _merge(x2, a, b_out, w_ga.astype(BF16), w_gb.astype(BF16),
                w_up_hg[0].astype(BF16), w_up_nsa[0].astype(BF16), w_o[0].astype(BF16),
                ln1_g[0].reshape(1, -1).astype(F32), ln1_b[0].reshape(1, -1).astype(F32), tm)
    out = _ffn(h1, w_ffn_gate[0].astype(BF16), w_ffn_up[0].astype(BF16), w_ffn_down[0].astype(BF16),
               ln2_g[0].reshape(1, -1).astype(F32), ln2_b[0].reshape(1, -1).astype(F32), tm)
    return out.reshape(B, S, D_MODEL)
```

```python
import functools

import numpy as np
import jax
import jax.numpy as jnp
from jax import lax
from jax.experimental import pallas as pl
from jax.experimental.pallas import tpu as pltpu

F32 = jnp.float32
BF16 = jnp.bfloat16

D_MODEL = 1024
HG_HEADS = 8
HG_DK = 128
HG_DV = 128
HG_W = HG_HEADS * HG_DK
HG_CHUNK = 64
NSA_HEADS = 16
NSA_KV = 4
NSA_G = 4
NSA_D = 64
NSA_QW = NSA_HEADS * NSA_D
NSA_KVW = NSA_KV * NSA_D
CMP_LEN = 32
CMP_STRIDE = 16
CMP_HIDDEN = 256
SEL_BLOCK = 64
SEL_TOPN = 8
SEL_FORCE = 1000.0
WINDOW = 512
ROPE_THETA = 10000.0
DEPTH = 1
DN_ALPHA = (2.0 * DEPTH) ** 0.25
LN_EPS = 1e-5
RMS_EPS = 1e-6
LOG2E = 1.4426950408889634
NEG = -1e30

LANES = 128
SLOT_W = NSA_KV * LANES
VSLOT = 2 * LANES
VSLOT_W = NSA_KV * VSLOT
VMEM_LIMIT = 56 * 1024 * 1024


def _dot(a, b):
    return jnp.dot(a, b, preferred_element_type=F32)


def _dot_nt(a, b):
    return lax.dot_general(a, b, (((1,), (1,)), ((), ())), preferred_element_type=F32)


def _dot_tn(a, b):
    return lax.dot_general(a, b, (((0,), (0,)), ((), ())), preferred_element_type=F32)


def _sigmoid(x):
    return 1.0 / (1.0 + jnp.exp(-x))


def _layer_norm(x, g, b):
    mu = jnp.mean(x, -1, keepdims=True)
    xc = x - mu
    var = jnp.mean(xc * xc, -1, keepdims=True)
    return xc * lax.rsqrt(var + LN_EPS) * g + b


def _params(sem):
    return pltpu.CompilerParams(dimension_semantics=sem, vmem_limit_bytes=VMEM_LIMIT)


def _const_spec(shape):
    nd = len(shape)
    return pl.BlockSpec(shape, lambda *_: (0,) * nd)


def _hg_proj_kernel(x_ref, w_ref, q_ref, f_ref, i_ref, g_ref):
    x = x_ref[...].astype(BF16)
    q_ref[...] = _dot(x, w_ref[:, 0:HG_W]).astype(BF16)
    f_ref[...] = _dot(x, w_ref[:, HG_W:2 * HG_W])
    i_ref[...] = _dot(x, w_ref[:, 2 * HG_W:3 * HG_W]).astype(BF16)
    g = _dot(x, w_ref[:, 3 * HG_W:4 * HG_W])
    g_ref[...] = (g * _sigmoid(g)).astype(BF16)


def _hg_proj(x2, w_hg, tm):
    T = x2.shape[0]
    row = lambda i: (i, 0)
    out_spec = pl.BlockSpec((tm, HG_W), row)
    return pl.pallas_call(
        _hg_proj_kernel,
        grid=(T // tm,),
        in_specs=[pl.BlockSpec((tm, D_MODEL), row), _const_spec((D_MODEL, 4 * HG_W))],
        out_specs=[out_spec] * 4,
        out_shape=[jax.ShapeDtypeStruct((T, HG_W), BF16), jax.ShapeDtypeStruct((T, HG_W), F32),
                   jax.ShapeDtypeStruct((T, HG_W), BF16), jax.ShapeDtypeStruct((T, HG_W), BF16)],
        compiler_params=_params(("parallel",)),
        name="hg_proj",
    )(x2, w_hg)


_NQ = NSA_QW
_NKC = NSA_KVW
_NSL = SLOT_W
_ROPE_W = _NQ + _NKC + 2 * _NSL
_O_KC = _NQ
_O_KS = _NQ + _NKC
_O_KW = _O_KS + _NSL
_O_VC = _ROPE_W
_O_VS = _O_VC + _NKC
_O_VW = _O_VS + _NSL
_O_GT = _O_VW + _NSL
_NSA_W = _O_GT + LANES


def _nsa_proj_kernel(x_ref, w_ref, cos_ref, sin_ref, e_ref,
                     q_ref, kc_ref, ks_ref, kw_ref, vc_ref, vs_ref, vw_ref, gt_ref):
    x = x_ref[...].astype(BF16)
    cos = cos_ref[...]
    sin = sin_ref[...]
    half = NSA_D // 2

    def rope(y):
        w = y.shape[1]
        reps = w // LANES
        lane = lax.broadcasted_iota(jnp.int32, y.shape, 1)
        fwd = pltpu.roll(y, w - half, 1)
        bwd = pltpu.roll(y, half, 1)
        rot = jnp.where((lane % NSA_D) < half, fwd, bwd)
        return y * jnp.tile(cos, (1, reps)) + rot * jnp.tile(sin, (1, reps))

    yq = rope(_dot(x, w_ref[:, 0:_NQ]))
    q_ref[...] = (yq * (NSA_D ** -0.5 * LOG2E)).astype(BF16)
    kc_ref[...] = rope(_dot(x, w_ref[:, _O_KC:_O_KC + _NKC])).astype(BF16)
    ks_ref[...] = rope(_dot(x, w_ref[:, _O_KS:_O_KS + _NSL])).astype(BF16) + e_ref[...]
    kw_ref[...] = rope(_dot(x, w_ref[:, _O_KW:_O_KW + _NSL])).astype(BF16)
    vc_ref[...] = _dot(x, w_ref[:, _O_VC:_O_VC + _NKC]).astype(BF16)
    def value_slots(y):
        ones = jnp.ones((y.shape[0], LANES), BF16)
        parts = []
        for h in range(NSA_KV):
            parts += [y[:, h * LANES:(h + 1) * LANES].astype(BF16), ones]
        return jnp.concatenate(parts, axis=1)

    vs_ref[...] = value_slots(_dot(x, w_ref[:, _O_VS:_O_VS + _NSL]))
    vw_ref[...] = value_slots(_dot(x, w_ref[:, _O_VW:_O_VW + _NSL]))
    gt_ref[...] = _sigmoid(_dot(x, w_ref[:, _O_GT:_O_GT + LANES]))


def _nsa_proj(x2, w_nsa, cos_t, sin_t, e_tab, tm, S):
    T = x2.shape[0]
    row = lambda i: (i, 0)
    tiles_per_seq = S // tm
    widths = [_NQ, _NKC, _NSL, _NSL, _NKC, VSLOT_W, VSLOT_W]
    return pl.pallas_call(
        _nsa_proj_kernel,
        grid=(T // tm,),
        in_specs=[pl.BlockSpec((tm, D_MODEL), row), _const_spec((D_MODEL, _NSA_W)),
                  pl.BlockSpec((tm, LANES), row), pl.BlockSpec((tm, LANES), row),
                  pl.BlockSpec((tm, _NSL), lambda i: (i % tiles_per_seq, 0))],
        out_specs=[pl.BlockSpec((tm, w), row) for w in widths] + [pl.BlockSpec((tm, LANES), row)],
        out_shape=[jax.ShapeDtypeStruct((T, w), BF16) for w in widths]
                  + [jax.ShapeDtypeStruct((T, LANES), F32)],
        compiler_params=_params(("parallel",)),
        name="nsa_proj",
    )(x2, w_nsa, cos_t, sin_t, e_tab)


def _hgrn_kernel(q_ref, f_ref, i_ref, g_ref, lb_ref, ng_ref, o_ref, *, n_chunks, heads, unroll):
    C = HG_CHUNK
    ng = ng_ref[...]
    r = lax.broadcasted_iota(jnp.int32, (C, C), 0)
    c = lax.broadcasted_iota(jnp.int32, (C, C), 1)
    tril = r >= c
    row = lax.broadcasted_iota(jnp.int32, (C, HG_DK), 0)

    def chunk_cumsum(x):
        d = 1
        while d < C:
            x = x + jnp.where(row >= d, pltpu.roll(x, d, 0), 0.0)
            d *= 2
        return x

    def one(ci, h, st_t):
        sl = pl.ds(pl.multiple_of(ci * C, C), C)
        hs = slice(h * HG_DK, (h + 1) * HG_DK)
        lb = lb_ref[:, hs]
        f = lb + (1.0 - lb) * _sigmoid(f_ref[sl, hs])
        b = chunk_cumsum(jnp.log(f))
        k = 1.0 - f
        eb = jnp.exp(b)
        decay = eb[C - 1:C, :]
        q_dec = (q_ref[sl, hs].astype(F32) * eb).astype(BF16)
        k_inv_f = k * (1.0 / eb)
        k_inv = k_inv_f.astype(BF16)
        k_end = (k_inv_f * decay).astype(BF16)
        v = i_ref[sl, hs]
        attn = jnp.where(tril, _dot_nt(q_dec, k_inv), 0.0).astype(BF16)
        o = _dot(attn, v) + _dot_nt(q_dec, st_t.astype(BF16))
        st_t = st_t * decay + _dot_tn(v, k_end)
        ms = jnp.mean(o * o, -1, keepdims=True)
        on = o * lax.rsqrt(ms + RMS_EPS) * ng
        o_ref[sl, hs] = (on * g_ref[sl, hs].astype(F32)).astype(BF16)
        return st_t

    def body(cj, states):
        for u in range(unroll):
            states = tuple(one(cj * unroll + u, h, states[h]) for h in range(heads))
        return states

    init = tuple(jnp.zeros((HG_DV, HG_DK), F32) for _ in range(heads))
    lax.fori_loop(0, n_chunks // unroll, body, init)


def _hgrn(q, f, i, g, lb, ng, B, S, heads=4, unroll=2):
    T = B * S
    w = heads * HG_DK
    blk = pl.BlockSpec((S, w), lambda b, h: (b, h))
    return pl.pallas_call(
        functools.partial(_hgrn_kernel, n_chunks=S // HG_CHUNK, heads=heads, unroll=unroll),
        grid=(B, HG_HEADS // heads),
        in_specs=[blk, blk, blk, blk,
                  pl.BlockSpec((1, w), lambda b, h: (0, h)),
                  pl.BlockSpec((1, HG_DV), lambda b, h: (0, 0))],
        out_specs=blk,
        out_shape=jax.ShapeDtypeStruct((T, HG_W), BF16),
        compiler_params=_params(("parallel", "parallel")),
        name="hgrn",
    )(q, f, i, g, lb, ng)


def _compress_kernel(uk_ref, uv_ref, pk_ref, w1k_ref, b1k_ref, w2k_ref, b2k_ref,
                     pv_ref, w1v_ref, b1v_ref, w2v_ref, b2v_ref, tab_ref, ko_ref, vo_ref):
    half = CMP_STRIDE * NSA_D

    def mlp(u_ref, pos_ref, w1_ref, b1_ref, w2_ref, b2_ref, o_ref, tab=None):
        w_top = w1_ref[0:half, :]
        w_bot = w1_ref[half:2 * half, :]
        base = _dot(pos_ref[...], w1_ref[...])[0:1, :] + b1_ref[...]
        for h in range(NSA_KV):
            u = u_ref[0, h]
            top = _dot(u, w_top)
            bot = _dot(u, w_bot)
            n = top.shape[0]
            bot_next = jnp.concatenate([bot[1:n, :], jnp.zeros((1, CMP_HIDDEN), F32)], axis=0)
            hid = top + bot_next + base
            hid = hid * _sigmoid(hid)
            out = _dot(hid.astype(BF16), w2_ref[...]) + b2_ref[...]
            if tab is not None:
                out = out + tab
            o_ref[0, h] = out.astype(BF16)

    mlp(uk_ref, pk_ref, w1k_ref, b1k_ref, w2k_ref, b2k_ref, ko_ref)
    mlp(uv_ref, pv_ref, w1v_ref, b1v_ref, w2v_ref, b2v_ref, vo_ref, tab_ref[...])


def _compress(uk, uv, pk, w1k, b1k, w2k, b2k, pv, w1v, b1v, w2v, b2v, tab):
    B, _, nh, hw = uk.shape
    u_spec = pl.BlockSpec((1, NSA_KV, nh, hw), lambda b: (b, 0, 0, 0))
    o_spec = lambda w: pl.BlockSpec((1, NSA_KV, nh, w), lambda b: (b, 0, 0, 0))
    specs = lambda arrs: [_const_spec(a.shape) for a in arrs]
    return pl.pallas_call(
        _compress_kernel,
        grid=(B,),
        in_specs=([u_spec, u_spec] + specs((pk, w1k, b1k, w2k, b2k)) + specs((pv, w1v, b1v, w2v, b2v))
                  + [_const_spec(tab.shape)]),
        out_specs=[o_spec(LANES), o_spec(VSLOT)],
        out_shape=[jax.ShapeDtypeStruct((B, NSA_KV, nh, LANES), BF16),
                   jax.ShapeDtypeStruct((B, NSA_KV, nh, VSLOT), BF16)],
        compiler_params=_params(("parallel",)),
        name="compress",
    )(uk, uv, pk, w1k, b1k, w2k, b2k, pv, w1v, b1v, w2v, b2v, tab)


_L_OVL = LANES - 32


def _nsa_kernel(q_ref, ks_ref, vs_ref, kw_ref, vw_ref, kc_ref, vc_ref, gt_ref, gx_ref,
                o_ref, sbuf, qaug_ref, acc_ref, accw_ref, mt_ref, cmp_ref, gexp_ref,
                *, TQ, CK, NW, n_cmp, n_blk, top_n):
    R = NSA_G * TQ
    NH = CK // LANES
    q0 = pl.program_id(1) * TQ
    lane_q = lax.broadcasted_iota(jnp.int32, (TQ, LANES), 1)
    t_q = q0 + lax.broadcasted_iota(jnp.int32, (TQ, 1), 0)

    def add_bias(s, bias):
        return jnp.concatenate([s[g * TQ:(g + 1) * TQ] + bias for g in range(NSA_G)], axis=0)

    def fold_max(s):
        return functools.reduce(jnp.maximum, [s[:, u * LANES:(u + 1) * LANES] for u in range(NH)])

    def lanes_ck(m):
        return jnp.concatenate([m] * NH, axis=1)

    bias_c = jnp.where((lane_q * CMP_STRIDE + (CMP_LEN - 1) <= t_q) & (lane_q < n_cmp), 0.0, NEG)
    key_l = lax.broadcasted_iota(jnp.int32, (TQ, CK), 1)
    n_sel = (q0 + TQ + CK - 1) // CK
    last = n_sel - 1
    bias_diag = jnp.where(last * CK + key_l <= t_q, 0.0, NEG)

    j_blk = lax.broadcasted_iota(jnp.int32, (n_blk, TQ), 0)
    cur = (q0 + lax.broadcasted_iota(jnp.int32, (n_blk, TQ), 1)) // SEL_BLOCK
    blk_ok = j_blk <= cur
    forced = (j_blk == 0) | (j_blk == cur) | (j_blk == cur - 1)

    gates = gt_ref[...]
    g_hi = gates.astype(BF16)
    g_lo = (gates - g_hi.astype(F32)).astype(BF16)
    gexp_ref[...] = _dot(g_hi, gx_ref[...]) + _dot(g_lo, gx_ref[...])

    for kvh in range(NSA_KV):
        q_slabs = []
        for pair in range(NSA_G // 2):
            col = (kvh * NSA_G + 2 * pair) * NSA_D
            x = q_ref[:, col:col + LANES].astype(F32)
            q_slabs.append(x)
            q_slabs.append(pltpu.roll(x, NSA_D, 1))
        q_plain = jnp.concatenate(
            [jnp.where(lane_q < NSA_D, x, 0.0).astype(BF16) for x in q_slabs], axis=0)

        s_c = add_bias(_dot_nt(q_plain, kc_ref[0, kvh]), bias_c)
        m_c = jnp.maximum(jnp.max(s_c, -1, keepdims=True), 0.1 * NEG)
        acc_c = _dot(jnp.exp2(s_c - m_c).astype(BF16), vc_ref[0, kvh])
        p_n = acc_c[:, 0:LANES] * (1.0 / jnp.maximum(acc_c[:, LANES:VSLOT], 1e-30))
        cmp_ref[kvh] = p_n

        imp_rows = functools.reduce(jnp.add, [p_n[g * TQ:(g + 1) * TQ] for g in range(NSA_G)])
        imp = imp_rows.T[_L_OVL:_L_OVL + n_blk, :]
        score = jnp.where(blk_ok, imp + jnp.where(forced, SEL_FORCE, 0.0), -1.0)
        rank = jnp.zeros((n_blk, TQ), jnp.int32)
        for i in range(n_blk):
            ri = score[i:i + 1, :]
            beats = (ri > score) | ((ri == score) & (j_blk > i))
            rank = rank + beats.astype(jnp.int32)
        bias_t = jnp.where((rank < top_n) & blk_ok, 0.0, NEG)
        bias_q = jnp.concatenate([jnp.zeros((NSA_D, TQ), F32), bias_t,
                                  jnp.zeros((LANES - NSA_D - n_blk, TQ), F32)], axis=0).T
        qaug_ref[kvh] = jnp.concatenate(
            [jnp.where(lane_q < NSA_D, x, bias_q).astype(BF16) for x in q_slabs], axis=0)

    def chunk(ref, start, kvh, slot=LANES):
        return ref[pl.ds(start, CK), kvh * slot:(kvh + 1) * slot]

    def score_pass(k_ref, ci, slot, bias):
        start = pl.multiple_of(ci * CK, CK)
        for kvh in range(NSA_KV):
            s = _dot_nt(qaug_ref[kvh], chunk(k_ref, start, kvh))
            if bias is not None:
                s = add_bias(s, bias)
            sbuf[kvh, slot] = s
            mt_ref[kvh] = jnp.maximum(mt_ref[kvh], fold_max(s))

    def value_pass(v_ref, ci, slot, out_ref):
        start = pl.multiple_of(ci * CK, CK)
        for kvh in range(NSA_KV):
            p = jnp.exp2(sbuf[kvh, slot] - lanes_ck(mt_ref[kvh]))
            out_ref[kvh] += _dot(p.astype(BF16), chunk(v_ref, start, kvh, VSLOT))

    def begin_max():
        for kvh in range(NSA_KV):
            mt_ref[kvh] = jnp.full((R, LANES), -3e38, F32)

    def finish_max(out_ref):
        for kvh in range(NSA_KV):
            mt_ref[kvh] = jnp.broadcast_to(jnp.max(mt_ref[kvh], -1, keepdims=True), (R, LANES))
            out_ref[kvh] = jnp.zeros((R, VSLOT), F32)

    def loop(n, body):
        lax.fori_loop(0, n, lambda i, c: (body(i), c)[1], 0)

    begin_max()
    loop(last // 2, lambda j: (score_pass(ks_ref, 2 * j, 2 * j, None),
                               score_pass(ks_ref, 2 * j + 1, 2 * j + 1, None)))
    loop(last % 2, lambda j: score_pass(ks_ref, last - 1, last - 1, None))
    score_pass(ks_ref, last, last, bias_diag)
    finish_max(acc_ref)
    loop(n_sel // 2, lambda j: (value_pass(vs_ref, 2 * j, 2 * j, acc_ref),
                                value_pass(vs_ref, 2 * j + 1, 2 * j + 1, acc_ref)))
    loop(n_sel % 2, lambda j: value_pass(vs_ref, last, last, acc_ref))

    def win_chunk(k):
        cw = last - (NW - 1) + k
        return cw, jnp.maximum(cw, 0)

    def win_scores(k):
        cw, cidx = win_chunk(k)
        key = cidx * CK + key_l
        ok = (key <= t_q) & (key > t_q - WINDOW) & (cw >= 0)
        score_pass(kw_ref, cidx, k, jnp.where(ok, 0.0, NEG))

    begin_max()
    loop(NW, win_scores)
    finish_max(accw_ref)
    loop(NW, lambda k: value_pass(vw_ref, win_chunk(k)[1], k, accw_ref))

    left = lane_q < NSA_D
    for kvh in range(NSA_KV):
        for pair in range(NSA_G // 2):
            col = (kvh * NSA_G + 2 * pair) * NSA_D
            r_e = pl.ds(2 * pair * TQ, TQ)
            r_o = pl.ds((2 * pair + 1) * TQ, TQ)

            def packed(ref, lanes):
                return jnp.where(left, ref[kvh, r_e, lanes], pltpu.roll(ref[kvh, r_o, lanes], NSA_D, 1))

            out = packed(cmp_ref, slice(0, LANES)) * gexp_ref[:, col:col + LANES]
            for br, ref in ((1, acc_ref), (2, accw_ref)):
                den = jnp.where(left, ref[kvh, r_e, LANES:VSLOT], ref[kvh, r_o, LANES:VSLOT])
                gate = gexp_ref[:, br * NSA_QW + col:br * NSA_QW + col + LANES]
                out = out + packed(ref, slice(0, LANES)) * (gate / den)
            o_ref[:, col:col + LANES] = out.astype(BF16)


def _window_chunks(S, TQ, CK):
    return max((q0 + TQ - 1) // CK - max(q0 - WINDOW + 1, 0) // CK + 1 for q0 in range(0, S, TQ))


def _nsa(q, ks, vs, kw, vw, kc, vc, gt, B, S, TQ, CK):
    T = B * S
    n_blk = S // SEL_BLOCK
    n_cmp = (S - CMP_LEN) // CMP_STRIDE + 1
    R = NSA_G * TQ
    NW = _window_chunks(S, TQ, CK)
    assert kc.shape[2] == LANES and n_blk <= LANES - _L_OVL and S // CK >= NW
    seq = lambda w: pl.BlockSpec((S, w), lambda b, i: (b, 0))
    qrow = lambda w: pl.BlockSpec((TQ, w), lambda b, i: (b * (S // TQ) + i, 0))
    cmp_spec = lambda w: pl.BlockSpec((1, NSA_KV, LANES, w), lambda b, i: (b, 0, 0, 0))
    kern = functools.partial(_nsa_kernel, TQ=TQ, CK=CK, NW=NW, n_cmp=n_cmp, n_blk=n_blk,
                             top_n=min(SEL_TOPN, n_blk))
    per_head = lambda w, dt: pltpu.VMEM((NSA_KV, R, w), dt)
    gx = jnp.asarray(_gate_expand_table(), BF16)
    return pl.pallas_call(
        kern,
        grid=(B, S // TQ),
        in_specs=[qrow(NSA_QW), seq(SLOT_W), seq(VSLOT_W), seq(SLOT_W), seq(VSLOT_W),
                  cmp_spec(LANES), cmp_spec(VSLOT), qrow(LANES), _const_spec(gx.shape)],
        out_specs=qrow(NSA_QW),
        out_shape=jax.ShapeDtypeStruct((T, NSA_QW), BF16),
        scratch_shapes=[pltpu.VMEM((NSA_KV, S // CK, R, CK), F32),
                        per_head(LANES, BF16),
                        per_head(VSLOT, F32), per_head(VSLOT, F32),
                        per_head(LANES, F32),
                        per_head(LANES, F32),
                        pltpu.VMEM((TQ, 3 * NSA_QW), F32)],
        compiler_params=_params(("parallel", "parallel")),
        name="nsa",
    )(q, ks, vs, kw, vw, kc, vc, gt, gx)


def _merge_kernel(x_ref, a_ref, b_ref, wga_ref, wgb_ref, wua_ref, wub_ref, wo_ref, g_ref, bb_ref, o_ref):
    x = x_ref[...]
    xb = x.astype(BF16)
    m = (_sigmoid(_dot(xb, wga_ref[...])) * _dot(a_ref[...], wua_ref[...])
         + _sigmoid(_dot(xb, wgb_ref[...])) * _dot(b_ref[...], wub_ref[...]))
    mix = _dot(m.astype(BF16), wo_ref[...])
    o_ref[...] = _layer_norm(DN_ALPHA * x + mix, g_ref[...], bb_ref[...])


def _merge(x2, a, b, wga, wgb, wua, wub, wo, g, bb, tm):
    T = x2.shape[0]
    row = pl.BlockSpec((tm, D_MODEL), lambda i: (i, 0))
    wsp = _const_spec((D_MODEL, D_MODEL))
    vec = _const_spec((1, D_MODEL))
    return pl.pallas_call(
        _merge_kernel,
        grid=(T // tm,),
        in_specs=[row, row, row, wsp, wsp, wsp, wsp, wsp, vec, vec],
        out_specs=row,
        out_shape=jax.ShapeDtypeStruct((T, D_MODEL), F32),
        compiler_params=_params(("parallel",)),
        name="merge",
    )(x2, a, b, wga, wgb, wua, wub, wo, g, bb)


def _ffn_kernel(h_ref, wg_ref, wu_ref, wd_ref, g_ref, b_ref, o_ref):
    h = h_ref[...]
    hb = h.astype(BF16)
    gate = _dot(hb, wg_ref[...])
    act = (gate * _sigmoid(gate) * _dot(hb, wu_ref[...])).astype(BF16)
    ffn = _dot(act, wd_ref[...])
    o_ref[...] = _layer_norm(DN_ALPHA * h + ffn, g_ref[...], b_ref[...])


def _ffn(h, wg, wu, wd, g, b, tm):
    T = h.shape[0]
    d_ff = wg.shape[1]
    row = pl.BlockSpec((tm, D_MODEL), lambda i: (i, 0))
    vec = _const_spec((1, D_MODEL))
    once = pl.Buffered(1)
    return pl.pallas_call(
        _ffn_kernel,
        grid=(T // tm,),
        in_specs=[row,
                  pl.BlockSpec((D_MODEL, d_ff), lambda i: (0, 0), pipeline_mode=once),
                  pl.BlockSpec((D_MODEL, d_ff), lambda i: (0, 0), pipeline_mode=once),
                  pl.BlockSpec((d_ff, D_MODEL), lambda i: (0, 0), pipeline_mode=once),
                  vec, vec],
        out_specs=row,
        out_shape=jax.ShapeDtypeStruct((T, D_MODEL), F32),
        compiler_params=_params(("parallel",)),
        name="ffn",
    )(h, wg, wu, wd, g, b)


def _pad_slots(w):
    d = w.shape[0]
    w = w.reshape(d, NSA_KV, NSA_D)
    return jnp.pad(w, ((0, 0), (0, 0), (0, LANES - NSA_D))).reshape(d, SLOT_W)


def _cmp_value_table(n_half, n_cmp, n_blk):
    cs = np.arange(n_cmp)[:, None] * CMP_STRIDE
    bs = np.arange(n_blk)[None, :] * SEL_BLOCK
    ov = np.minimum(cs + CMP_LEN, bs + SEL_BLOCK) - np.maximum(cs, bs)
    out = np.zeros((n_half, VSLOT), np.float32)
    out[:, LANES:] = 1.0
    out[:n_cmp, _L_OVL:_L_OVL + n_blk] = np.clip(ov, 0, None) / CMP_LEN
    return out


def _gate_expand_table():
    out = np.zeros((LANES, 3 * NSA_QW), np.float32)
    for br in range(3):
        for h in range(NSA_HEADS):
            c = br * NSA_QW + h * NSA_D
            out[br * NSA_HEADS + h, c:c + NSA_D] = 1.0
    return out


def kernel(x, positions, w_in, hg_lb_logits, hg_norm_g, cmp_k_pos, cmp_k_w1, cmp_k_b1, cmp_k_w2, cmp_k_b2, cmp_v_pos, cmp_v_w1, cmp_v_b1, cmp_v_w2, cmp_v_b2, w_up_hg, w_up_nsa, w_o, ln1_g, ln1_b, w_ffn_gate, w_ffn_up, w_ffn_down, ln2_g, ln2_b):
    B, S, _ = x.shape
    T = B * S
    tm = min(512, S)
    TQ = 128
    CK = min(256, S)
    n_blk = S // SEL_BLOCK
    n_cmp = (S - CMP_LEN) // CMP_STRIDE + 1
    n_half = S // CMP_STRIDE

    lb_table = jnp.cumsum(jax.nn.softmax(hg_lb_logits.astype(F32), axis=0), axis=0)
    x2 = x.reshape(T, D_MODEL)

    w = w_in[0]
    o = 0
    hg_cols = w[:, o:o + 4 * HG_W]; o += 4 * HG_W
    wq = w[:, o:o + NSA_QW]; o += NSA_QW
    kv = []
    for _ in range(6):
        kv.append(w[:, o:o + NSA_KVW]); o += NSA_KVW
    w_kc, w_vc, w_ks, w_vs, w_kw, w_vw = kv
    w_gt = jnp.pad(w[:, o:o + 3 * NSA_HEADS], ((0, 0), (0, LANES - 3 * NSA_HEADS))); o += 3 * NSA_HEADS
    w_ga = w[:, o:o + D_MODEL]; o += D_MODEL
    w_gb = w[:, o:o + D_MODEL]
    w_nsa = jnp.concatenate([wq, w_kc, _pad_slots(w_ks), _pad_slots(w_kw),
                             w_vc, _pad_slots(w_vs), _pad_slots(w_vw), w_gt], axis=1).astype(BF16)

    half = NSA_D // 2
    inv_freq = ROPE_THETA ** (-jnp.arange(half, dtype=F32) / half)
    ang = positions.astype(F32).reshape(T, 1) * inv_freq[None, :]
    cos_t = jnp.tile(jnp.cos(ang), (1, LANES // half))
    sin_h = jnp.sin(ang)
    sin_t = jnp.tile(jnp.concatenate([-sin_h, sin_h], axis=1), (1, LANES // NSA_D))

    e_np = np.zeros((S, NSA_KV, LANES), np.float32)
    e_np[np.arange(S), :, NSA_D + np.arange(S) // SEL_BLOCK] = 1.0
    e_tab = jnp.asarray(e_np.reshape(S, SLOT_W), BF16)

    hq, hf, hi, hg = _hg_proj(x2, hg_cols.astype(BF16), tm)
    a = _hgrn(hq, hf, hi, hg, lb_table[0:1], hg_norm_g[0:1].astype(F32), B, S)

    q, kc_tok, ks, kw, vc_tok, vs, vw, gt = _nsa_proj(x2, w_nsa, cos_t, sin_t, e_tab, tm, S)

    def half_blocks(t):
        t = t.reshape(B, n_half, CMP_STRIDE, NSA_KV, NSA_D)
        return t.transpose(0, 3, 1, 2, 4).reshape(B, NSA_KV, n_half, CMP_STRIDE * NSA_D)

    def pos_rows(p):
        return jnp.broadcast_to(p.reshape(1, CMP_LEN * NSA_D), (8, CMP_LEN * NSA_D)).astype(BF16)

    def pad_out(w2, b2, width):
        return (jnp.pad(w2, ((0, 0), (0, width - NSA_D))).astype(BF16),
                jnp.pad(b2, (0, width - NSA_D)).reshape(1, width).astype(F32))

    w2k, b2k = pad_out(cmp_k_w2[0], cmp_k_b2[0], LANES)
    w2v, b2v = pad_out(cmp_v_w2[0], cmp_v_b2[0], VSLOT)
    kc, vc = _compress(
        half_blocks(kc_tok), half_blocks(vc_tok),
        pos_rows(cmp_k_pos[0]), cmp_k_w1[0].astype(BF16), cmp_k_b1[0].reshape(1, -1).astype(F32), w2k, b2k,
        pos_rows(cmp_v_pos[0]), cmp_v_w1[0].astype(BF16), cmp_v_b1[0].reshape(1, -1).astype(F32), w2v, b2v,
        jnp.asarray(_cmp_value_table(n_half, n_cmp, n_blk)))

    b_out = _nsa(q, ks, vs, kw, vw, kc, vc, gt, B, S, TQ, CK)

    h1 = _merge(x2, a, b_out, w_ga.astype(BF16), w_gb.astype(BF16),
                w_up_hg[0].astype(BF16), w_up_nsa[0].astype(BF16), w_o[0].astype(BF16),
                ln1_g[0].reshape(1, -1).astype(F32), ln1_b[0].reshape(1, -1).astype(F32), tm)
    out = _ffn(h1, w_ffn_gate[0].astype(BF16), w_ffn_up[0].astype(BF16), w_ffn_down[0].astype(BF16),
               ln2_g[0].reshape(1, -1).astype(F32), ln2_b[0].reshape(1, -1).astype(F32), tm)
    return out.reshape(B, S, D_MODEL)
```

```python
import functools

import numpy as np
import jax
import jax.numpy as jnp
from jax import lax
from jax.experimental import pallas as pl
from jax.experimental.pallas import tpu as pltpu

F32 = jnp.float32
BF16 = jnp.bfloat16

D_MODEL = 1024
HG_HEADS = 8
HG_DK = 128
HG_DV = 128
HG_W = HG_HEADS * HG_DK
HG_CHUNK = 64
NSA_HEADS = 16
NSA_KV = 4
NSA_G = 4
NSA_D = 64
NSA_QW = NSA_HEADS * NSA_D
NSA_KVW = NSA_KV * NSA_D
CMP_LEN = 32
CMP_STRIDE = 16
CMP_HIDDEN = 256
SEL_BLOCK = 64
SEL_TOPN = 8
SEL_FORCE = 1000.0
WINDOW = 512
ROPE_THETA = 10000.0
DEPTH = 1
DN_ALPHA = (2.0 * DEPTH) ** 0.25
LN_EPS = 1e-5
RMS_EPS = 1e-6
LOG2E = 1.4426950408889634
NEG = -1e30

LANES = 128
SLOT_W = NSA_KV * LANES
VSLOT = 2 * LANES
VSLOT_W = NSA_KV * VSLOT
VMEM_LIMIT = 56 * 1024 * 1024


def _dot(a, b):
    return jnp.dot(a, b, preferred_element_type=F32)


def _dot_nt(a, b):
    return lax.dot_general(a, b, (((1,), (1,)), ((), ())), preferred_element_type=F32)


def _dot_tn(a, b):
    return lax.dot_general(a, b, (((0,), (0,)), ((), ())), preferred_element_type=F32)


def _sigmoid(x):
    return 1.0 / (1.0 + jnp.exp(-x))


def _layer_norm(x, g, b):
    mu = jnp.mean(x, -1, keepdims=True)
    xc = x - mu
    var = jnp.mean(xc * xc, -1, keepdims=True)
    return xc * lax.rsqrt(var + LN_EPS) * g + b


def _params(sem):
    return pltpu.CompilerParams(dimension_semantics=sem, vmem_limit_bytes=VMEM_LIMIT)


def _const_spec(shape):
    nd = len(shape)
    return pl.BlockSpec(shape, lambda *_: (0,) * nd)


def _hg_proj_kernel(x_ref, w_ref, q_ref, f_ref, i_ref, g_ref):
    x = x_ref[...].astype(BF16)
    q_ref[...] = _dot(x, w_ref[:, 0:HG_W]).astype(BF16)
    f_ref[...] = _dot(x, w_ref[:, HG_W:2 * HG_W])
    i_ref[...] = _dot(x, w_ref[:, 2 * HG_W:3 * HG_W]).astype(BF16)
    g = _dot(x, w_ref[:, 3 * HG_W:4 * HG_W])
    g_ref[...] = (g * _sigmoid(g)).astype(BF16)


def _hg_proj(x2, w_hg, tm):
    T = x2.shape[0]
    row = lambda i: (i, 0)
    out_spec = pl.BlockSpec((tm, HG_W), row)
    return pl.pallas_call(
        _hg_proj_kernel,
        grid=(T // tm,),
        in_specs=[pl.BlockSpec((tm, D_MODEL), row), _const_spec((D_MODEL, 4 * HG_W))],
        out_specs=[out_spec] * 4,
        out_shape=[jax.ShapeDtypeStruct((T, HG_W), BF16), jax.ShapeDtypeStruct((T, HG_W), F32),
                   jax.ShapeDtypeStruct((T, HG_W), BF16), jax.ShapeDtypeStruct((T, HG_W), BF16)],
        compiler_params=_params(("parallel",)),
        name="hg_proj",
    )(x2, w_hg)


_NQ = NSA_QW
_NKC = NSA_KVW
_NSL = SLOT_W
_ROPE_W = _NQ + _NKC + 2 * _NSL
_O_KC = _NQ
_O_KS = _NQ + _NKC
_O_KW = _O_KS + _NSL
_O_VC = _ROPE_W
_O_VS = _O_VC + _NKC
_O_VW = _O_VS + _NSL
_O_GT = _O_VW + _NSL
_NSA_W = _O_GT + LANES


def _nsa_proj_kernel(x_ref, w_ref, cos_ref, sin_ref, e_ref,
                     q_ref, kc_ref, ks_ref, kw_ref, vc_ref, vs_ref, vw_ref, gt_ref):
    x = x_ref[...].astype(BF16)
    cos = cos_ref[...]
    sin = sin_ref[...]
    half = NSA_D // 2

    def rope(y):
        w = y.shape[1]
        reps = w // LANES
        lane = lax.broadcasted_iota(jnp.int32, y.shape, 1)
        fwd = pltpu.roll(y, w - half, 1)
        bwd = pltpu.roll(y, half, 1)
        rot = jnp.where((lane % NSA_D) < half, fwd, bwd)
        return y * jnp.tile(cos, (1, reps)) + rot * jnp.tile(sin, (1, reps))

    yq = rope(_dot(x, w_ref[:, 0:_NQ]))
    q_ref[...] = (yq * (NSA_D ** -0.5 * LOG2E)).astype(BF16)
    kc_ref[...] = rope(_dot(x, w_ref[:, _O_KC:_O_KC + _NKC])).astype(BF16)
    ks_ref[...] = rope(_dot(x, w_ref[:, _O_KS:_O_KS + _NSL])).astype(BF16) + e_ref[...]
    kw_ref[...] = rope(_dot(x, w_ref[:, _O_KW:_O_KW + _NSL])).astype(BF16)
    vc_ref[...] = _dot(x, w_ref[:, _O_VC:_O_VC + _NKC]).astype(BF16)
    def value_slots(y):
        ones = jnp.ones((y.shape[0], LANES), BF16)
        parts = []
        for h in range(NSA_KV):
            parts += [y[:, h * LANES:(h + 1) * LANES].astype(BF16), ones]
        return jnp.concatenate(parts, axis=1)

    vs_ref[...] = value_slots(_dot(x, w_ref[:, _O_VS:_O_VS + _NSL]))
    vw_ref[...] = value_slots(_dot(x, w_ref[:, _O_VW:_O_VW + _NSL]))
    gates = _sigmoid(_dot(x, w_ref[:, _O_GT:_O_GT + LANES]))
    g_hi = gates.astype(BF16).astype(F32)
    lane = lax.broadcasted_iota(jnp.int32, gates.shape, 1)
    gt_ref[...] = jnp.where(lane < LANES // 2, g_hi, pltpu.roll(gates - g_hi, LANES // 2, 1)).astype(BF16)


def _nsa_proj(x2, w_nsa, cos_t, sin_t, e_tab, tm, S):
    T = x2.shape[0]
    row = lambda i: (i, 0)
    tiles_per_seq = S // tm
    widths = [_NQ, _NKC, _NSL, _NSL, _NKC, VSLOT_W, VSLOT_W]
    return pl.pallas_call(
        _nsa_proj_kernel,
        grid=(T // tm,),
        in_specs=[pl.BlockSpec((tm, D_MODEL), row), _const_spec((D_MODEL, _NSA_W)),
                  pl.BlockSpec((tm, LANES), row), pl.BlockSpec((tm, LANES), row),
                  pl.BlockSpec((tm, _NSL), lambda i: (i % tiles_per_seq, 0))],
        out_specs=[pl.BlockSpec((tm, w), row) for w in widths] + [pl.BlockSpec((tm, LANES), row)],
        out_shape=[jax.ShapeDtypeStruct((T, w), BF16) for w in widths]
                  + [jax.ShapeDtypeStruct((T, LANES), BF16)],
        compiler_params=_params(("parallel",)),
        name="nsa_proj",
    )(x2, w_nsa, cos_t, sin_t, e_tab)


def _hgrn_kernel(q_ref, f_ref, i_ref, g_ref, lb_ref, ng_ref, o_ref, *, n_chunks, heads, unroll):
    C = HG_CHUNK
    ng = ng_ref[...]
    r = lax.broadcasted_iota(jnp.int32, (C, C), 0)
    c = lax.broadcasted_iota(jnp.int32, (C, C), 1)
    tril = r >= c
    row = lax.broadcasted_iota(jnp.int32, (C, HG_DK), 0)

    def chunk_cumsum(x):
        d = 1
        while d < C:
            x = x + jnp.where(row >= d, pltpu.roll(x, d, 0), 0.0)
            d *= 2
        return x

    def one(ci, h, st_t):
        sl = pl.ds(pl.multiple_of(ci * C, C), C)
        hs = slice(h * HG_DK, (h + 1) * HG_DK)
        lb = lb_ref[:, hs]
        f = lb + (1.0 - lb) * _sigmoid(f_ref[sl, hs])
        b = chunk_cumsum(jnp.log(f))
        k = 1.0 - f
        eb = jnp.exp(b)
        decay = eb[C - 1:C, :]
        q_dec = (q_ref[sl, hs].astype(F32) * eb).astype(BF16)
        k_inv_f = k * (1.0 / eb)
        k_inv = k_inv_f.astype(BF16)
        k_end = (k_inv_f * decay).astype(BF16)
        v = i_ref[sl, hs]
        attn = jnp.where(tril, _dot_nt(q_dec, k_inv), 0.0).astype(BF16)
        o = _dot(attn, v) + _dot_nt(q_dec, st_t.astype(BF16))
        st_t = st_t * decay + _dot_tn(v, k_end)
        ms = jnp.mean(o * o, -1, keepdims=True)
        on = o * lax.rsqrt(ms + RMS_EPS) * ng
        o_ref[sl, hs] = (on * g_ref[sl, hs].astype(F32)).astype(BF16)
        return st_t

    def body(cj, states):
        for u in range(unroll):
            states = tuple(one(cj * unroll + u, h, states[h]) for h in range(heads))
        return states

    init = tuple(jnp.zeros((HG_DV, HG_DK), F32) for _ in range(heads))
    lax.fori_loop(0, n_chunks // unroll, body, init)


def _hgrn(q, f, i, g, lb, ng, B, S, heads=8, unroll=2):
    T = B * S
    w = heads * HG_DK
    blk = pl.BlockSpec((S, w), lambda b, h: (b, h))
    return pl.pallas_call(
        functools.partial(_hgrn_kernel, n_chunks=S // HG_CHUNK, heads=heads, unroll=unroll),
        grid=(B, HG_HEADS // heads),
        in_specs=[blk, blk, blk, blk,
                  pl.BlockSpec((1, w), lambda b, h: (0, h)),
                  pl.BlockSpec((1, HG_DV), lambda b, h: (0, 0))],
        out_specs=blk,
        out_shape=jax.ShapeDtypeStruct((T, HG_W), BF16),
        compiler_params=_params(("parallel", "parallel")),
        name="hgrn",
    )(q, f, i, g, lb, ng)


def _compress_kernel(uk_ref, uv_ref, pk_ref, w1k_ref, b1k_ref, w2k_ref, b2k_ref,
                     pv_ref, w1v_ref, b1v_ref, w2v_ref, b2v_ref, tab_ref, ko_ref, vo_ref):
    half = CMP_STRIDE * NSA_D

    def mlp(u_ref, pos_ref, w1_ref, b1_ref, w2_ref, b2_ref, o_ref, tab=None):
        w_top = w1_ref[0:half, :]
        w_bot = w1_ref[half:2 * half, :]
        base = _dot(pos_ref[...], w1_ref[...])[0:1, :] + b1_ref[...]
        for h in range(NSA_KV):
            u = u_ref[0, h]
            top = _dot(u, w_top)
            bot = _dot(u, w_bot)
            n = top.shape[0]
            bot_next = jnp.concatenate([bot[1:n, :], jnp.zeros((1, CMP_HIDDEN), F32)], axis=0)
            hid = top + bot_next + base
            hid = hid * _sigmoid(hid)
            out = _dot(hid.astype(BF16), w2_ref[...]) + b2_ref[...]
            if tab is not None:
                out = out + tab
            o_ref[0, h] = out.astype(BF16)

    mlp(uk_ref, pk_ref, w1k_ref, b1k_ref, w2k_ref, b2k_ref, ko_ref)
    mlp(uv_ref, pv_ref, w1v_ref, b1v_ref, w2v_ref, b2v_ref, vo_ref, tab_ref[...])


def _compress(uk, uv, pk, w1k, b1k, w2k, b2k, pv, w1v, b1v, w2v, b2v, tab):
    B, _, nh, hw = uk.shape
    u_spec = pl.BlockSpec((1, NSA_KV, nh, hw), lambda b: (b, 0, 0, 0))
    o_spec = lambda w: pl.BlockSpec((1, NSA_KV, nh, w), lambda b: (b, 0, 0, 0))
    specs = lambda arrs: [_const_spec(a.shape) for a in arrs]
    return pl.pallas_call(
        _compress_kernel,
        grid=(B,),
        in_specs=([u_spec, u_spec] + specs((pk, w1k, b1k, w2k, b2k)) + specs((pv, w1v, b1v, w2v, b2v))
                  + [_const_spec(tab.shape)]),
        out_specs=[o_spec(LANES), o_spec(VSLOT)],
        out_shape=[jax.ShapeDtypeStruct((B, NSA_KV, nh, LANES), BF16),
                   jax.ShapeDtypeStruct((B, NSA_KV, nh, VSLOT), BF16)],
        compiler_params=_params(("parallel",)),
        name="compress",
    )(uk, uv, pk, w1k, b1k, w2k, b2k, pv, w1v, b1v, w2v, b2v, tab)


_L_OVL = LANES - 32


def _nsa_kernel(q_ref, ks_ref, vs_ref, kw_ref, vw_ref, kc_ref, vc_ref, gt_ref, gx_ref,
                o_ref, sbuf, qaug_ref, acc_ref, accw_ref, mt_ref, cmp_ref, gexp_ref,
                *, TQ, CK, NW, n_cmp, n_blk, top_n):
    R = NSA_G * TQ
    NH = CK // LANES
    q0 = pl.program_id(1) * TQ
    lane_q = lax.broadcasted_iota(jnp.int32, (TQ, LANES), 1)
    t_q = q0 + lax.broadcasted_iota(jnp.int32, (TQ, 1), 0)

    def add_bias(s, bias):
        return jnp.concatenate([s[g * TQ:(g + 1) * TQ] + bias for g in range(NSA_G)], axis=0)

    def fold_max(s):
        return functools.reduce(jnp.maximum, [s[:, u * LANES:(u + 1) * LANES] for u in range(NH)])

    def lanes_ck(m):
        return jnp.concatenate([m] * NH, axis=1)

    bias_c = jnp.where((lane_q * CMP_STRIDE + (CMP_LEN - 1) <= t_q) & (lane_q < n_cmp), 0.0, NEG)
    key_l = lax.broadcasted_iota(jnp.int32, (TQ, CK), 1)
    n_sel = (q0 + TQ + CK - 1) // CK
    last = n_sel - 1
    bias_diag = jnp.where(last * CK + key_l <= t_q, 0.0, NEG)

    j_blk = lax.broadcasted_iota(jnp.int32, (n_blk, TQ), 0)
    cur = (q0 + lax.broadcasted_iota(jnp.int32, (n_blk, TQ), 1)) // SEL_BLOCK
    blk_ok = j_blk <= cur
    forced = (j_blk == 0) | (j_blk == cur) | (j_blk == cur - 1)

    gexp_ref[...] = _dot(gt_ref[...], gx_ref[...])

    for kvh in range(NSA_KV):
        q_slabs = []
        for pair in range(NSA_G // 2):
            col = (kvh * NSA_G + 2 * pair) * NSA_D
            x = q_ref[:, col:col + LANES].astype(F32)
            q_slabs.append(x)
            q_slabs.append(pltpu.roll(x, NSA_D, 1))
        q_plain = jnp.concatenate(
            [jnp.where(lane_q < NSA_D, x, 0.0).astype(BF16) for x in q_slabs], axis=0)

        s_c = add_bias(_dot_nt(q_plain, kc_ref[0, kvh]), bias_c)
        m_c = jnp.maximum(jnp.max(s_c, -1, keepdims=True), 0.1 * NEG)
        acc_c = _dot(jnp.exp2(s_c - m_c).astype(BF16), vc_ref[0, kvh])
        p_n = acc_c[:, 0:LANES] * (1.0 / jnp.maximum(acc_c[:, LANES:VSLOT], 1e-30))
        cmp_ref[kvh] = p_n

        imp_rows = functools.reduce(jnp.add, [p_n[g * TQ:(g + 1) * TQ] for g in range(NSA_G)])
        imp = imp_rows.T[_L_OVL:_L_OVL + n_blk, :]
        score = jnp.where(blk_ok, imp + jnp.where(forced, SEL_FORCE, 0.0), -1.0)
        rank = jnp.zeros((n_blk, TQ), jnp.int32)
        for i in range(n_blk):
            ri = score[i:i + 1, :]
            beats = (ri > score) | ((ri == score) & (j_blk > i))
            rank = rank + beats.astype(jnp.int32)
        bias_t = jnp.where((rank < top_n) & blk_ok, 0.0, NEG)
        bias_q = jnp.concatenate([jnp.zeros((NSA_D, TQ), F32), bias_t,
                                  jnp.zeros((LANES - NSA_D - n_blk, TQ), F32)], axis=0).T
        qaug_ref[kvh] = jnp.concatenate(
            [jnp.where(lane_q < NSA_D, x, bias_q).astype(BF16) for x in q_slabs], axis=0)

    def chunk(ref, start, kvh, slot=LANES):
        return ref[pl.ds(start, CK), kvh * slot:(kvh + 1) * slot]

    def score_pass(k_ref, ci, slot, bias):
        start = pl.multiple_of(ci * CK, CK)
        for kvh in range(NSA_KV):
            s = _dot_nt(qaug_ref[kvh], chunk(k_ref, start, kvh))
            if bias is not None:
                s = add_bias(s, bias)
            sbuf[kvh, slot] = s
            mt_ref[kvh] = jnp.maximum(mt_ref[kvh], fold_max(s))

    def value_pass(v_ref, ci, slot, out_ref):
        start = pl.multiple_of(ci * CK, CK)
        for kvh in range(NSA_KV):
            p = jnp.exp2(sbuf[kvh, slot] - lanes_ck(mt_ref[kvh]))
            out_ref[kvh] += _dot(p.astype(BF16), chunk(v_ref, start, kvh, VSLOT))

    def begin_max():
        for kvh in range(NSA_KV):
            mt_ref[kvh] = jnp.full((R, LANES), -3e38, F32)

    def finish_max(out_ref):
        for kvh in range(NSA_KV):
            mt_ref[kvh] = jnp.broadcast_to(jnp.max(mt_ref[kvh], -1, keepdims=True), (R, LANES))
            out_ref[kvh] = jnp.zeros((R, VSLOT), F32)

    def loop(n, body):
        lax.fori_loop(0, n, lambda i, c: (body(i), c)[1], 0)

    begin_max()
    loop(last // 2, lambda j: (score_pass(ks_ref, 2 * j, 2 * j, None),
                               score_pass(ks_ref, 2 * j + 1, 2 * j + 1, None)))
    loop(last % 2, lambda j: score_pass(ks_ref, last - 1, last - 1, None))
    score_pass(ks_ref, last, last, bias_diag)
    finish_max(acc_ref)
    loop(n_sel // 2, lambda j: (value_pass(vs_ref, 2 * j, 2 * j, acc_ref),
                                value_pass(vs_ref, 2 * j + 1, 2 * j + 1, acc_ref)))
    loop(n_sel % 2, lambda j: value_pass(vs_ref, last, last, acc_ref))

    def win_chunk(k):
        cw = last - (NW - 1) + k
        return cw, jnp.maximum(cw, 0)

    def win_scores(k):
        cw, cidx = win_chunk(k)
        key = cidx * CK + key_l
        ok = (key <= t_q) & (key > t_q - WINDOW) & (cw >= 0)
        score_pass(kw_ref, cidx, k, jnp.where(ok, 0.0, NEG))

    begin_max()
    for k in range(NW):
        win_scores(k)
    finish_max(accw_ref)
    for k in range(NW):
        value_pass(vw_ref, win_chunk(k)[1], k, accw_ref)

    left = lane_q < NSA_D
    for kvh in range(NSA_KV):
        for pair in range(NSA_G // 2):
            col = (kvh * NSA_G + 2 * pair) * NSA_D
            r_e = pl.ds(2 * pair * TQ, TQ)
            r_o = pl.ds((2 * pair + 1) * TQ, TQ)

            def packed(ref, lanes):
                return jnp.where(left, ref[kvh, r_e, lanes], pltpu.roll(ref[kvh, r_o, lanes], NSA_D, 1))

            out = packed(cmp_ref, slice(0, LANES)) * gexp_ref[:, col:col + LANES]
            for br, ref in ((1, acc_ref), (2, accw_ref)):
                den = jnp.where(left, ref[kvh, r_e, LANES:VSLOT], ref[kvh, r_o, LANES:VSLOT])
                gate = gexp_ref[:, br * NSA_QW + col:br * NSA_QW + col + LANES]
                out = out + packed(ref, slice(0, LANES)) * (gate / den)
            o_ref[:, col:col + LANES] = out.astype(BF16)


def _window_chunks(S, TQ, CK):
    return max((q0 + TQ - 1) // CK - max(q0 - WINDOW + 1, 0) // CK + 1 for q0 in range(0, S, TQ))


def _nsa(q, ks, vs, kw, vw, kc, vc, gt, B, S, TQ, CK):
    T = B * S
    n_blk = S // SEL_BLOCK
    n_cmp = (S - CMP_LEN) // CMP_STRIDE + 1
    R = NSA_G * TQ
    NW = _window_chunks(S, TQ, CK)
    assert kc.shape[2] == LANES and n_blk <= LANES - _L_OVL and S // CK >= NW
    seq = lambda w: pl.BlockSpec((S, w), lambda b, i: (b, 0))
    qrow = lambda w: pl.BlockSpec((TQ, w), lambda b, i: (b * (S // TQ) + i, 0))
    cmp_spec = lambda w: pl.BlockSpec((1, NSA_KV, LANES, w), lambda b, i: (b, 0, 0, 0))
    kern = functools.partial(_nsa_kernel, TQ=TQ, CK=CK, NW=NW, n_cmp=n_cmp, n_blk=n_blk,
                             top_n=min(SEL_TOPN, n_blk))
    per_head = lambda w, dt: pltpu.VMEM((NSA_KV, R, w), dt)
    gx = jnp.asarray(_gate_expand_table(), BF16)
    return pl.pallas_call(
        kern,
        grid=(B, S // TQ),
        in_specs=[qrow(NSA_QW), seq(SLOT_W), seq(VSLOT_W), seq(SLOT_W), seq(VSLOT_W),
                  cmp_spec(LANES), cmp_spec(VSLOT), qrow(LANES), _const_spec(gx.shape)],
        out_specs=qrow(NSA_QW),
        out_shape=jax.ShapeDtypeStruct((T, NSA_QW), BF16),
        scratch_shapes=[pltpu.VMEM((NSA_KV, S // CK, R, CK), F32),
                        per_head(LANES, BF16),
                        per_head(VSLOT, F32), per_head(VSLOT, F32),
                        per_head(LANES, F32),
                        per_head(LANES, F32),
                        pltpu.VMEM((TQ, 3 * NSA_QW), F32)],
        compiler_params=_params(("parallel", "parallel")),
        name="nsa",
    )(q, ks, vs, kw, vw, kc, vc, gt, gx)


def _merge_kernel(x_ref, a_ref, b_ref, wga_ref, wgb_ref, wua_ref, wub_ref, wo_ref, g_ref, bb_ref, o_ref):
    x = x_ref[...]
    xb = x.astype(BF16)
    m = (_sigmoid(_dot(xb, wga_ref[...])) * _dot(a_ref[...], wua_ref[...])
         + _sigmoid(_dot(xb, wgb_ref[...])) * _dot(b_ref[...], wub_ref[...]))
    mix = _dot(m.astype(BF16), wo_ref[...])
    o_ref[...] = _layer_norm(DN_ALPHA * x + mix, g_ref[...], bb_ref[...])


def _merge(x2, a, b, wga, wgb, wua, wub, wo, g, bb, tm):
    T = x2.shape[0]
    row = pl.BlockSpec((tm, D_MODEL), lambda i: (i, 0))
    wsp = _const_spec((D_MODEL, D_MODEL))
    vec = _const_spec((1, D_MODEL))
    return pl.pallas_call(
        _merge_kernel,
        grid=(T // tm,),
        in_specs=[row, row, row, wsp, wsp, wsp, wsp, wsp, vec, vec],
        out_specs=row,
        out_shape=jax.ShapeDtypeStruct((T, D_MODEL), F32),
        compiler_params=_params(("parallel",)),
        name="merge",
    )(x2, a, b, wga, wgb, wua, wub, wo, g, bb)


def _ffn_kernel(h_ref, wg_ref, wu_ref, wd_ref, g_ref, b_ref, o_ref):
    h = h_ref[...]
    hb = h.astype(BF16)
    gate = _dot(hb, wg_ref[...])
    act = (gate * _sigmoid(gate) * _dot(hb, wu_ref[...])).astype(BF16)
    ffn = _dot(act, wd_ref[...])
    o_ref[...] = _layer_norm(DN_ALPHA * h + ffn, g_ref[...], b_ref[...])


def _ffn(h, wg, wu, wd, g, b, tm):
    T = h.shape[0]
    d_ff = wg.shape[1]
    row = pl.BlockSpec((tm, D_MODEL), lambda i: (i, 0))
    vec = _const_spec((1, D_MODEL))
    once = pl.Buffered(1)
    return pl.pallas_call(
        _ffn_kernel,
        grid=(T // tm,),
        in_specs=[row,
                  pl.BlockSpec((D_MODEL, d_ff), lambda i: (0, 0), pipeline_mode=once),
                  pl.BlockSpec((D_MODEL, d_ff), lambda i: (0, 0), pipeline_mode=once),
                  pl.BlockSpec((d_ff, D_MODEL), lambda i: (0, 0), pipeline_mode=once),
                  vec, vec],
        out_specs=row,
        out_shape=jax.ShapeDtypeStruct((T, D_MODEL), F32),
        compiler_params=_params(("parallel",)),
        name="ffn",
    )(h, wg, wu, wd, g, b)


def _pad_slots(w):
    d = w.shape[0]
    w = w.reshape(d, NSA_KV, NSA_D)
    return jnp.pad(w, ((0, 0), (0, 0), (0, LANES - NSA_D))).reshape(d, SLOT_W)


def _cmp_value_table(n_half, n_cmp, n_blk):
    cs = np.arange(n_cmp)[:, None] * CMP_STRIDE
    bs = np.arange(n_blk)[None, :] * SEL_BLOCK
    ov = np.minimum(cs + CMP_LEN, bs + SEL_BLOCK) - np.maximum(cs, bs)
    out = np.zeros((n_half, VSLOT), np.float32)
    out[:, LANES:] = 1.0
    out[:n_cmp, _L_OVL:_L_OVL + n_blk] = np.clip(ov, 0, None) / CMP_LEN
    return out


def _gate_expand_table():
    out = np.zeros((LANES, 3 * NSA_QW), np.float32)
    for br in range(3):
        for h in range(NSA_HEADS):
            c = br * NSA_QW + h * NSA_D
            out[br * NSA_HEADS + h, c:c + NSA_D] = 1.0
            out[LANES // 2 + br * NSA_HEADS + h, c:c + NSA_D] = 1.0
    return out


def kernel(x, positions, w_in, hg_lb_logits, hg_norm_g, cmp_k_pos, cmp_k_w1, cmp_k_b1, cmp_k_w2, cmp_k_b2, cmp_v_pos, cmp_v_w1, cmp_v_b1, cmp_v_w2, cmp_v_b2, w_up_hg, w_up_nsa, w_o, ln1_g, ln1_b, w_ffn_gate, w_ffn_up, w_ffn_down, ln2_g, ln2_b):
    B, S, _ = x.shape
    T = B * S
    tm = min(512, S)
    TQ = 128
    CK = min(256, S)
    n_blk = S // SEL_BLOCK
    n_cmp = (S - CMP_LEN) // CMP_STRIDE + 1
    n_half = S // CMP_STRIDE

    lb_table = jnp.cumsum(jax.nn.softmax(hg_lb_logits.astype(F32), axis=0), axis=0)
    x2 = x.reshape(T, D_MODEL)

    w = w_in[0]
    o = 0
    hg_cols = w[:, o:o + 4 * HG_W]; o += 4 * HG_W
    wq = w[:, o:o + NSA_QW]; o += NSA_QW
    kv = []
    for _ in range(6):
        kv.append(w[:, o:o + NSA_KVW]); o += NSA_KVW
    w_kc, w_vc, w_ks, w_vs, w_kw, w_vw = kv
    w_gt = jnp.pad(w[:, o:o + 3 * NSA_HEADS], ((0, 0), (0, LANES - 3 * NSA_HEADS))); o += 3 * NSA_HEADS
    w_ga = w[:, o:o + D_MODEL]; o += D_MODEL
    w_gb = w[:, o:o + D_MODEL]
    w_nsa = jnp.concatenate([wq, w_kc, _pad_slots(w_ks), _pad_slots(w_kw),
                             w_vc, _pad_slots(w_vs), _pad_slots(w_vw), w_gt], axis=1).astype(BF16)

    half = NSA_D // 2
    inv_freq = ROPE_THETA ** (-jnp.arange(half, dtype=F32) / half)
    ang = positions.astype(F32).reshape(T, 1) * inv_freq[None, :]
    cos_t = jnp.tile(jnp.cos(ang), (1, LANES // half))
    sin_h = jnp.sin(ang)
    sin_t = jnp.tile(jnp.concatenate([-sin_h, sin_h], axis=1), (1, LANES // NSA_D))

    e_np = np.zeros((S, NSA_KV, LANES), np.float32)
    e_np[np.arange(S), :, NSA_D + np.arange(S) // SEL_BLOCK] = 1.0
    e_tab = jnp.asarray(e_np.reshape(S, SLOT_W), BF16)

    hq, hf, hi, hg = _hg_proj(x2, hg_cols.astype(BF16), tm)
    a = _hgrn(hq, hf, hi, hg, lb_table[0:1], hg_norm_g[0:1].astype(F32), B, S)

    q, kc_tok, ks, kw, vc_tok, vs, vw, gt = _nsa_proj(x2, w_nsa, cos_t, sin_t, e_tab, tm, S)

    def half_blocks(t):
        t = t.reshape(B, n_half, CMP_STRIDE, NSA_KV, NSA_D)
        return t.transpose(0, 3, 1, 2, 4).reshape(B, NSA_KV, n_half, CMP_STRIDE * NSA_D)

    def pos_rows(p):
        return jnp.broadcast_to(p.reshape(1, CMP_LEN * NSA_D), (8, CMP_LEN * NSA_D)).astype(BF16)

    def pad_out(w2, b2, width):
        return (jnp.pad(w2, ((0, 0), (0, width - NSA_D))).astype(BF16),
                jnp.pad(b2, (0, width - NSA_D)).reshape(1, width).astype(F32))

    w2k, b2k = pad_out(cmp_k_w2[0], cmp_k_b2[0], LANES)
    w2v, b2v = pad_out(cmp_v_w2[0], cmp_v_b2[0], VSLOT)
    kc, vc = _compress(
        half_blocks(kc_tok), half_blocks(vc_tok),
        pos_rows(cmp_k_pos[0]), cmp_k_w1[0].astype(BF16), cmp_k_b1[0].reshape(1, -1).astype(F32), w2k, b2k,
        pos_rows(cmp_v_pos[0]), cmp_v_w1[0].astype(BF16), cmp_v_b1[0].reshape(1, -1).astype(F32), w2v, b2v,
        jnp.asarray(_cmp_value_table(n_half, n_cmp, n_blk)))

    b_out = _nsa(q, ks, vs, kw, vw, kc, vc, gt, B, S, TQ, CK)

    h1 = _merge(x2, a, b_out, w_ga.astype(BF16), w_gb.astype(BF16),
                w_up_hg[0].astype(BF16), w_up_nsa[0].astype(BF16), w_o[0].astype(BF16),
                ln1_g[0].reshape(1, -1).astype(F32), ln1_b[0].reshape(1, -1).astype(F32), tm)
    out = _ffn(h1, w_ffn_gate[0].astype(BF16), w_ffn_up[0].astype(BF16), w_ffn_down[0].astype(BF16),
               ln2_g[0].reshape(1, -1).astype(F32), ln2_b[0].reshape(1, -1).astype(F32), tm)
    return out.reshape(B, S, D_MODEL)
```

```python
import functools

import numpy as np
import jax
import jax.numpy as jnp
from jax import lax
from jax.experimental import pallas as pl
from jax.experimental.pallas import tpu as pltpu

F32 = jnp.float32
BF16 = jnp.bfloat16

D_MODEL = 1024
HG_HEADS = 8
HG_DK = 128
HG_DV = 128
HG_W = HG_HEADS * HG_DK
HG_CHUNK = 64
NSA_HEADS = 16
NSA_KV = 4
NSA_G = 4
NSA_D = 64
NSA_QW = NSA_HEADS * NSA_D
NSA_KVW = NSA_KV * NSA_D
CMP_LEN = 32
CMP_STRIDE = 16
CMP_HIDDEN = 256
SEL_BLOCK = 64
SEL_TOPN = 8
SEL_FORCE = 1000.0
WINDOW = 512
ROPE_THETA = 10000.0
DEPTH = 1
DN_ALPHA = (2.0 * DEPTH) ** 0.25
LN_EPS = 1e-5
RMS_EPS = 1e-6
LOG2E = 1.4426950408889634
NEG = -1e30

LANES = 128
SLOT_W = NSA_KV * LANES
VSLOT = 2 * LANES
VSLOT_W = NSA_KV * VSLOT
VMEM_LIMIT = 56 * 1024 * 1024


def _dot(a, b):
    return jnp.dot(a, b, preferred_element_type=F32)


def _dot_nt(a, b):
    return lax.dot_general(a, b, (((1,), (1,)), ((), ())), preferred_element_type=F32)


def _dot_tn(a, b):
    return lax.dot_general(a, b, (((0,), (0,)), ((), ())), preferred_element_type=F32)


def _sigmoid(x):
    return 1.0 / (1.0 + jnp.exp(-x))


def _layer_norm(x, g, b):
    mu = jnp.mean(x, -1, keepdims=True)
    xc = x - mu
    var = jnp.mean(xc * xc, -1, keepdims=True)
    return xc * lax.rsqrt(var + LN_EPS) * g + b


def _params(sem):
    return pltpu.CompilerParams(dimension_semantics=sem, vmem_limit_bytes=VMEM_LIMIT)


def _const_spec(shape):
    nd = len(shape)
    return pl.BlockSpec(shape, lambda *_: (0,) * nd)


def _hg_proj_kernel(x_ref, w_ref, q_ref, f_ref, i_ref, g_ref):
    x = x_ref[...].astype(BF16)
    q_ref[...] = _dot(x, w_ref[:, 0:HG_W]).astype(BF16)
    f_ref[...] = _dot(x, w_ref[:, HG_W:2 * HG_W])
    i_ref[...] = _dot(x, w_ref[:, 2 * HG_W:3 * HG_W]).astype(BF16)
    g = _dot(x, w_ref[:, 3 * HG_W:4 * HG_W])
    g_ref[...] = (g * _sigmoid(g)).astype(BF16)


def _hg_proj(x2, w_hg, tm):
    T = x2.shape[0]
    row = lambda i: (i, 0)
    out_spec = pl.BlockSpec((tm, HG_W), row)
    return pl.pallas_call(
        _hg_proj_kernel,
        grid=(T // tm,),
        in_specs=[pl.BlockSpec((tm, D_MODEL), row), _const_spec((D_MODEL, 4 * HG_W))],
        out_specs=[out_spec] * 4,
        out_shape=[jax.ShapeDtypeStruct((T, HG_W), BF16), jax.ShapeDtypeStruct((T, HG_W), F32),
                   jax.ShapeDtypeStruct((T, HG_W), BF16), jax.ShapeDtypeStruct((T, HG_W), BF16)],
        compiler_params=_params(("parallel",)),
        name="hg_proj",
    )(x2, w_hg)


_NQ = NSA_QW
_NKC = NSA_KVW
_NSL = SLOT_W
_ROPE_W = _NQ + _NKC + 2 * _NSL
_O_KC = _NQ
_O_KS = _NQ + _NKC
_O_KW = _O_KS + _NSL
_O_VC = _ROPE_W
_O_VS = _O_VC + _NKC
_O_VW = _O_VS + _NSL
_O_GT = _O_VW + _NSL
_NSA_W = _O_GT + LANES


def _nsa_proj_kernel(x_ref, w_ref, cos_ref, sin_ref, e_ref,
                     q_ref, kc_ref, ks_ref, kw_ref, vc_ref, vs_ref, vw_ref, gt_ref):
    x = x_ref[...].astype(BF16)
    cos = cos_ref[...]
    sin = sin_ref[...]
    half = NSA_D // 2

    def rope(y):
        w = y.shape[1]
        reps = w // LANES
        lane = lax.broadcasted_iota(jnp.int32, y.shape, 1)
        fwd = pltpu.roll(y, w - half, 1)
        bwd = pltpu.roll(y, half, 1)
        rot = jnp.where((lane % NSA_D) < half, fwd, bwd)
        return y * jnp.tile(cos, (1, reps)) + rot * jnp.tile(sin, (1, reps))

    yq = rope(_dot(x, w_ref[:, 0:_NQ]))
    q_ref[...] = (yq * (NSA_D ** -0.5 * LOG2E)).astype(BF16)
    kc_ref[...] = rope(_dot(x, w_ref[:, _O_KC:_O_KC + _NKC])).astype(BF16)
    ks_ref[...] = rope(_dot(x, w_ref[:, _O_KS:_O_KS + _NSL])).astype(BF16) + e_ref[...]
    kw_ref[...] = rope(_dot(x, w_ref[:, _O_KW:_O_KW + _NSL])).astype(BF16)
    vc_ref[...] = _dot(x, w_ref[:, _O_VC:_O_VC + _NKC]).astype(BF16)
    def value_slots(y):
        ones = jnp.ones((y.shape[0], LANES), BF16)
        parts = []
        for h in range(NSA_KV):
            parts += [y[:, h * LANES:(h + 1) * LANES].astype(BF16), ones]
        return jnp.concatenate(parts, axis=1)

    vs_ref[...] = value_slots(_dot(x, w_ref[:, _O_VS:_O_VS + _NSL]))
    vw_ref[...] = value_slots(_dot(x, w_ref[:, _O_VW:_O_VW + _NSL]))
    gates = _sigmoid(_dot(x, w_ref[:, _O_GT:_O_GT + LANES]))
    g_hi = gates.astype(BF16).astype(F32)
    lane = lax.broadcasted_iota(jnp.int32, gates.shape, 1)
    gt_ref[...] = jnp.where(lane < LANES // 2, g_hi, pltpu.roll(gates - g_hi, LANES // 2, 1)).astype(BF16)


def _nsa_proj(x2, w_nsa, cos_t, sin_t, e_tab, tm, S):
    T = x2.shape[0]
    row = lambda i: (i, 0)
    tiles_per_seq = S // tm
    widths = [_NQ, _NKC, _NSL, _NSL, _NKC, VSLOT_W, VSLOT_W]
    return pl.pallas_call(
        _nsa_proj_kernel,
        grid=(T // tm,),
        in_specs=[pl.BlockSpec((tm, D_MODEL), row), _const_spec((D_MODEL, _NSA_W)),
                  pl.BlockSpec((tm, LANES), row), pl.BlockSpec((tm, LANES), row),
                  pl.BlockSpec((tm, _NSL), lambda i: (i % tiles_per_seq, 0))],
        out_specs=[pl.BlockSpec((tm, w), row) for w in widths] + [pl.BlockSpec((tm, LANES), row)],
        out_shape=[jax.ShapeDtypeStruct((T, w), BF16) for w in widths]
                  + [jax.ShapeDtypeStruct((T, LANES), BF16)],
        compiler_params=_params(("parallel",)),
        name="nsa_proj",
    )(x2, w_nsa, cos_t, sin_t, e_tab)


def _hgrn_kernel(q_ref, f_ref, i_ref, g_ref, lb_ref, ng_ref, o_ref, *, n_chunks, heads, unroll):
    C = HG_CHUNK
    ng = ng_ref[...]
    r = lax.broadcasted_iota(jnp.int32, (C, C), 0)
    c = lax.broadcasted_iota(jnp.int32, (C, C), 1)
    tril = r >= c
    row = lax.broadcasted_iota(jnp.int32, (C, HG_DK), 0)

    def chunk_cumsum(x):
        d = 1
        while d < C:
            x = x + jnp.where(row >= d, pltpu.roll(x, d, 0), 0.0)
            d *= 2
        return x

    def body(cj, states):
        states = list(states)
        items = [(u, h) for u in range(unroll) for h in range(heads)]
        rows = {it: pl.ds(pl.multiple_of((cj * unroll + it[0]) * C, C), C) for it in items}
        cols = {it: slice(it[1] * HG_DK, (it[1] + 1) * HG_DK) for it in items}
        f, eb, q_dec, k_inv, k_end, attn, out = {}, {}, {}, {}, {}, {}, {}
        for it in items:
            lb = lb_ref[:, cols[it]]
            f[it] = lb + (1.0 - lb) * _sigmoid(f_ref[rows[it], cols[it]])
        for it in items:
            eb[it] = jnp.exp(chunk_cumsum(jnp.log(f[it])))
        for it in items:
            q_dec[it] = (q_ref[rows[it], cols[it]].astype(F32) * eb[it]).astype(BF16)
            k_inv_f = (1.0 - f[it]) * (1.0 / eb[it])
            k_inv[it] = k_inv_f.astype(BF16)
            k_end[it] = (k_inv_f * eb[it][C - 1:C, :]).astype(BF16)
        for it in items:
            attn[it] = jnp.where(tril, _dot_nt(q_dec[it], k_inv[it]), 0.0).astype(BF16)
        for it in items:
            h = it[1]
            v = i_ref[rows[it], cols[it]]
            out[it] = _dot(attn[it], v) + _dot_nt(q_dec[it], states[h].astype(BF16))
            states[h] = states[h] * eb[it][C - 1:C, :] + _dot_tn(v, k_end[it])
        for it in items:
            o = out[it]
            on = o * lax.rsqrt(jnp.mean(o * o, -1, keepdims=True) + RMS_EPS) * ng
            o_ref[rows[it], cols[it]] = (on * g_ref[rows[it], cols[it]].astype(F32)).astype(BF16)
        return tuple(states)

    init = tuple(jnp.zeros((HG_DV, HG_DK), F32) for _ in range(heads))
    lax.fori_loop(0, n_chunks // unroll, body, init)


def _hgrn(q, f, i, g, lb, ng, B, S, heads=8, unroll=4):
    T = B * S
    w = heads * HG_DK
    blk = pl.BlockSpec((S, w), lambda b, h: (b, h))
    return pl.pallas_call(
        functools.partial(_hgrn_kernel, n_chunks=S // HG_CHUNK, heads=heads, unroll=unroll),
        grid=(B, HG_HEADS // heads),
        in_specs=[blk, blk, blk, blk,
                  pl.BlockSpec((1, w), lambda b, h: (0, h)),
                  pl.BlockSpec((1, HG_DV), lambda b, h: (0, 0))],
        out_specs=blk,
        out_shape=jax.ShapeDtypeStruct((T, HG_W), BF16),
        compiler_params=_params(("parallel", "parallel")),
        name="hgrn",
    )(q, f, i, g, lb, ng)


def _compress_kernel(uk_ref, uv_ref, pk_ref, w1k_ref, b1k_ref, w2k_ref, b2k_ref,
                     pv_ref, w1v_ref, b1v_ref, w2v_ref, b2v_ref, tab_ref, ko_ref, vo_ref):
    half = CMP_STRIDE * NSA_D

    def mlp(u_ref, pos_ref, w1_ref, b1_ref, w2_ref, b2_ref, o_ref, tab=None):
        w_top = w1_ref[0:half, :]
        w_bot = w1_ref[half:2 * half, :]
        base = _dot(pos_ref[...], w1_ref[...])[0:1, :] + b1_ref[...]
        for h in range(NSA_KV):
            u = u_ref[0, h]
            top = _dot(u, w_top)
            bot = _dot(u, w_bot)
            n = top.shape[0]
            bot_next = jnp.concatenate([bot[1:n, :], jnp.zeros((1, CMP_HIDDEN), F32)], axis=0)
            hid = top + bot_next + base
            hid = hid * _sigmoid(hid)
            out = _dot(hid.astype(BF16), w2_ref[...]) + b2_ref[...]
            if tab is not None:
                out = out + tab
            o_ref[0, h] = out.astype(BF16)

    mlp(uk_ref, pk_ref, w1k_ref, b1k_ref, w2k_ref, b2k_ref, ko_ref)
    mlp(uv_ref, pv_ref, w1v_ref, b1v_ref, w2v_ref, b2v_ref, vo_ref, tab_ref[...])


def _compress(uk, uv, pk, w1k, b1k, w2k, b2k, pv, w1v, b1v, w2v, b2v, tab):
    B, _, nh, hw = uk.shape
    u_spec = pl.BlockSpec((1, NSA_KV, nh, hw), lambda b: (b, 0, 0, 0))
    o_spec = lambda w: pl.BlockSpec((1, NSA_KV, nh, w), lambda b: (b, 0, 0, 0))
    specs = lambda arrs: [_const_spec(a.shape) for a in arrs]
    return pl.pallas_call(
        _compress_kernel,
        grid=(B,),
        in_specs=([u_spec, u_spec] + specs((pk, w1k, b1k, w2k, b2k)) + specs((pv, w1v, b1v, w2v, b2v))
                  + [_const_spec(tab.shape)]),
        out_specs=[o_spec(LANES), o_spec(VSLOT)],
        out_shape=[jax.ShapeDtypeStruct((B, NSA_KV, nh, LANES), BF16),
                   jax.ShapeDtypeStruct((B, NSA_KV, nh, VSLOT), BF16)],
        compiler_params=_params(("parallel",)),
        name="compress",
    )(uk, uv, pk, w1k, b1k, w2k, b2k, pv, w1v, b1v, w2v, b2v, tab)


_L_OVL = LANES - 32


def _nsa_kernel(q_ref, ks_ref, vs_ref, kw_ref, vw_ref, kc_ref, vc_ref, gt_ref, gx_ref,
                o_ref, sbuf, qpl_ref, qaug_ref, acc_ref, accw_ref, mt_ref, cmp_ref, gexp_ref,
                *, TQ, CK, NW, n_cmp, n_blk, top_n):
    R = NSA_G * TQ
    NH = CK // LANES
    q0 = pl.program_id(1) * TQ
    lane_q = lax.broadcasted_iota(jnp.int32, (TQ, LANES), 1)
    t_q = q0 + lax.broadcasted_iota(jnp.int32, (TQ, 1), 0)

    def add_bias(s, bias):
        return jnp.concatenate([s[g * TQ:(g + 1) * TQ] + bias for g in range(NSA_G)], axis=0)

    def fold_max(s):
        return functools.reduce(jnp.maximum, [s[:, u * LANES:(u + 1) * LANES] for u in range(NH)])

    def lanes_ck(m):
        return jnp.concatenate([m] * NH, axis=1)

    bias_c = jnp.where((lane_q * CMP_STRIDE + (CMP_LEN - 1) <= t_q) & (lane_q < n_cmp), 0.0, NEG)
    key_l = lax.broadcasted_iota(jnp.int32, (TQ, CK), 1)
    n_sel = (q0 + TQ + CK - 1) // CK
    last = n_sel - 1
    bias_diag = jnp.where(last * CK + key_l <= t_q, 0.0, NEG)

    j_blk = lax.broadcasted_iota(jnp.int32, (n_blk, TQ), 0)
    cur = (q0 + lax.broadcasted_iota(jnp.int32, (n_blk, TQ), 1)) // SEL_BLOCK
    blk_ok = j_blk <= cur
    forced = (j_blk == 0) | (j_blk == cur) | (j_blk == cur - 1)

    gexp_ref[...] = _dot(gt_ref[...], gx_ref[...])

    def q_slabs(kvh):
        slabs = []
        for pair in range(NSA_G // 2):
            col = (kvh * NSA_G + 2 * pair) * NSA_D
            x = q_ref[:, col:col + LANES].astype(F32)
            slabs += [x, pltpu.roll(x, NSA_D, 1)]
        return slabs

    for kvh in range(NSA_KV):
        qpl_ref[kvh] = jnp.concatenate(
            [jnp.where(lane_q < NSA_D, x, 0.0).astype(BF16) for x in q_slabs(kvh)], axis=0)

    KV = range(NSA_KV)
    s_c = [add_bias(_dot_nt(qpl_ref[h], kc_ref[0, h]), bias_c) for h in KV]
    m_c = [jnp.maximum(jnp.max(s, -1, keepdims=True), 0.1 * NEG) for s in s_c]
    acc_c = [_dot(jnp.exp2(s_c[h] - m_c[h]).astype(BF16), vc_ref[0, h]) for h in KV]
    p_n = [a[:, 0:LANES] * (1.0 / jnp.maximum(a[:, LANES:VSLOT], 1e-30)) for a in acc_c]
    for h in KV:
        cmp_ref[h] = p_n[h]
    imp = [functools.reduce(jnp.add, [p[g * TQ:(g + 1) * TQ] for g in range(NSA_G)]).T[_L_OVL:_L_OVL + n_blk, :]
           for p in p_n]
    score = [jnp.where(blk_ok, x + jnp.where(forced, SEL_FORCE, 0.0), -1.0) for x in imp]
    rank = [jnp.zeros((n_blk, TQ), jnp.int32) for _ in KV]
    for i in range(n_blk):
        for h in KV:
            ri = score[h][i:i + 1, :]
            beats = (ri > score[h]) | ((ri == score[h]) & (j_blk > i))
            rank[h] = rank[h] + beats.astype(jnp.int32)
    for h in KV:
        bias_t = jnp.where((rank[h] < top_n) & blk_ok, 0.0, NEG)
        bias_q = jnp.concatenate([jnp.zeros((NSA_D, TQ), F32), bias_t,
                                  jnp.zeros((LANES - NSA_D - n_blk, TQ), F32)], axis=0).T
        qaug_ref[h] = jnp.concatenate(
            [jnp.where(lane_q < NSA_D, x, bias_q).astype(BF16) for x in q_slabs(h)], axis=0)

    def chunk(ref, start, kvh, slot=LANES):
        return ref[pl.ds(start, CK), kvh * slot:(kvh + 1) * slot]

    def score_pass(qs_ref, k_ref, ci, slot, bias):
        start = pl.multiple_of(ci * CK, CK)
        for kvh in range(NSA_KV):
            s = _dot_nt(qs_ref[kvh], chunk(k_ref, start, kvh))
            if bias is not None:
                s = add_bias(s, bias)
            sbuf[kvh, slot] = s
            mt_ref[kvh] = jnp.maximum(mt_ref[kvh], fold_max(s))

    def value_pass(v_ref, ci, slot, out_ref):
        start = pl.multiple_of(ci * CK, CK)
        for kvh in range(NSA_KV):
            p = jnp.exp2(sbuf[kvh, slot] - lanes_ck(mt_ref[kvh]))
            out_ref[kvh] += _dot(p.astype(BF16), chunk(v_ref, start, kvh, VSLOT))

    def begin_max():
        for kvh in range(NSA_KV):
            mt_ref[kvh] = jnp.full((R, LANES), -3e38, F32)

    def finish_max(out_ref):
        for kvh in range(NSA_KV):
            mt_ref[kvh] = jnp.broadcast_to(jnp.max(mt_ref[kvh], -1, keepdims=True), (R, LANES))
            out_ref[kvh] = jnp.zeros((R, VSLOT), F32)

    def loop(n, body):
        lax.fori_loop(0, n, lambda i, c: (body(i), c)[1], 0)

    def win_chunk(k):
        cw = last - (NW - 1) + k
        return cw, jnp.maximum(cw, 0)

    def win_scores(k):
        cw, cidx = win_chunk(k)
        key = cidx * CK + key_l
        ok = (key <= t_q) & (key > t_q - WINDOW) & (cw >= 0)
        score_pass(qpl_ref, kw_ref, cidx, k, jnp.where(ok, 0.0, NEG))

    begin_max()
    loop(last // 2, lambda j: (score_pass(qaug_ref, ks_ref, 2 * j, 2 * j, None),
                               score_pass(qaug_ref, ks_ref, 2 * j + 1, 2 * j + 1, None)))
    loop(last % 2, lambda j: score_pass(qaug_ref, ks_ref, last - 1, last - 1, None))
    score_pass(qaug_ref, ks_ref, last, last, bias_diag)
    finish_max(acc_ref)
    loop(n_sel // 2, lambda j: (value_pass(vs_ref, 2 * j, 2 * j, acc_ref),
                                value_pass(vs_ref, 2 * j + 1, 2 * j + 1, acc_ref)))
    loop(n_sel % 2, lambda j: value_pass(vs_ref, last, last, acc_ref))

    begin_max()
    for k in range(NW):
        win_scores(k)
    finish_max(accw_ref)
    for k in range(NW):
        value_pass(vw_ref, win_chunk(k)[1], k, accw_ref)

    left = lane_q < NSA_D
    for kvh in range(NSA_KV):
        for pair in range(NSA_G // 2):
            col = (kvh * NSA_G + 2 * pair) * NSA_D
            r_e = pl.ds(2 * pair * TQ, TQ)
            r_o = pl.ds((2 * pair + 1) * TQ, TQ)

            def packed(ref, lanes):
                return jnp.where(left, ref[kvh, r_e, lanes], pltpu.roll(ref[kvh, r_o, lanes], NSA_D, 1))

            out = packed(cmp_ref, slice(0, LANES)) * gexp_ref[:, col:col + LANES]
            for br, ref in ((1, acc_ref), (2, accw_ref)):
                den = jnp.where(left, ref[kvh, r_e, LANES:VSLOT], ref[kvh, r_o, LANES:VSLOT])
                gate = gexp_ref[:, br * NSA_QW + col:br * NSA_QW + col + LANES]
                out = out + packed(ref, slice(0, LANES)) * (gate / den)
            o_ref[:, col:col + LANES] = out.astype(BF16)


def _window_chunks(S, TQ, CK):
    return max((q0 + TQ - 1) // CK - max(q0 - WINDOW + 1, 0) // CK + 1 for q0 in range(0, S, TQ))


def _nsa(q, ks, vs, kw, vw, kc, vc, gt, B, S, TQ, CK):
    T = B * S
    n_blk = S // SEL_BLOCK
    n_cmp = (S - CMP_LEN) // CMP_STRIDE + 1
    R = NSA_G * TQ
    NW = _window_chunks(S, TQ, CK)
    assert kc.shape[2] == LANES and n_blk <= LANES - _L_OVL and S // CK >= NW
    seq = lambda w: pl.BlockSpec((S, w), lambda b, i: (b, 0))
    qrow = lambda w: pl.BlockSpec((TQ, w), lambda b, i: (b * (S // TQ) + i, 0))
    cmp_spec = lambda w: pl.BlockSpec((1, NSA_KV, LANES, w), lambda b, i: (b, 0, 0, 0))
    kern = functools.partial(_nsa_kernel, TQ=TQ, CK=CK, NW=NW, n_cmp=n_cmp, n_blk=n_blk,
                             top_n=min(SEL_TOPN, n_blk))
    per_head = lambda w, dt: pltpu.VMEM((NSA_KV, R, w), dt)
    gx = jnp.asarray(_gate_expand_table(), BF16)
    return pl.pallas_call(
        kern,
        grid=(B, S // TQ),
        in_specs=[qrow(NSA_QW), seq(SLOT_W), seq(VSLOT_W), seq(SLOT_W), seq(VSLOT_W),
                  cmp_spec(LANES), cmp_spec(VSLOT), qrow(LANES), _const_spec(gx.shape)],
        out_specs=qrow(NSA_QW),
        out_shape=jax.ShapeDtypeStruct((T, NSA_QW), BF16),
        scratch_shapes=[pltpu.VMEM((NSA_KV, S // CK, R, CK), F32),
                        per_head(LANES, BF16), per_head(LANES, BF16),
                        per_head(VSLOT, F32), per_head(VSLOT, F32),
                        per_head(LANES, F32),
                        per_head(LANES, F32),
                        pltpu.VMEM((TQ, 3 * NSA_QW), F32)],
        compiler_params=_params(("parallel", "parallel")),
        name="nsa",
    )(q, ks, vs, kw, vw, kc, vc, gt, gx)


def _merge_kernel(x_ref, a_ref, b_ref, wga_ref, wgb_ref, wua_ref, wub_ref, wo_ref, g_ref, bb_ref, o_ref):
    x = x_ref[...]
    xb = x.astype(BF16)
    m = (_sigmoid(_dot(xb, wga_ref[...])) * _dot(a_ref[...], wua_ref[...])
         + _sigmoid(_dot(xb, wgb_ref[...])) * _dot(b_ref[...], wub_ref[...]))
    mix = _dot(m.astype(BF16), wo_ref[...])
    o_ref[...] = _layer_norm(DN_ALPHA * x + mix, g_ref[...], bb_ref[...])


def _merge(x2, a, b, wga, wgb, wua, wub, wo, g, bb, tm):
    T = x2.shape[0]
    row = pl.BlockSpec((tm, D_MODEL), lambda i: (i, 0))
    wsp = _const_spec((D_MODEL, D_MODEL))
    vec = _const_spec((1, D_MODEL))
    return pl.pallas_call(
        _merge_kernel,
        grid=(T // tm,),
        in_specs=[row, row, row, wsp, wsp, wsp, wsp, wsp, vec, vec],
        out_specs=row,
        out_shape=jax.ShapeDtypeStruct((T, D_MODEL), F32),
        compiler_params=_params(("parallel",)),
        name="merge",
    )(x2, a, b, wga, wgb, wua, wub, wo, g, bb)


def _ffn_kernel(h_ref, wg_ref, wu_ref, wd_ref, g_ref, b_ref, o_ref):
    h = h_ref[...]
    hb = h.astype(BF16)
    gate = _dot(hb, wg_ref[...])
    act = (gate * _sigmoid(gate) * _dot(hb, wu_ref[...])).astype(BF16)
    ffn = _dot(act, wd_ref[...])
    o_ref[...] = _layer_norm(DN_ALPHA * h + ffn, g_ref[...], b_ref[...])


def _ffn(h, wg, wu, wd, g, b, tm):
    T = h.shape[0]
    d_ff = wg.shape[1]
    row = pl.BlockSpec((tm, D_MODEL), lambda i: (i, 0))
    vec = _const_spec((1, D_MODEL))
    once = pl.Buffered(1)
    return pl.pallas_call(
        _ffn_kernel,
        grid=(T // tm,),
        in_specs=[row,
                  pl.BlockSpec((D_MODEL, d_ff), lambda i: (0, 0), pipeline_mode=once),
                  pl.BlockSpec((D_MODEL, d_ff), lambda i: (0, 0), pipeline_mode=once),
                  pl.BlockSpec((d_ff, D_MODEL), lambda i: (0, 0), pipeline_mode=once),
                  vec, vec],
        out_specs=row,
        out_shape=jax.ShapeDtypeStruct((T, D_MODEL), F32),
        compiler_params=_params(("parallel",)),
        name="ffn",
    )(h, wg, wu, wd, g, b)


def _pad_slots(w):
    d = w.shape[0]
    w = w.reshape(d, NSA_KV, NSA_D)
    return jnp.pad(w, ((0, 0), (0, 0), (0, LANES - NSA_D))).reshape(d, SLOT_W)


def _cmp_value_table(n_half, n_cmp, n_blk):
    cs = np.arange(n_cmp)[:, None] * CMP_STRIDE
    bs = np.arange(n_blk)[None, :] * SEL_BLOCK
    ov = np.minimum(cs + CMP_LEN, bs + SEL_BLOCK) - np.maximum(cs, bs)
    out = np.zeros((n_half, VSLOT), np.float32)
    out[:, LANES:] = 1.0
    out[:n_cmp, _L_OVL:_L_OVL + n_blk] = np.clip(ov, 0, None) / CMP_LEN
    return out


def _gate_expand_table():
    out = np.zeros((LANES, 3 * NSA_QW), np.float32)
    for br in range(3):
        for h in range(NSA_HEADS):
            c = br * NSA_QW + h * NSA_D
            out[br * NSA_HEADS + h, c:c + NSA_D] = 1.0
            out[LANES // 2 + br * NSA_HEADS + h, c:c + NSA_D] = 1.0
    return out


def kernel(x, positions, w_in, hg_lb_logits, hg_norm_g, cmp_k_pos, cmp_k_w1, cmp_k_b1, cmp_k_w2, cmp_k_b2, cmp_v_pos, cmp_v_w1, cmp_v_b1, cmp_v_w2, cmp_v_b2, w_up_hg, w_up_nsa, w_o, ln1_g, ln1_b, w_ffn_gate, w_ffn_up, w_ffn_down, ln2_g, ln2_b):
    B, S, _ = x.shape
    T = B * S
    tm = min(512, S)
    TQ = 128
    CK = min(256, S)
    n_blk = S // SEL_BLOCK
    n_cmp = (S - CMP_LEN) // CMP_STRIDE + 1
    n_half = S // CMP_STRIDE

    lb_table = jnp.cumsum(jax.nn.softmax(hg_lb_logits.astype(F32), axis=0), axis=0)
    x2 = x.reshape(T, D_MODEL)

    w = w_in[0]
    o = 0
    o += 4 * HG_W
    wq = w[:, o:o + NSA_QW]; o += NSA_QW
    kv = []
    for _ in range(6):
        kv.append(w[:, o:o + NSA_KVW]); o += NSA_KVW
    w_kc, w_vc, w_ks, w_vs, w_kw, w_vw = kv
    w_gt = jnp.pad(w[:, o:o + 3 * NSA_HEADS], ((0, 0), (0, LANES - 3 * NSA_HEADS))); o += 3 * NSA_HEADS
    w_ga = w[:, o:o + D_MODEL]; o += D_MODEL
    w_gb = w[:, o:o + D_MODEL]
    w_nsa = jnp.concatenate([wq, w_kc, _pad_slots(w_ks), _pad_slots(w_kw),
                             w_vc, _pad_slots(w_vs), _pad_slots(w_vw), w_gt], axis=1).astype(BF16)

    half = NSA_D // 2
    inv_freq = ROPE_THETA ** (-jnp.arange(half, dtype=F32) / half)
    ang = positions.astype(F32).reshape(T, 1) * inv_freq[None, :]
    cos_t = jnp.tile(jnp.cos(ang), (1, LANES // half))
    sin_h = jnp.sin(ang)
    sin_t = jnp.tile(jnp.concatenate([-sin_h, sin_h], axis=1), (1, LANES // NSA_D))

    e_np = np.zeros((S, NSA_KV, LANES), np.float32)
    e_np[np.arange(S), :, NSA_D + np.arange(S) // SEL_BLOCK] = 1.0
    e_tab = jnp.asarray(e_np.reshape(S, SLOT_W), BF16)

    hq, hf, hi, hg = _hg_proj(x2, w.astype(BF16), tm)
    a = _hgrn(hq, hf, hi, hg, lb_table[0:1], hg_norm_g[0:1].astype(F32), B, S)

    q, kc_tok, ks, kw, vc_tok, vs, vw, gt = _nsa_proj(x2, w_nsa, cos_t, sin_t, e_tab, tm, S)

    def half_blocks(t):
        t = t.reshape(B, n_half, CMP_STRIDE, NSA_KV, NSA_D)
        return t.transpose(0, 3, 1, 2, 4).reshape(B, NSA_KV, n_half, CMP_STRIDE * NSA_D)

    def pos_rows(p):
        return jnp.broadcast_to(p.reshape(1, CMP_LEN * NSA_D), (8, CMP_LEN * NSA_D)).astype(BF16)

    def pad_out(w2, b2, width):
        return (jnp.pad(w2, ((0, 0), (0, width - NSA_D))).astype(BF16),
                jnp.pad(b2, (0, width - NSA_D)).reshape(1, width).astype(F32))

    w2k, b2k = pad_out(cmp_k_w2[0], cmp_k_b2[0], LANES)
    w2v, b2v = pad_out(cmp_v_w2[0], cmp_v_b2[0], VSLOT)
    kc, vc = _compress(
        half_blocks(kc_tok), half_blocks(vc_tok),
        pos_rows(cmp_k_pos[0]), cmp_k_w1[0].astype(BF16), cmp_k_b1[0].reshape(1, -1).astype(F32), w2k, b2k,
        pos_rows(cmp_v_pos[0]), cmp_v_w1[0].astype(BF16), cmp_v_b1[0].reshape(1, -1).astype(F32), w2v, b2v,
        jnp.asarray(_cmp_value_table(n_half, n_cmp, n_blk)))

    b_out = _nsa(q, ks, vs, kw, vw, kc, vc, gt, B, S, TQ, CK)

    h1 = _merge(x2, a, b_out, w_ga.astype(BF16), w_gb.astype(BF16),
                w_up_hg[0].astype(BF16), w_up_nsa[0].astype(BF16), w_o[0].astype(BF16),
                ln1_g[0].reshape(1, -1).astype(F32), ln1_b[0].reshape(1, -1).astype(F32), tm)
    out = _ffn(h1, w_ffn_gate[0].astype(BF16), w_ffn_up[0].astype(BF16), w_ffn_down[0].astype(BF16),
               ln2_g[0].reshape(1, -1).astype(F32), ln2_b[0].reshape(1, -1).astype(F32), tm)
    return out.reshape(B, S, D_MODEL)
```

```python
import functools

import numpy as np
import jax
import jax.numpy as jnp
from jax import lax
from jax.experimental import pallas as pl
from jax.experimental.pallas import tpu as pltpu

F32 = jnp.float32
BF16 = jnp.bfloat16

D_MODEL = 1024
HG_HEADS = 8
HG_DK = 128
HG_DV = 128
HG_W = HG_HEADS * HG_DK
HG_CHUNK = 64
NSA_HEADS = 16
NSA_KV = 4
NSA_G = 4
NSA_D = 64
NSA_QW = NSA_HEADS * NSA_D
NSA_KVW = NSA_KV * NSA_D
CMP_LEN = 32
CMP_STRIDE = 16
CMP_HIDDEN = 256
SEL_BLOCK = 64
SEL_TOPN = 8
SEL_FORCE = 1000.0
WINDOW = 512
ROPE_THETA = 10000.0
DEPTH = 1
DN_ALPHA = (2.0 * DEPTH) ** 0.25
LN_EPS = 1e-5
RMS_EPS = 1e-6
LOG2E = 1.4426950408889634
NEG = -1e30

LANES = 128
SLOT_W = NSA_KV * LANES
VSLOT = 2 * LANES
VSLOT_W = NSA_KV * VSLOT
VMEM_LIMIT = 56 * 1024 * 1024


def _dot(a, b):
    return jnp.dot(a, b, preferred_element_type=F32)


def _dot_nt(a, b):
    return lax.dot_general(a, b, (((1,), (1,)), ((), ())), preferred_element_type=F32)


def _dot_tn(a, b):
    return lax.dot_general(a, b, (((0,), (0,)), ((), ())), preferred_element_type=F32)


def _sigmoid(x):
    return 1.0 / (1.0 + jnp.exp(-x))


def _layer_norm(x, g, b):
    mu = jnp.mean(x, -1, keepdims=True)
    xc = x - mu
    var = jnp.mean(xc * xc, -1, keepdims=True)
    return xc * lax.rsqrt(var + LN_EPS) * g + b


def _params(sem):
    return pltpu.CompilerParams(dimension_semantics=sem, vmem_limit_bytes=VMEM_LIMIT)


def _const_spec(shape):
    nd = len(shape)
    return pl.BlockSpec(shape, lambda *_: (0,) * nd)


def _hg_proj_kernel(x_ref, w_ref, q_ref, f_ref, i_ref, g_ref):
    x = x_ref[...].astype(BF16)
    q_ref[...] = _dot(x, w_ref[:, 0:HG_W]).astype(BF16)
    f_ref[...] = _dot(x, w_ref[:, HG_W:2 * HG_W])
    i_ref[...] = _dot(x, w_ref[:, 2 * HG_W:3 * HG_W]).astype(BF16)
    g = _dot(x, w_ref[:, 3 * HG_W:4 * HG_W])
    g_ref[...] = (g * _sigmoid(g)).astype(BF16)


def _hg_proj(x2, w_hg, tm):
    T = x2.shape[0]
    row = lambda i: (i, 0)
    out_spec = pl.BlockSpec((tm, HG_W), row)
    return pl.pallas_call(
        _hg_proj_kernel,
        grid=(T // tm,),
        in_specs=[pl.BlockSpec((tm, D_MODEL), row), _const_spec((D_MODEL, 4 * HG_W))],
        out_specs=[out_spec] * 4,
        out_shape=[jax.ShapeDtypeStruct((T, HG_W), BF16), jax.ShapeDtypeStruct((T, HG_W), F32),
                   jax.ShapeDtypeStruct((T, HG_W), BF16), jax.ShapeDtypeStruct((T, HG_W), BF16)],
        compiler_params=_params(("parallel",)),
        name="hg_proj",
    )(x2, w_hg)


_NQ = NSA_QW
_NKC = NSA_KVW
_NSL = SLOT_W
_ROPE_W = _NQ + _NKC + 2 * _NSL
_O_KC = _NQ
_O_KS = _NQ + _NKC
_O_KW = _O_KS + _NSL
_O_VC = _ROPE_W
_O_VS = _O_VC + _NKC
_O_VW = _O_VS + _NSL
_O_GT = _O_VW + _NSL
_NSA_W = _O_GT + LANES


def _nsa_proj_kernel(x_ref, w_ref, cos_ref, sin_ref, e_ref,
                     q_ref, kc_ref, ks_ref, kw_ref, vc_ref, vs_ref, vw_ref, gt_ref):
    x = x_ref[...].astype(BF16)
    cos = cos_ref[...]
    sin = sin_ref[...]
    half = NSA_D // 2

    def rope(y):
        w = y.shape[1]
        reps = w // LANES
        lane = lax.broadcasted_iota(jnp.int32, y.shape, 1)
        fwd = pltpu.roll(y, w - half, 1)
        bwd = pltpu.roll(y, half, 1)
        rot = jnp.where((lane % NSA_D) < half, fwd, bwd)
        return y * jnp.tile(cos, (1, reps)) + rot * jnp.tile(sin, (1, reps))

    yq = rope(_dot(x, w_ref[:, 0:_NQ]))
    q_ref[...] = (yq * (NSA_D ** -0.5 * LOG2E)).astype(BF16)

    def per_head(y, o_ref):
        left = lax.broadcasted_iota(jnp.int32, (y.shape[0], LANES), 1) < NSA_D
        for c in range(y.shape[1] // LANES):
            two = y[:, c * LANES:(c + 1) * LANES]
            o_ref[2 * c] = jnp.where(left, two, 0.0)
            o_ref[2 * c + 1] = jnp.where(left, pltpu.roll(two, NSA_D, 1), 0.0)

    per_head(rope(_dot(x, w_ref[:, _O_KC:_O_KC + _NKC])), kc_ref)
    ks_ref[...] = rope(_dot(x, w_ref[:, _O_KS:_O_KS + _NSL])).astype(BF16) + e_ref[...]
    kw_ref[...] = rope(_dot(x, w_ref[:, _O_KW:_O_KW + _NSL])).astype(BF16)
    per_head(_dot(x, w_ref[:, _O_VC:_O_VC + _NKC]), vc_ref)
    def value_slots(y):
        ones = jnp.ones((y.shape[0], LANES), BF16)
        parts = []
        for h in range(NSA_KV):
            parts += [y[:, h * LANES:(h + 1) * LANES].astype(BF16), ones]
        return jnp.concatenate(parts, axis=1)

    vs_ref[...] = value_slots(_dot(x, w_ref[:, _O_VS:_O_VS + _NSL]))
    vw_ref[...] = value_slots(_dot(x, w_ref[:, _O_VW:_O_VW + _NSL]))
    gates = _sigmoid(_dot(x, w_ref[:, _O_GT:_O_GT + LANES]))
    g_hi = gates.astype(BF16).astype(F32)
    lane = lax.broadcasted_iota(jnp.int32, gates.shape, 1)
    gt_ref[...] = jnp.where(lane < LANES // 2, g_hi, pltpu.roll(gates - g_hi, LANES // 2, 1)).astype(BF16)


def _nsa_proj(x2, w_nsa, cos_t, sin_t, e_tab, tm, S):
    T = x2.shape[0]
    row = lambda i: (i, 0)
    tiles_per_seq = S // tm
    tok = lambda w: (pl.BlockSpec((tm, w), row), jax.ShapeDtypeStruct((T, w), BF16))
    heads = (pl.BlockSpec((NSA_KV, tm, LANES), lambda i: (0, i, 0)),
             jax.ShapeDtypeStruct((NSA_KV, T, LANES), F32))
    outs = [tok(_NQ), heads, tok(_NSL), tok(_NSL), heads, tok(VSLOT_W), tok(VSLOT_W), tok(LANES)]
    return pl.pallas_call(
        _nsa_proj_kernel,
        grid=(T // tm,),
        in_specs=[pl.BlockSpec((tm, D_MODEL), row), _const_spec((D_MODEL, _NSA_W)),
                  pl.BlockSpec((tm, LANES), row), pl.BlockSpec((tm, LANES), row),
                  pl.BlockSpec((tm, _NSL), lambda i: (i % tiles_per_seq, 0))],
        out_specs=[spec for spec, _ in outs],
        out_shape=[shape for _, shape in outs],
        compiler_params=_params(("parallel",)),
        name="nsa_proj",
    )(x2, w_nsa, cos_t, sin_t, e_tab)


def _hgrn_kernel(q_ref, f_ref, i_ref, g_ref, lb_ref, ng_ref, o_ref, *, n_chunks, heads, unroll):
    C = HG_CHUNK
    ng = ng_ref[...]
    r = lax.broadcasted_iota(jnp.int32, (C, C), 0)
    c = lax.broadcasted_iota(jnp.int32, (C, C), 1)
    tril = r >= c
    row = lax.broadcasted_iota(jnp.int32, (C, HG_DK), 0)

    def chunk_cumsum(x):
        d = 1
        while d < C:
            x = x + jnp.where(row >= d, pltpu.roll(x, d, 0), 0.0)
            d *= 2
        return x

    def body(cj, states):
        states = list(states)
        items = [(u, h) for u in range(unroll) for h in range(heads)]
        rows = {it: pl.ds(pl.multiple_of((cj * unroll + it[0]) * C, C), C) for it in items}
        cols = {it: slice(it[1] * HG_DK, (it[1] + 1) * HG_DK) for it in items}
        f, eb, q_dec, k_inv, k_end, attn, out = {}, {}, {}, {}, {}, {}, {}
        for it in items:
            lb = lb_ref[:, cols[it]]
            f[it] = lb + (1.0 - lb) * _sigmoid(f_ref[rows[it], cols[it]])
        for it in items:
            eb[it] = jnp.exp(chunk_cumsum(jnp.log(f[it])))
        for it in items:
            q_dec[it] = (q_ref[rows[it], cols[it]].astype(F32) * eb[it]).astype(BF16)
            k_inv_f = (1.0 - f[it]) * (1.0 / eb[it])
            k_inv[it] = k_inv_f.astype(BF16)
            k_end[it] = (k_inv_f * eb[it][C - 1:C, :]).astype(BF16)
        for it in items:
            attn[it] = jnp.where(tril, _dot_nt(q_dec[it], k_inv[it]), 0.0).astype(BF16)
        for it in items:
            h = it[1]
            v = i_ref[rows[it], cols[it]]
            out[it] = _dot(attn[it], v) + _dot_nt(q_dec[it], states[h].astype(BF16))
            states[h] = states[h] * eb[it][C - 1:C, :] + _dot_tn(v, k_end[it])
        for it in items:
            o = out[it]
            on = o * lax.rsqrt(jnp.mean(o * o, -1, keepdims=True) + RMS_EPS) * ng
            o_ref[rows[it], cols[it]] = (on * g_ref[rows[it], cols[it]].astype(F32)).astype(BF16)
        return tuple(states)

    init = tuple(jnp.zeros((HG_DV, HG_DK), F32) for _ in range(heads))
    lax.fori_loop(0, n_chunks // unroll, body, init)


def _hgrn(q, f, i, g, lb, ng, B, S, heads=8, unroll=4):
    T = B * S
    w = heads * HG_DK
    blk = pl.BlockSpec((S, w), lambda b, h: (b, h))
    return pl.pallas_call(
        functools.partial(_hgrn_kernel, n_chunks=S // HG_CHUNK, heads=heads, unroll=unroll),
        grid=(B, HG_HEADS // heads),
        in_specs=[blk, blk, blk, blk,
                  pl.BlockSpec((1, w), lambda b, h: (0, h)),
                  pl.BlockSpec((1, HG_DV), lambda b, h: (0, 0))],
        out_specs=blk,
        out_shape=jax.ShapeDtypeStruct((T, HG_W), BF16),
        compiler_params=_params(("parallel", "parallel")),
        name="hgrn",
    )(q, f, i, g, lb, ng)


def _compress_kernel(tk_ref, tv_ref, pk_ref, w1k_ref, b1k_ref, w2k_ref, b2k_ref,
                     pv_ref, w1v_ref, b1v_ref, w2v_ref, b2v_ref, tab_ref, ko_ref, vo_ref, u_ref, *, n_half):
    half = CMP_STRIDE * LANES

    def mlp(t_ref, pos_ref, w1_ref, b1_ref, w2_ref, b2_ref, o_ref, tab=None):
        w_top = w1_ref[0:half, :]
        w_bot = w1_ref[half:2 * half, :]
        base = _dot(pos_ref[...], w1_ref[...])[0:1, :] + b1_ref[...]
        for h in range(NSA_KV):
            for l in range(CMP_STRIDE):
                u_ref[:, l * LANES:(l + 1) * LANES] = t_ref[h, pl.ds(l, n_half, stride=CMP_STRIDE),
                                                            :].astype(BF16)
            u = u_ref[...]
            top = _dot(u, w_top)
            bot = _dot(u, w_bot)
            n = top.shape[0]
            bot_next = jnp.concatenate([bot[1:n, :], jnp.zeros((1, CMP_HIDDEN), F32)], axis=0)
            hid = top + bot_next + base
            hid = hid * _sigmoid(hid)
            out = _dot(hid.astype(BF16), w2_ref[...]) + b2_ref[...]
            if tab is not None:
                out = out + tab
            o_ref[0, h] = out.astype(BF16)

    mlp(tk_ref, pk_ref, w1k_ref, b1k_ref, w2k_ref, b2k_ref, ko_ref)
    mlp(tv_ref, pv_ref, w1v_ref, b1v_ref, w2v_ref, b2v_ref, vo_ref, tab_ref[...])


def _compress(tk, tv, pk, w1k, b1k, w2k, b2k, pv, w1v, b1v, w2v, b2v, tab, B, S):
    nh = S // CMP_STRIDE
    t_spec = pl.BlockSpec((NSA_KV, S, LANES), lambda b: (0, b, 0))
    o_spec = lambda w: pl.BlockSpec((1, NSA_KV, nh, w), lambda b: (b, 0, 0, 0))
    specs = lambda arrs: [_const_spec(a.shape) for a in arrs]
    return pl.pallas_call(
        functools.partial(_compress_kernel, n_half=nh),
        grid=(B,),
        in_specs=([t_spec, t_spec] + specs((pk, w1k, b1k, w2k, b2k)) + specs((pv, w1v, b1v, w2v, b2v))
                  + [_const_spec(tab.shape)]),
        out_specs=[o_spec(LANES), o_spec(VSLOT)],
        out_shape=[jax.ShapeDtypeStruct((B, NSA_KV, nh, LANES), BF16),
                   jax.ShapeDtypeStruct((B, NSA_KV, nh, VSLOT), BF16)],
        scratch_shapes=[pltpu.VMEM((nh, CMP_STRIDE * LANES), BF16)],
        compiler_params=_params(("parallel",)),
        name="compress",
    )(tk, tv, pk, w1k, b1k, w2k, b2k, pv, w1v, b1v, w2v, b2v, tab)


_L_OVL = LANES - 32


def _nsa_kernel(q_ref, ks_ref, vs_ref, kw_ref, vw_ref, kc_ref, vc_ref, gt_ref, gx_ref,
                o_ref, sbuf, qpl_ref, qaug_ref, acc_ref, accw_ref, mt_ref, cmp_ref, gexp_ref,
                *, TQ, CK, NW, n_cmp, n_blk, top_n):
    R = NSA_G * TQ
    NH = CK // LANES
    q0 = pl.program_id(1) * TQ
    lane_q = lax.broadcasted_iota(jnp.int32, (TQ, LANES), 1)
    t_q = q0 + lax.broadcasted_iota(jnp.int32, (TQ, 1), 0)

    def add_bias(s, bias):
        return jnp.concatenate([s[g * TQ:(g + 1) * TQ] + bias for g in range(NSA_G)], axis=0)

    def fold_max(s):
        return functools.reduce(jnp.maximum, [s[:, u * LANES:(u + 1) * LANES] for u in range(NH)])

    def lanes_ck(m):
        return jnp.concatenate([m] * NH, axis=1)

    bias_c = jnp.where((lane_q * CMP_STRIDE + (CMP_LEN - 1) <= t_q) & (lane_q < n_cmp), 0.0, NEG)
    key_l = lax.broadcasted_iota(jnp.int32, (TQ, CK), 1)
    n_sel = (q0 + TQ + CK - 1) // CK
    last = n_sel - 1
    bias_diag = jnp.where(last * CK + key_l <= t_q, 0.0, NEG)

    j_blk = lax.broadcasted_iota(jnp.int32, (n_blk, TQ), 0)
    cur = (q0 + lax.broadcasted_iota(jnp.int32, (n_blk, TQ), 1)) // SEL_BLOCK
    blk_ok = j_blk <= cur
    forced = (j_blk == 0) | (j_blk == cur) | (j_blk == cur - 1)

    gexp_ref[...] = _dot(gt_ref[...], gx_ref[...])

    def q_slabs(kvh):
        slabs = []
        for pair in range(NSA_G // 2):
            col = (kvh * NSA_G + 2 * pair) * NSA_D
            x = q_ref[:, col:col + LANES].astype(F32)
            slabs += [x, pltpu.roll(x, NSA_D, 1)]
        return slabs

    for kvh in range(NSA_KV):
        qpl_ref[kvh] = jnp.concatenate(
            [jnp.where(lane_q < NSA_D, x, 0.0).astype(BF16) for x in q_slabs(kvh)], axis=0)

    KV = range(NSA_KV)
    s_c = [add_bias(_dot_nt(qpl_ref[h], kc_ref[0, h]), bias_c) for h in KV]
    m_c = [jnp.maximum(jnp.max(s, -1, keepdims=True), 0.1 * NEG) for s in s_c]
    acc_c = [_dot(jnp.exp2(s_c[h] - m_c[h]).astype(BF16), vc_ref[0, h]) for h in KV]
    p_n = [a[:, 0:LANES] * (1.0 / jnp.maximum(a[:, LANES:VSLOT], 1e-30)) for a in acc_c]
    for h in KV:
        cmp_ref[h] = p_n[h]
    imp = [functools.reduce(jnp.add, [p[g * TQ:(g + 1) * TQ] for g in range(NSA_G)]).T[_L_OVL:_L_OVL + n_blk, :]
           for p in p_n]
    score = [jnp.where(blk_ok, x + jnp.where(forced, SEL_FORCE, 0.0), -1.0) for x in imp]
    rank = [jnp.zeros((n_blk, TQ), jnp.int32) for _ in KV]
    for i in range(n_blk):
        for h in KV:
            ri = score[h][i:i + 1, :]
            beats = (ri > score[h]) | ((ri == score[h]) & (j_blk > i))
            rank[h] = rank[h] + beats.astype(jnp.int32)
    for h in KV:
        bias_t = jnp.where((rank[h] < top_n) & blk_ok, 0.0, NEG)
        bias_q = jnp.concatenate([jnp.zeros((NSA_D, TQ), F32), bias_t,
                                  jnp.zeros((LANES - NSA_D - n_blk, TQ), F32)], axis=0).T
        qaug_ref[h] = jnp.concatenate(
            [jnp.where(lane_q < NSA_D, x, bias_q).astype(BF16) for x in q_slabs(h)], axis=0)

    def chunk(ref, start, kvh, slot=LANES):
        return ref[pl.ds(start, CK), kvh * slot:(kvh + 1) * slot]

    def score_pass(qs_ref, k_ref, ci, slot, bias):
        start = pl.multiple_of(ci * CK, CK)
        for kvh in range(NSA_KV):
            s = _dot_nt(qs_ref[kvh], chunk(k_ref, start, kvh))
            if bias is not None:
                s = add_bias(s, bias)
            sbuf[kvh, slot] = s
            mt_ref[kvh] = jnp.maximum(mt_ref[kvh], fold_max(s))

    def value_pass(v_ref, ci, slot, out_ref):
        start = pl.multiple_of(ci * CK, CK)
        for kvh in range(NSA_KV):
            p = jnp.exp2(sbuf[kvh, slot] - lanes_ck(mt_ref[kvh]))
            out_ref[kvh] += _dot(p.astype(BF16), chunk(v_ref, start, kvh, VSLOT))

    def begin_max():
        for kvh in range(NSA_KV):
            mt_ref[kvh] = jnp.full((R, LANES), -3e38, F32)

    def finish_max(out_ref):
        for kvh in range(NSA_KV):
            mt_ref[kvh] = jnp.broadcast_to(jnp.max(mt_ref[kvh], -1, keepdims=True), (R, LANES))
            out_ref[kvh] = jnp.zeros((R, VSLOT), F32)

    def loop(n, body):
        lax.fori_loop(0, n, lambda i, c: (body(i), c)[1], 0)

    def win_chunk(k):
        cw = last - (NW - 1) + k
        return cw, jnp.maximum(cw, 0)

    def win_scores(k):
        cw, cidx = win_chunk(k)
        key = cidx * CK + key_l
        ok = (key <= t_q) & (key > t_q - WINDOW) & (cw >= 0)
        score_pass(qpl_ref, kw_ref, cidx, k, jnp.where(ok, 0.0, NEG))

    begin_max()
    loop(last // 2, lambda j: (score_pass(qaug_ref, ks_ref, 2 * j, 2 * j, None),
                               score_pass(qaug_ref, ks_ref, 2 * j + 1, 2 * j + 1, None)))
    loop(last % 2, lambda j: score_pass(qaug_ref, ks_ref, last - 1, last - 1, None))
    score_pass(qaug_ref, ks_ref, last, last, bias_diag)
    finish_max(acc_ref)
    loop(n_sel // 2, lambda j: (value_pass(vs_ref, 2 * j, 2 * j, acc_ref),
                                value_pass(vs_ref, 2 * j + 1, 2 * j + 1, acc_ref)))
    loop(n_sel % 2, lambda j: value_pass(vs_ref, last, last, acc_ref))

    begin_max()
    for k in range(NW):
        win_scores(k)
    finish_max(accw_ref)
    for k in range(NW):
        value_pass(vw_ref, win_chunk(k)[1], k, accw_ref)

    left = lane_q < NSA_D
    for kvh in range(NSA_KV):
        for pair in range(NSA_G // 2):
            col = (kvh * NSA_G + 2 * pair) * NSA_D
            r_e = pl.ds(2 * pair * TQ, TQ)
            r_o = pl.ds((2 * pair + 1) * TQ, TQ)

            def packed(ref, lanes):
                return jnp.where(left, ref[kvh, r_e, lanes], pltpu.roll(ref[kvh, r_o, lanes], NSA_D, 1))

            out = packed(cmp_ref, slice(0, LANES)) * gexp_ref[:, col:col + LANES]
            for br, ref in ((1, acc_ref), (2, accw_ref)):
                den = jnp.where(left, ref[kvh, r_e, LANES:VSLOT], ref[kvh, r_o, LANES:VSLOT])
                gate = gexp_ref[:, br * NSA_QW + col:br * NSA_QW + col + LANES]
                out = out + packed(ref, slice(0, LANES)) * (gate / den)
            o_ref[:, col:col + LANES] = out.astype(BF16)


def _window_chunks(S, TQ, CK):
    return max((q0 + TQ - 1) // CK - max(q0 - WINDOW + 1, 0) // CK + 1 for q0 in range(0, S, TQ))


def _nsa(q, ks, vs, kw, vw, kc, vc, gt, B, S, TQ, CK):
    T = B * S
    n_blk = S // SEL_BLOCK
    n_cmp = (S - CMP_LEN) // CMP_STRIDE + 1
    R = NSA_G * TQ
    NW = _window_chunks(S, TQ, CK)
    assert kc.shape[2] == LANES and n_blk <= LANES - _L_OVL and S // CK >= NW
    seq = lambda w: pl.BlockSpec((S, w), lambda b, i: (b, 0))
    qrow = lambda w: pl.BlockSpec((TQ, w), lambda b, i: (b * (S // TQ) + i, 0))
    cmp_spec = lambda w: pl.BlockSpec((1, NSA_KV, LANES, w), lambda b, i: (b, 0, 0, 0))
    kern = functools.partial(_nsa_kernel, TQ=TQ, CK=CK, NW=NW, n_cmp=n_cmp, n_blk=n_blk,
                             top_n=min(SEL_TOPN, n_blk))
    per_head = lambda w, dt: pltpu.VMEM((NSA_KV, R, w), dt)
    gx = jnp.asarray(_gate_expand_table(), BF16)
    return pl.pallas_call(
        kern,
        grid=(B, S // TQ),
        in_specs=[qrow(NSA_QW), seq(SLOT_W), seq(VSLOT_W), seq(SLOT_W), seq(VSLOT_W),
                  cmp_spec(LANES), cmp_spec(VSLOT), qrow(LANES), _const_spec(gx.shape)],
        out_specs=qrow(NSA_QW),
        out_shape=jax.ShapeDtypeStruct((T, NSA_QW), BF16),
        scratch_shapes=[pltpu.VMEM((NSA_KV, S // CK, R, CK), F32),
                        per_head(LANES, BF16), per_head(LANES, BF16),
                        per_head(VSLOT, F32), per_head(VSLOT, F32),
                        per_head(LANES, F32),
                        per_head(LANES, F32),
                        pltpu.VMEM((TQ, 3 * NSA_QW), F32)],
        compiler_params=_params(("parallel", "parallel")),
        name="nsa",
    )(q, ks, vs, kw, vw, kc, vc, gt, gx)


def _merge_kernel(x_ref, a_ref, b_ref, wga_ref, wgb_ref, wua_ref, wub_ref, wo_ref, g_ref, bb_ref, o_ref):
    x = x_ref[...]
    xb = x.astype(BF16)
    m = (_sigmoid(_dot(xb, wga_ref[...])) * _dot(a_ref[...], wua_ref[...])
         + _sigmoid(_dot(xb, wgb_ref[...])) * _dot(b_ref[...], wub_ref[...]))
    mix = _dot(m.astype(BF16), wo_ref[...])
    o_ref[...] = _layer_norm(DN_ALPHA * x + mix, g_ref[...], bb_ref[...])


def _merge(x2, a, b, wga, wgb, wua, wub, wo, g, bb, tm):
    T = x2.shape[0]
    row = pl.BlockSpec((tm, D_MODEL), lambda i: (i, 0))
    wsp = _const_spec((D_MODEL, D_MODEL))
    vec = _const_spec((1, D_MODEL))
    return pl.pallas_call(
        _merge_kernel,
        grid=(T // tm,),
        in_specs=[row, row, row, wsp, wsp, wsp, wsp, wsp, vec, vec],
        out_specs=row,
        out_shape=jax.ShapeDtypeStruct((T, D_MODEL), F32),
        compiler_params=_params(("parallel",)),
        name="merge",
    )(x2, a, b, wga, wgb, wua, wub, wo, g, bb)


def _ffn_kernel(h_ref, wg_ref, wu_ref, wd_ref, g_ref, b_ref, o_ref):
    h = h_ref[...]
    hb = h.astype(BF16)
    gate = _dot(hb, wg_ref[...])
    act = (gate * _sigmoid(gate) * _dot(hb, wu_ref[...])).astype(BF16)
    ffn = _dot(act, wd_ref[...])
    o_ref[...] = _layer_norm(DN_ALPHA * h + ffn, g_ref[...], b_ref[...])


def _ffn(h, wg, wu, wd, g, b, tm):
    T = h.shape[0]
    d_ff = wg.shape[1]
    row = pl.BlockSpec((tm, D_MODEL), lambda i: (i, 0))
    vec = _const_spec((1, D_MODEL))
    once = pl.Buffered(1)
    return pl.pallas_call(
        _ffn_kernel,
        grid=(T // tm,),
        in_specs=[row,
                  pl.BlockSpec((D_MODEL, d_ff), lambda i: (0, 0), pipeline_mode=once),
                  pl.BlockSpec((D_MODEL, d_ff), lambda i: (0, 0), pipeline_mode=once),
                  pl.BlockSpec((d_ff, D_MODEL), lambda i: (0, 0), pipeline_mode=once),
                  vec, vec],
        out_specs=row,
        out_shape=jax.ShapeDtypeStruct((T, D_MODEL), F32),
        compiler_params=_params(("parallel",)),
        name="ffn",
    )(h, wg, wu, wd, g, b)


def _pad_slots(w):
    d = w.shape[0]
    w = w.reshape(d, NSA_KV, NSA_D)
    return jnp.pad(w, ((0, 0), (0, 0), (0, LANES - NSA_D))).reshape(d, SLOT_W)


def _cmp_value_table(n_half, n_cmp, n_blk):
    cs = np.arange(n_cmp)[:, None] * CMP_STRIDE
    bs = np.arange(n_blk)[None, :] * SEL_BLOCK
    ov = np.minimum(cs + CMP_LEN, bs + SEL_BLOCK) - np.maximum(cs, bs)
    out = np.zeros((n_half, VSLOT), np.float32)
    out[:, LANES:] = 1.0
    out[:n_cmp, _L_OVL:_L_OVL + n_blk] = np.clip(ov, 0, None) / CMP_LEN
    return out


def _gate_expand_table():
    out = np.zeros((LANES, 3 * NSA_QW), np.float32)
    for br in range(3):
        for h in range(NSA_HEADS):
            c = br * NSA_QW + h * NSA_D
            out[br * NSA_HEADS + h, c:c + NSA_D] = 1.0
            out[LANES // 2 + br * NSA_HEADS + h, c:c + NSA_D] = 1.0
    return out


def kernel(x, positions, w_in, hg_lb_logits, hg_norm_g, cmp_k_pos, cmp_k_w1, cmp_k_b1, cmp_k_w2, cmp_k_b2, cmp_v_pos, cmp_v_w1, cmp_v_b1, cmp_v_w2, cmp_v_b2, w_up_hg, w_up_nsa, w_o, ln1_g, ln1_b, w_ffn_gate, w_ffn_up, w_ffn_down, ln2_g, ln2_b):
    B, S, _ = x.shape
    T = B * S
    tm = min(512, S)
    TQ = 128
    CK = min(256, S)
    n_blk = S // SEL_BLOCK
    n_cmp = (S - CMP_LEN) // CMP_STRIDE + 1
    n_half = S // CMP_STRIDE

    lb_table = jnp.cumsum(jax.nn.softmax(hg_lb_logits.astype(F32), axis=0), axis=0)
    x2 = x.reshape(T, D_MODEL)

    w = w_in[0]
    o = 0
    o += 4 * HG_W
    wq = w[:, o:o + NSA_QW]; o += NSA_QW
    kv = []
    for _ in range(6):
        kv.append(w[:, o:o + NSA_KVW]); o += NSA_KVW
    w_kc, w_vc, w_ks, w_vs, w_kw, w_vw = kv
    w_gt = jnp.pad(w[:, o:o + 3 * NSA_HEADS], ((0, 0), (0, LANES - 3 * NSA_HEADS))); o += 3 * NSA_HEADS
    w_ga = w[:, o:o + D_MODEL]; o += D_MODEL
    w_gb = w[:, o:o + D_MODEL]
    w_nsa = jnp.concatenate([wq, w_kc, _pad_slots(w_ks), _pad_slots(w_kw),
                             w_vc, _pad_slots(w_vs), _pad_slots(w_vw), w_gt], axis=1).astype(BF16)

    half = NSA_D // 2
    inv_freq = ROPE_THETA ** (-jnp.arange(half, dtype=F32) / half)
    ang = positions.astype(F32).reshape(T, 1) * inv_freq[None, :]
    cos_t = jnp.tile(jnp.cos(ang), (1, LANES // half))
    sin_h = jnp.sin(ang)
    sin_t = jnp.tile(jnp.concatenate([-sin_h, sin_h], axis=1), (1, LANES // NSA_D))

    e_np = np.zeros((S, NSA_KV, LANES), np.float32)
    e_np[np.arange(S), :, NSA_D + np.arange(S) // SEL_BLOCK] = 1.0
    e_tab = jnp.asarray(e_np.reshape(S, SLOT_W), BF16)

    hq, hf, hi, hg = _hg_proj(x2, w.astype(BF16), tm)
    a = _hgrn(hq, hf, hi, hg, lb_table[0:1], hg_norm_g[0:1].astype(F32), B, S)

    q, kc_tok, ks, kw, vc_tok, vs, vw, gt = _nsa_proj(x2, w_nsa, cos_t, sin_t, e_tab, tm, S)

    def slot_rows(m):
        m = m.reshape(CMP_LEN, NSA_D, -1)
        return jnp.pad(m, ((0, 0), (0, LANES - NSA_D), (0, 0))).reshape(CMP_LEN * LANES, -1)

    def pos_rows(p):
        flat = slot_rows(p.reshape(CMP_LEN * NSA_D, 1)).reshape(1, CMP_LEN * LANES)
        return jnp.broadcast_to(flat, (8, CMP_LEN * LANES)).astype(BF16)

    def pad_out(w2, b2, width):
        return (jnp.pad(w2, ((0, 0), (0, width - NSA_D))).astype(BF16),
                jnp.pad(b2, (0, width - NSA_D)).reshape(1, width).astype(F32))

    w2k, b2k = pad_out(cmp_k_w2[0], cmp_k_b2[0], LANES)
    w2v, b2v = pad_out(cmp_v_w2[0], cmp_v_b2[0], VSLOT)
    kc, vc = _compress(
        kc_tok, vc_tok,
        pos_rows(cmp_k_pos[0]), slot_rows(cmp_k_w1[0]).astype(BF16), cmp_k_b1[0].reshape(1, -1).astype(F32),
        w2k, b2k,
        pos_rows(cmp_v_pos[0]), slot_rows(cmp_v_w1[0]).astype(BF16), cmp_v_b1[0].reshape(1, -1).astype(F32),
        w2v, b2v,
        jnp.asarray(_cmp_value_table(n_half, n_cmp, n_blk)), B, S)

    b_out = _nsa(q, ks, vs, kw, vw, kc, vc, gt, B, S, TQ, CK)

    h1 = _merge(x2, a, b_out, w_ga.astype(BF16), w_gb.astype(BF16),
                w_up_hg[0].astype(BF16), w_up_nsa[0].astype(BF16), w_o[0].astype(BF16),
                ln1_g[0].reshape(1, -1).astype(F32), ln1_b[0].reshape(1, -1).astype(F32), tm)
    out = _ffn(h1, w_ffn_gate[0].astype(BF16), w_ffn_up[0].astype(BF16), w_ffn_down[0].astype(BF16),
               ln2_g[0].reshape(1, -1).astype(F32), ln2_b[0].reshape(1, -1).astype(F32), tm)
    return out.reshape(B, S, D_MODEL)
```

```python
import functools

import numpy as np
import jax
import jax.numpy as jnp
from jax import lax
from jax.experimental import pallas as pl
from jax.experimental.pallas import tpu as pltpu

F32 = jnp.float32
BF16 = jnp.bfloat16

D_MODEL = 1024
HG_HEADS = 8
HG_DK = 128
HG_DV = 128
HG_W = HG_HEADS * HG_DK
HG_CHUNK = 64
NSA_HEADS = 16
NSA_KV = 4
NSA_G = 4
NSA_D = 64
NSA_QW = NSA_HEADS * NSA_D
NSA_KVW = NSA_KV * NSA_D
CMP_LEN = 32
CMP_STRIDE = 16
CMP_HIDDEN = 256
SEL_BLOCK = 64
SEL_TOPN = 8
SEL_FORCE = 1000.0
WINDOW = 512
ROPE_THETA = 10000.0
DEPTH = 1
DN_ALPHA = (2.0 * DEPTH) ** 0.25
LN_EPS = 1e-5
RMS_EPS = 1e-6
LOG2E = 1.4426950408889634
NEG = -1e30

LANES = 128
SLOT_W = NSA_KV * LANES
VSLOT = 2 * LANES
VMEM_LIMIT = 56 * 1024 * 1024


def _dot(a, b):
    return jnp.dot(a, b, preferred_element_type=F32)


def _dot_nt(a, b):
    return lax.dot_general(a, b, (((1,), (1,)), ((), ())), preferred_element_type=F32)


def _dot_tn(a, b):
    return lax.dot_general(a, b, (((0,), (0,)), ((), ())), preferred_element_type=F32)


def _sigmoid(x):
    return 1.0 / (1.0 + jnp.exp(-x))


def _layer_norm(x, g, b):
    mu = jnp.mean(x, -1, keepdims=True)
    xc = x - mu
    var = jnp.mean(xc * xc, -1, keepdims=True)
    return xc * lax.rsqrt(var + LN_EPS) * g + b


def _params(sem):
    return pltpu.CompilerParams(dimension_semantics=sem, vmem_limit_bytes=VMEM_LIMIT)


def _const_spec(shape):
    nd = len(shape)
    return pl.BlockSpec(shape, lambda *_: (0,) * nd)


def _hg_proj_kernel(x_ref, w_ref, q_ref, f_ref, i_ref, g_ref):
    x = x_ref[...].astype(BF16)
    q_ref[...] = _dot(x, w_ref[:, 0:HG_W]).astype(BF16)
    f_ref[...] = _dot(x, w_ref[:, HG_W:2 * HG_W])
    i_ref[...] = _dot(x, w_ref[:, 2 * HG_W:3 * HG_W]).astype(BF16)
    g = _dot(x, w_ref[:, 3 * HG_W:4 * HG_W])
    g_ref[...] = (g * _sigmoid(g)).astype(BF16)


def _hg_proj(x2, w_hg, tm):
    T = x2.shape[0]
    row = lambda i: (i, 0)
    out_spec = pl.BlockSpec((tm, HG_W), row)
    return pl.pallas_call(
        _hg_proj_kernel,
        grid=(T // tm,),
        in_specs=[pl.BlockSpec((tm, D_MODEL), row), _const_spec((D_MODEL, 4 * HG_W))],
        out_specs=[out_spec] * 4,
        out_shape=[jax.ShapeDtypeStruct((T, HG_W), BF16), jax.ShapeDtypeStruct((T, HG_W), F32),
                   jax.ShapeDtypeStruct((T, HG_W), BF16), jax.ShapeDtypeStruct((T, HG_W), BF16)],
        compiler_params=_params(("parallel",)),
        name="hg_proj",
    )(x2, w_hg)


_NQ = NSA_QW
_NKC = NSA_KVW
_NSL = SLOT_W
_ROPE_W = _NQ + _NKC + 2 * _NSL
_O_KC = _NQ
_O_KS = _NQ + _NKC
_O_KW = _O_KS + _NSL
_O_VC = _ROPE_W
_O_VS = _O_VC + _NKC
_O_VW = _O_VS + _NSL
_O_GT = _O_VW + _NSL
_NSA_W = _O_GT + LANES


def _nsa_proj_kernel(x_ref, w_ref, cos_ref, sin_ref, e_ref,
                     q_ref, kc_ref, ks_ref, kw_ref, vc_ref, vs_ref, vw_ref, gt_ref):
    x = x_ref[...].astype(BF16)
    cos = cos_ref[...]
    sin = sin_ref[...]
    half = NSA_D // 2

    def rope(y):
        w = y.shape[1]
        reps = w // LANES
        lane = lax.broadcasted_iota(jnp.int32, y.shape, 1)
        fwd = pltpu.roll(y, w - half, 1)
        bwd = pltpu.roll(y, half, 1)
        rot = jnp.where((lane % NSA_D) < half, fwd, bwd)
        return y * jnp.tile(cos, (1, reps)) + rot * jnp.tile(sin, (1, reps))

    yq = rope(_dot(x, w_ref[:, 0:_NQ]))
    q_ref[...] = (yq * (NSA_D ** -0.5 * LOG2E)).astype(BF16)

    def per_head(y, o_ref):
        left = lax.broadcasted_iota(jnp.int32, (y.shape[0], LANES), 1) < NSA_D
        for c in range(y.shape[1] // LANES):
            two = y[:, c * LANES:(c + 1) * LANES]
            o_ref[2 * c] = jnp.where(left, two, 0.0)
            o_ref[2 * c + 1] = jnp.where(left, pltpu.roll(two, NSA_D, 1), 0.0)

    per_head(rope(_dot(x, w_ref[:, _O_KC:_O_KC + _NKC])), kc_ref)
    ks_ref[...] = rope(_dot(x, w_ref[:, _O_KS:_O_KS + _NSL])).astype(BF16) + e_ref[...]
    kw_ref[...] = rope(_dot(x, w_ref[:, _O_KW:_O_KW + _NSL])).astype(BF16)
    per_head(_dot(x, w_ref[:, _O_VC:_O_VC + _NKC]), vc_ref)
    vs_ref[...] = _dot(x, w_ref[:, _O_VS:_O_VS + _NSL]).astype(BF16)
    vw_ref[...] = _dot(x, w_ref[:, _O_VW:_O_VW + _NSL]).astype(BF16)
    gates = _sigmoid(_dot(x, w_ref[:, _O_GT:_O_GT + LANES]))
    g_hi = gates.astype(BF16).astype(F32)
    lane = lax.broadcasted_iota(jnp.int32, gates.shape, 1)
    gt_ref[...] = jnp.where(lane < LANES // 2, g_hi, pltpu.roll(gates - g_hi, LANES // 2, 1)).astype(BF16)


def _nsa_proj(x2, w_nsa, cos_t, sin_t, e_tab, tm, S):
    T = x2.shape[0]
    row = lambda i: (i, 0)
    tiles_per_seq = S // tm
    tok = lambda w: (pl.BlockSpec((tm, w), row), jax.ShapeDtypeStruct((T, w), BF16))
    heads = (pl.BlockSpec((NSA_KV, tm, LANES), lambda i: (0, i, 0)),
             jax.ShapeDtypeStruct((NSA_KV, T, LANES), F32))
    outs = [tok(_NQ), heads, tok(_NSL), tok(_NSL), heads, tok(_NSL), tok(_NSL), tok(LANES)]
    return pl.pallas_call(
        _nsa_proj_kernel,
        grid=(T // tm,),
        in_specs=[pl.BlockSpec((tm, D_MODEL), row), _const_spec((D_MODEL, _NSA_W)),
                  pl.BlockSpec((tm, LANES), row), pl.BlockSpec((tm, LANES), row),
                  pl.BlockSpec((tm, _NSL), lambda i: (i % tiles_per_seq, 0))],
        out_specs=[spec for spec, _ in outs],
        out_shape=[shape for _, shape in outs],
        compiler_params=_params(("parallel",)),
        name="nsa_proj",
    )(x2, w_nsa, cos_t, sin_t, e_tab)


def _hgrn_kernel(q_ref, f_ref, i_ref, g_ref, lb_ref, ng_ref, o_ref, *, n_chunks, heads, unroll):
    C = HG_CHUNK
    ng = ng_ref[...]
    r = lax.broadcasted_iota(jnp.int32, (C, C), 0)
    c = lax.broadcasted_iota(jnp.int32, (C, C), 1)
    tril = r >= c
    row = lax.broadcasted_iota(jnp.int32, (C, HG_DK), 0)

    def chunk_cumsum(x):
        d = 1
        while d < C:
            x = x + jnp.where(row >= d, pltpu.roll(x, d, 0), 0.0)
            d *= 2
        return x

    def body(cj, states):
        states = list(states)
        items = [(u, h) for u in range(unroll) for h in range(heads)]
        rows = {it: pl.ds(pl.multiple_of((cj * unroll + it[0]) * C, C), C) for it in items}
        cols = {it: slice(it[1] * HG_DK, (it[1] + 1) * HG_DK) for it in items}
        f, eb, q_dec, k_inv, k_end, attn, out = {}, {}, {}, {}, {}, {}, {}
        for it in items:
            lb = lb_ref[:, cols[it]]
            f[it] = lb + (1.0 - lb) * _sigmoid(f_ref[rows[it], cols[it]])
        for it in items:
            eb[it] = jnp.exp(chunk_cumsum(jnp.log(f[it])))
        for it in items:
            q_dec[it] = (q_ref[rows[it], cols[it]].astype(F32) * eb[it]).astype(BF16)
            k_inv_f = (1.0 - f[it]) * (1.0 / eb[it])
            k_inv[it] = k_inv_f.astype(BF16)
            k_end[it] = (k_inv_f * eb[it][C - 1:C, :]).astype(BF16)
        for it in items:
            attn[it] = jnp.where(tril, _dot_nt(q_dec[it], k_inv[it]), 0.0).astype(BF16)
        for it in items:
            h = it[1]
            v = i_ref[rows[it], cols[it]]
            out[it] = _dot(attn[it], v) + _dot_nt(q_dec[it], states[h].astype(BF16))
            states[h] = states[h] * eb[it][C - 1:C, :] + _dot_tn(v, k_end[it])
        for it in items:
            o = out[it]
            on = o * lax.rsqrt(jnp.mean(o * o, -1, keepdims=True) + RMS_EPS) * ng
            o_ref[rows[it], cols[it]] = (on * g_ref[rows[it], cols[it]].astype(F32)).astype(BF16)
        return tuple(states)

    init = tuple(jnp.zeros((HG_DV, HG_DK), F32) for _ in range(heads))
    lax.fori_loop(0, n_chunks // unroll, body, init)


def _hgrn(q, f, i, g, lb, ng, B, S, heads=8, unroll=4):
    T = B * S
    w = heads * HG_DK
    blk = pl.BlockSpec((S, w), lambda b, h: (b, h))
    return pl.pallas_call(
        functools.partial(_hgrn_kernel, n_chunks=S // HG_CHUNK, heads=heads, unroll=unroll),
        grid=(B, HG_HEADS // heads),
        in_specs=[blk, blk, blk, blk,
                  pl.BlockSpec((1, w), lambda b, h: (0, h)),
                  pl.BlockSpec((1, HG_DV), lambda b, h: (0, 0))],
        out_specs=blk,
        out_shape=jax.ShapeDtypeStruct((T, HG_W), BF16),
        compiler_params=_params(("parallel", "parallel")),
        name="hgrn",
    )(q, f, i, g, lb, ng)


def _compress_kernel(tk_ref, tv_ref, pk_ref, w1k_ref, b1k_ref, w2k_ref, b2k_ref,
                     pv_ref, w1v_ref, b1v_ref, w2v_ref, b2v_ref, tab_ref, ko_ref, vo_ref, u_ref, *, n_half):
    half = CMP_STRIDE * LANES

    def mlp(t_ref, pos_ref, w1_ref, b1_ref, w2_ref, b2_ref, o_ref, tab=None):
        w_top = w1_ref[0:half, :]
        w_bot = w1_ref[half:2 * half, :]
        base = _dot(pos_ref[...], w1_ref[...])[0:1, :] + b1_ref[...]
        for h in range(NSA_KV):
            for l in range(CMP_STRIDE):
                u_ref[h * n_half:(h + 1) * n_half, l * LANES:(l + 1) * LANES] = (
                    t_ref[h, pl.ds(l, n_half, stride=CMP_STRIDE), :].astype(BF16))
        u = u_ref[...]
        top = _dot(u, w_top)
        bot = _dot(u, w_bot)
        hids = []
        for h in range(NSA_KV):
            r0 = h * n_half
            bot_next = jnp.concatenate([bot[r0 + 1:r0 + n_half, :], jnp.zeros((1, CMP_HIDDEN), F32)], axis=0)
            hids.append(top[r0:r0 + n_half, :] + bot_next + base)
        hid = jnp.concatenate(hids, axis=0)
        hid = hid * _sigmoid(hid)
        out = _dot(hid.astype(BF16), w2_ref[...]) + b2_ref[...]
        for h in range(NSA_KV):
            out_h = out[h * n_half:(h + 1) * n_half, :]
            o_ref[0, h] = (out_h if tab is None else out_h + tab).astype(BF16)

    mlp(tk_ref, pk_ref, w1k_ref, b1k_ref, w2k_ref, b2k_ref, ko_ref)
    mlp(tv_ref, pv_ref, w1v_ref, b1v_ref, w2v_ref, b2v_ref, vo_ref, tab_ref[...])


def _compress(tk, tv, pk, w1k, b1k, w2k, b2k, pv, w1v, b1v, w2v, b2v, tab, B, S):
    nh = S // CMP_STRIDE
    t_spec = pl.BlockSpec((NSA_KV, S, LANES), lambda b: (0, b, 0))
    o_spec = lambda w: pl.BlockSpec((1, NSA_KV, nh, w), lambda b: (b, 0, 0, 0))
    specs = lambda arrs: [_const_spec(a.shape) for a in arrs]
    return pl.pallas_call(
        functools.partial(_compress_kernel, n_half=nh),
        grid=(B,),
        in_specs=([t_spec, t_spec] + specs((pk, w1k, b1k, w2k, b2k)) + specs((pv, w1v, b1v, w2v, b2v))
                  + [_const_spec(tab.shape)]),
        out_specs=[o_spec(LANES), o_spec(VSLOT)],
        out_shape=[jax.ShapeDtypeStruct((B, NSA_KV, nh, LANES), BF16),
                   jax.ShapeDtypeStruct((B, NSA_KV, nh, VSLOT), BF16)],
        scratch_shapes=[pltpu.VMEM((NSA_KV * nh, CMP_STRIDE * LANES), BF16)],
        compiler_params=_params(("parallel",)),
        name="compress",
    )(tk, tv, pk, w1k, b1k, w2k, b2k, pv, w1v, b1v, w2v, b2v, tab)


_L_OVL = LANES - 32


def _nsa_kernel(q_ref, ks_ref, vs_ref, kw_ref, vw_ref, kc_ref, vc_ref, gt_ref, gx_ref,
                o_ref, sbuf, qpl_ref, qaug_ref, acc_ref, accw_ref, mt_ref, cmp_ref, gexp_ref,
                *, TQ, CK, NW, n_cmp, n_blk, top_n):
    R = NSA_G * TQ
    NH = CK // LANES
    q0 = pl.program_id(1) * TQ
    lane_q = lax.broadcasted_iota(jnp.int32, (TQ, LANES), 1)
    t_q = q0 + lax.broadcasted_iota(jnp.int32, (TQ, 1), 0)

    def add_bias(s, bias):
        return jnp.concatenate([s[g * TQ:(g + 1) * TQ] + bias for g in range(NSA_G)], axis=0)

    def fold_max(s):
        return functools.reduce(jnp.maximum, [s[:, u * LANES:(u + 1) * LANES] for u in range(NH)])

    def lanes_ck(m):
        return jnp.concatenate([m] * NH, axis=1)

    bias_c = jnp.where((lane_q * CMP_STRIDE + (CMP_LEN - 1) <= t_q) & (lane_q < n_cmp), 0.0, NEG)
    key_l = lax.broadcasted_iota(jnp.int32, (TQ, CK), 1)
    n_sel = (q0 + TQ + CK - 1) // CK
    last = n_sel - 1
    bias_diag = jnp.where(last * CK + key_l <= t_q, 0.0, NEG)

    j_blk = lax.broadcasted_iota(jnp.int32, (n_blk, TQ), 0)
    cur = (q0 + lax.broadcasted_iota(jnp.int32, (n_blk, TQ), 1)) // SEL_BLOCK
    blk_ok = j_blk <= cur
    forced = (j_blk == 0) | (j_blk == cur) | (j_blk == cur - 1)

    gexp_ref[...] = _dot(gt_ref[...], gx_ref[...])

    def q_slabs(kvh):
        slabs = []
        for pair in range(NSA_G // 2):
            col = (kvh * NSA_G + 2 * pair) * NSA_D
            x = q_ref[:, col:col + LANES].astype(F32)
            slabs += [x, pltpu.roll(x, NSA_D, 1)]
        return slabs

    for kvh in range(NSA_KV):
        qpl_ref[kvh] = jnp.concatenate(
            [jnp.where(lane_q < NSA_D, x, 0.0).astype(BF16) for x in q_slabs(kvh)], axis=0)

    KV = range(NSA_KV)
    s_c = [add_bias(_dot_nt(qpl_ref[h], kc_ref[0, h]), bias_c) for h in KV]
    m_c = [jnp.maximum(jnp.max(s, -1, keepdims=True), 0.1 * NEG) for s in s_c]
    acc_c = [_dot(jnp.exp2(s_c[h] - m_c[h]).astype(BF16), vc_ref[0, h]) for h in KV]
    p_n = [a[:, 0:LANES] * (1.0 / jnp.maximum(a[:, LANES:VSLOT], 1e-30)) for a in acc_c]
    for h in KV:
        cmp_ref[h] = p_n[h]
    imp = [functools.reduce(jnp.add, [p[g * TQ:(g + 1) * TQ] for g in range(NSA_G)]).T[_L_OVL:_L_OVL + n_blk, :]
           for p in p_n]
    score = [jnp.where(blk_ok, x + jnp.where(forced, SEL_FORCE, 0.0), -1.0) for x in imp]
    rank = [jnp.zeros((n_blk, TQ), jnp.int32) for _ in KV]
    for i in range(n_blk):
        for h in KV:
            ri = score[h][i:i + 1, :]
            beats = (ri > score[h]) | ((ri == score[h]) & (j_blk > i))
            rank[h] = rank[h] + beats.astype(jnp.int32)
    for h in KV:
        bias_t = jnp.where((rank[h] < top_n) & blk_ok, 0.0, NEG)
        bias_q = jnp.concatenate([jnp.zeros((NSA_D, TQ), F32), bias_t,
                                  jnp.zeros((LANES - NSA_D - n_blk, TQ), F32)], axis=0).T
        qaug_ref[h] = jnp.concatenate(
            [jnp.where(lane_q < NSA_D, x, bias_q).astype(BF16) for x in q_slabs(h)], axis=0)

    def chunk(ref, start, kvh, slot=LANES):
        return ref[pl.ds(start, CK), kvh * slot:(kvh + 1) * slot]

    def score_pass(qs_ref, k_ref, ci, slot, bias, m_ref):
        start = pl.multiple_of(ci * CK, CK)
        for kvh in range(NSA_KV):
            s = _dot_nt(qs_ref[kvh], chunk(k_ref, start, kvh))
            if bias is not None:
                s = add_bias(s, bias)
            sbuf[kvh, slot] = s
            m_ref[kvh] = jnp.maximum(m_ref[kvh], fold_max(s))

    def value_pass(v_ref, ci, slot, m_ref, out_ref):
        start = pl.multiple_of(ci * CK, CK)
        for kvh in range(NSA_KV):
            p = jnp.exp2(sbuf[kvh, slot] - lanes_ck(m_ref[kvh]))
            v1 = jnp.concatenate([chunk(v_ref, start, kvh), jnp.ones((CK, LANES), BF16)], axis=1)
            out_ref[kvh] += _dot(p.astype(BF16), v1)

    def begin_max(m_ref):
        for kvh in range(NSA_KV):
            m_ref[kvh] = jnp.full((R, LANES), -3e38, F32)

    def finish_max(m_ref, out_ref):
        for kvh in range(NSA_KV):
            m_ref[kvh] = jnp.broadcast_to(jnp.max(m_ref[kvh], -1, keepdims=True), (R, LANES))
            out_ref[kvh] = jnp.zeros((R, VSLOT), F32)

    def loop(n, body):
        lax.fori_loop(0, n, lambda i, c: (body(i), c)[1], 0)

    def win_chunk(k):
        cw = last - (NW - 1) + k
        return cw, jnp.maximum(cw, 0)

    def win_scores(k):
        cw, cidx = win_chunk(k)
        key = cidx * CK + key_l
        ok = (key <= t_q) & (key > t_q - WINDOW) & (cw >= 0)
        score_pass(qpl_ref, kw_ref, cidx, k, jnp.where(ok, 0.0, NEG), mt_ref)

    sel_scores = lambda ci, bias=None: score_pass(qaug_ref, ks_ref, ci, ci, bias, mt_ref)
    sel_values = lambda ci: value_pass(vs_ref, ci, ci, mt_ref, acc_ref)
    begin_max(mt_ref)
    loop(last // 2, lambda j: (sel_scores(2 * j), sel_scores(2 * j + 1)))
    loop(last % 2, lambda j: sel_scores(last - 1))
    sel_scores(last, bias_diag)
    finish_max(mt_ref, acc_ref)
    loop(n_sel // 2, lambda j: (sel_values(2 * j), sel_values(2 * j + 1)))
    loop(n_sel % 2, lambda j: sel_values(last))

    begin_max(mt_ref)
    for k in range(NW):
        win_scores(k)
    finish_max(mt_ref, accw_ref)
    for k in range(NW):
        value_pass(vw_ref, win_chunk(k)[1], k, mt_ref, accw_ref)

    left = lane_q < NSA_D
    for kvh in range(NSA_KV):
        for pair in range(NSA_G // 2):
            col = (kvh * NSA_G + 2 * pair) * NSA_D
            r_e = pl.ds(2 * pair * TQ, TQ)
            r_o = pl.ds((2 * pair + 1) * TQ, TQ)

            def packed(ref, lanes):
                return jnp.where(left, ref[kvh, r_e, lanes], pltpu.roll(ref[kvh, r_o, lanes], NSA_D, 1))

            out = packed(cmp_ref, slice(0, LANES)) * gexp_ref[:, col:col + LANES]
            for br, ref in ((1, acc_ref), (2, accw_ref)):
                den = jnp.where(left, ref[kvh, r_e, LANES:VSLOT], ref[kvh, r_o, LANES:VSLOT])
                gate = gexp_ref[:, br * NSA_QW + col:br * NSA_QW + col + LANES]
                out = out + packed(ref, slice(0, LANES)) * (gate / den)
            o_ref[:, col:col + LANES] = out.astype(BF16)


def _window_chunks(S, TQ, CK):
    return max((q0 + TQ - 1) // CK - max(q0 - WINDOW + 1, 0) // CK + 1 for q0 in range(0, S, TQ))


def _nsa(q, ks, vs, kw, vw, kc, vc, gt, B, S, TQ, CK):
    T = B * S
    n_blk = S // SEL_BLOCK
    n_cmp = (S - CMP_LEN) // CMP_STRIDE + 1
    R = NSA_G * TQ
    NW = _window_chunks(S, TQ, CK)
    assert kc.shape[2] == LANES and n_blk <= LANES - _L_OVL and S // CK >= NW
    seq = lambda w: pl.BlockSpec((S, w), lambda b, i: (b, 0))
    qrow = lambda w: pl.BlockSpec((TQ, w), lambda b, i: (b * (S // TQ) + i, 0))
    cmp_spec = lambda w: pl.BlockSpec((1, NSA_KV, LANES, w), lambda b, i: (b, 0, 0, 0))
    kern = functools.partial(_nsa_kernel, TQ=TQ, CK=CK, NW=NW, n_cmp=n_cmp, n_blk=n_blk,
                             top_n=min(SEL_TOPN, n_blk))
    per_head = lambda w, dt: pltpu.VMEM((NSA_KV, R, w), dt)
    gx = jnp.asarray(_gate_expand_table(), BF16)
    return pl.pallas_call(
        kern,
        grid=(B, S // TQ),
        in_specs=[qrow(NSA_QW), seq(SLOT_W), seq(SLOT_W), seq(SLOT_W), seq(SLOT_W),
                  cmp_spec(LANES), cmp_spec(VSLOT), qrow(LANES), _const_spec(gx.shape)],
        out_specs=qrow(NSA_QW),
        out_shape=jax.ShapeDtypeStruct((T, NSA_QW), BF16),
        scratch_shapes=[pltpu.VMEM((NSA_KV, S // CK, R, CK), F32),
                        per_head(LANES, BF16), per_head(LANES, BF16),
                        per_head(VSLOT, F32), per_head(VSLOT, F32),
                        per_head(LANES, F32),
                        per_head(LANES, F32),
                        pltpu.VMEM((TQ, 3 * NSA_QW), F32)],
        compiler_params=_params(("parallel", "parallel")),
        name="nsa",
    )(q, ks, vs, kw, vw, kc, vc, gt, gx)


def _merge_kernel(x_ref, a_ref, b_ref, wga_ref, wgb_ref, wua_ref, wub_ref, wo_ref, g_ref, bb_ref, o_ref):
    x = x_ref[...]
    xb = x.astype(BF16)
    m = (_sigmoid(_dot(xb, wga_ref[...])) * _dot(a_ref[...], wua_ref[...])
         + _sigmoid(_dot(xb, wgb_ref[...])) * _dot(b_ref[...], wub_ref[...]))
    mix = _dot(m.astype(BF16), wo_ref[...])
    o_ref[...] = _layer_norm(DN_ALPHA * x + mix, g_ref[...], bb_ref[...])


def _merge(x2, a, b, wga, wgb, wua, wub, wo, g, bb, tm):
    T = x2.shape[0]
    row = pl.BlockSpec((tm, D_MODEL), lambda i: (i, 0))
    wsp = _const_spec((D_MODEL, D_MODEL))
    vec = _const_spec((1, D_MODEL))
    return pl.pallas_call(
        _merge_kernel,
        grid=(T // tm,),
        in_specs=[row, row, row, wsp, wsp, wsp, wsp, wsp, vec, vec],
        out_specs=row,
        out_shape=jax.ShapeDtypeStruct((T, D_MODEL), F32),
        compiler_params=_params(("parallel",)),
        name="merge",
    )(x2, a, b, wga, wgb, wua, wub, wo, g, bb)


def _ffn_kernel(h_ref, wg_ref, wu_ref, wd_ref, g_ref, b_ref, o_ref):
    h = h_ref[...]
    hb = h.astype(BF16)
    gate = _dot(hb, wg_ref[...])
    act = (gate * _sigmoid(gate) * _dot(hb, wu_ref[...])).astype(BF16)
    ffn = _dot(act, wd_ref[...])
    o_ref[...] = _layer_norm(DN_ALPHA * h + ffn, g_ref[...], b_ref[...])


def _ffn(h, wg, wu, wd, g, b, tm):
    T = h.shape[0]
    d_ff = wg.shape[1]
    row = pl.BlockSpec((tm, D_MODEL), lambda i: (i, 0))
    vec = _const_spec((1, D_MODEL))
    once = pl.Buffered(1)
    return pl.pallas_call(
        _ffn_kernel,
        grid=(T // tm,),
        in_specs=[row,
                  pl.BlockSpec((D_MODEL, d_ff), lambda i: (0, 0), pipeline_mode=once),
                  pl.BlockSpec((D_MODEL, d_ff), lambda i: (0, 0), pipeline_mode=once),
                  pl.BlockSpec((d_ff, D_MODEL), lambda i: (0, 0), pipeline_mode=once),
                  vec, vec],
        out_specs=row,
        out_shape=jax.ShapeDtypeStruct((T, D_MODEL), F32),
        compiler_params=_params(("parallel",)),
        name="ffn",
    )(h, wg, wu, wd, g, b)


def _pad_slots(w):
    d = w.shape[0]
    w = w.reshape(d, NSA_KV, NSA_D)
    return jnp.pad(w, ((0, 0), (0, 0), (0, LANES - NSA_D))).reshape(d, SLOT_W)


def _cmp_value_table(n_half, n_cmp, n_blk):
    cs = np.arange(n_cmp)[:, None] * CMP_STRIDE
    bs = np.arange(n_blk)[None, :] * SEL_BLOCK
    ov = np.minimum(cs + CMP_LEN, bs + SEL_BLOCK) - np.maximum(cs, bs)
    out = np.zeros((n_half, VSLOT), np.float32)
    out[:, LANES:] = 1.0
    out[:n_cmp, _L_OVL:_L_OVL + n_blk] = np.clip(ov, 0, None) / CMP_LEN
    return out


def _gate_expand_table():
    out = np.zeros((LANES, 3 * NSA_QW), np.float32)
    for br in range(3):
        for h in range(NSA_HEADS):
            c = br * NSA_QW + h * NSA_D
            out[br * NSA_HEADS + h, c:c + NSA_D] = 1.0
            out[LANES // 2 + br * NSA_HEADS + h, c:c + NSA_D] = 1.0
    return out


def kernel(x, positions, w_in, hg_lb_logits, hg_norm_g, cmp_k_pos, cmp_k_w1, cmp_k_b1, cmp_k_w2, cmp_k_b2, cmp_v_pos, cmp_v_w1, cmp_v_b1, cmp_v_w2, cmp_v_b2, w_up_hg, w_up_nsa, w_o, ln1_g, ln1_b, w_ffn_gate, w_ffn_up, w_ffn_down, ln2_g, ln2_b):
    B, S, _ = x.shape
    T = B * S
    tm = min(512, S)
    TQ = 128
    CK = min(256, S)
    n_blk = S // SEL_BLOCK
    n_cmp = (S - CMP_LEN) // CMP_STRIDE + 1
    n_half = S // CMP_STRIDE

    lb_table = jnp.cumsum(jax.nn.softmax(hg_lb_logits.astype(F32), axis=0), axis=0)
    x2 = x.reshape(T, D_MODEL)

    w = w_in[0]
    o = 0
    o += 4 * HG_W
    wq = w[:, o:o + NSA_QW]; o += NSA_QW
    kv = []
    for _ in range(6):
        kv.append(w[:, o:o + NSA_KVW]); o += NSA_KVW
    w_kc, w_vc, w_ks, w_vs, w_kw, w_vw = kv
    w_gt = jnp.pad(w[:, o:o + 3 * NSA_HEADS], ((0, 0), (0, LANES - 3 * NSA_HEADS))); o += 3 * NSA_HEADS
    w_ga = w[:, o:o + D_MODEL]; o += D_MODEL
    w_gb = w[:, o:o + D_MODEL]
    w_nsa = jnp.concatenate([wq, w_kc, _pad_slots(w_ks), _pad_slots(w_kw),
                             w_vc, _pad_slots(w_vs), _pad_slots(w_vw), w_gt], axis=1).astype(BF16)

    half = NSA_D // 2
    inv_freq = ROPE_THETA ** (-jnp.arange(half, dtype=F32) / half)
    ang = positions.astype(F32).reshape(T, 1) * inv_freq[None, :]
    cos_t = jnp.tile(jnp.cos(ang), (1, LANES // half))
    sin_h = jnp.sin(ang)
    sin_t = jnp.tile(jnp.concatenate([-sin_h, sin_h], axis=1), (1, LANES // NSA_D))

    e_np = np.zeros((S, NSA_KV, LANES), np.float32)
    e_np[np.arange(S), :, NSA_D + np.arange(S) // SEL_BLOCK] = 1.0
    e_tab = jnp.asarray(e_np.reshape(S, SLOT_W), BF16)

    hq, hf, hi, hg = _hg_proj(x2, w.astype(BF16), tm)
    a = _hgrn(hq, hf, hi, hg, lb_table[0:1], hg_norm_g[0:1].astype(F32), B, S)

    q, kc_tok, ks, kw, vc_tok, vs, vw, gt = _nsa_proj(x2, w_nsa, cos_t, sin_t, e_tab, tm, S)

    def slot_rows(m):
        m = m.reshape(CMP_LEN, NSA_D, -1)
        return jnp.pad(m, ((0, 0), (0, LANES - NSA_D), (0, 0))).reshape(CMP_LEN * LANES, -1)

    def pos_rows(p):
        flat = slot_rows(p.reshape(CMP_LEN * NSA_D, 1)).reshape(1, CMP_LEN * LANES)
        return jnp.broadcast_to(flat, (8, CMP_LEN * LANES)).astype(BF16)

    def pad_out(w2, b2, width):
        return (jnp.pad(w2, ((0, 0), (0, width - NSA_D))).astype(BF16),
                jnp.pad(b2, (0, width - NSA_D)).reshape(1, width).astype(F32))

    w2k, b2k = pad_out(cmp_k_w2[0], cmp_k_b2[0], LANES)
    w2v, b2v = pad_out(cmp_v_w2[0], cmp_v_b2[0], VSLOT)
    kc, vc = _compress(
        kc_tok, vc_tok,
        pos_rows(cmp_k_pos[0]), slot_rows(cmp_k_w1[0]).astype(BF16), cmp_k_b1[0].reshape(1, -1).astype(F32),
        w2k, b2k,
        pos_rows(cmp_v_pos[0]), slot_rows(cmp_v_w1[0]).astype(BF16), cmp_v_b1[0].reshape(1, -1).astype(F32),
        w2v, b2v,
        jnp.asarray(_cmp_value_table(n_half, n_cmp, n_blk)), B, S)

    b_out = _nsa(q, ks, vs, kw, vw, kc, vc, gt, B, S, TQ, CK)

    h1 = _merge(x2, a, b_out, w_ga.astype(BF16), w_gb.astype(BF16),
                w_up_hg[0].astype(BF16), w_up_nsa[0].astype(BF16), w_o[0].astype(BF16),
                ln1_g[0].reshape(1, -1).astype(F32), ln1_b[0].reshape(1, -1).astype(F32), tm)
    out = _ffn(h1, w_ffn_gate[0].astype(BF16), w_ffn_up[0].astype(BF16), w_ffn_down[0].astype(BF16),
               ln2_g[0].reshape(1, -1).astype(F32), ln2_b[0].reshape(1, -1).astype(F32), tm)
    return out.reshape(B, S, D_MODEL)
```

```python
import functools

import numpy as np
import jax
import jax.numpy as jnp
from jax import lax
from jax.experimental import pallas as pl
from jax.experimental.pallas import tpu as pltpu

F32 = jnp.float32
BF16 = jnp.bfloat16

D_MODEL = 1024
HG_HEADS = 8
HG_DK = 128
HG_DV = 128
HG_W = HG_HEADS * HG_DK
HG_CHUNK = 64
NSA_HEADS = 16
NSA_KV = 4
NSA_G = 4
NSA_D = 64
NSA_QW = NSA_HEADS * NSA_D
NSA_KVW = NSA_KV * NSA_D
CMP_LEN = 32
CMP_STRIDE = 16
CMP_HIDDEN = 256
SEL_BLOCK = 64
SEL_TOPN = 8
SEL_FORCE = 1000.0
WINDOW = 512
ROPE_THETA = 10000.0
DEPTH = 1
DN_ALPHA = (2.0 * DEPTH) ** 0.25
LN_EPS = 1e-5
RMS_EPS = 1e-6
LOG2E = 1.4426950408889634
NEG = -1e30

LANES = 128
SLOT_W = NSA_KV * LANES
VSLOT = 2 * LANES
VMEM_LIMIT = 56 * 1024 * 1024


def _dot(a, b):
    return jnp.dot(a, b, preferred_element_type=F32)


def _dot_nt(a, b):
    return lax.dot_general(a, b, (((1,), (1,)), ((), ())), preferred_element_type=F32)


def _dot_tn(a, b):
    return lax.dot_general(a, b, (((0,), (0,)), ((), ())), preferred_element_type=F32)


def _sigmoid(x):
    return 1.0 / (1.0 + jnp.exp(-x))


def _layer_norm(x, g, b):
    mu = jnp.mean(x, -1, keepdims=True)
    xc = x - mu
    var = jnp.mean(xc * xc, -1, keepdims=True)
    return xc * lax.rsqrt(var + LN_EPS) * g + b


def _params(sem):
    return pltpu.CompilerParams(dimension_semantics=sem, vmem_limit_bytes=VMEM_LIMIT)


def _const_spec(shape):
    nd = len(shape)
    return pl.BlockSpec(shape, lambda *_: (0,) * nd)


def _hg_proj_kernel(x_ref, w_ref, q_ref, f_ref, i_ref, g_ref):
    x = x_ref[...].astype(BF16)
    q_ref[...] = _dot(x, w_ref[:, 0:HG_W]).astype(BF16)
    f_ref[...] = _dot(x, w_ref[:, HG_W:2 * HG_W])
    i_ref[...] = _dot(x, w_ref[:, 2 * HG_W:3 * HG_W]).astype(BF16)
    g = _dot(x, w_ref[:, 3 * HG_W:4 * HG_W])
    g_ref[...] = (g * _sigmoid(g)).astype(BF16)


def _hg_proj(x2, w_hg, tm):
    T = x2.shape[0]
    row = lambda i: (i, 0)
    out_spec = pl.BlockSpec((tm, HG_W), row)
    return pl.pallas_call(
        _hg_proj_kernel,
        grid=(T // tm,),
        in_specs=[pl.BlockSpec((tm, D_MODEL), row), _const_spec((D_MODEL, 4 * HG_W))],
        out_specs=[out_spec] * 4,
        out_shape=[jax.ShapeDtypeStruct((T, HG_W), BF16), jax.ShapeDtypeStruct((T, HG_W), F32),
                   jax.ShapeDtypeStruct((T, HG_W), BF16), jax.ShapeDtypeStruct((T, HG_W), BF16)],
        compiler_params=_params(("parallel",)),
        name="hg_proj",
    )(x2, w_hg)


_NQ = NSA_QW
_NKC = NSA_KVW
_NSL = SLOT_W
_ROPE_W = _NQ + _NKC + 2 * _NSL
_O_KC = _NQ
_O_KS = _NQ + _NKC
_O_KW = _O_KS + _NSL
_O_VC = _ROPE_W
_O_VS = _O_VC + _NKC
_O_VW = _O_VS + _NSL
_O_GT = _O_VW + _NSL
_NSA_W = _O_GT + LANES


def _nsa_proj_kernel(x_ref, w_ref, cos_ref, sin_ref, e_ref,
                     q_ref, kc_ref, ks_ref, kw_ref, vc_ref, vs_ref, vw_ref, gt_ref):
    x = x_ref[...].astype(BF16)
    cos = cos_ref[...]
    sin = sin_ref[...]
    half = NSA_D // 2

    def rope(y):
        w = y.shape[1]
        reps = w // LANES
        lane = lax.broadcasted_iota(jnp.int32, y.shape, 1)
        fwd = pltpu.roll(y, w - half, 1)
        bwd = pltpu.roll(y, half, 1)
        rot = jnp.where((lane % NSA_D) < half, fwd, bwd)
        return y * jnp.tile(cos, (1, reps)) + rot * jnp.tile(sin, (1, reps))

    yq = rope(_dot(x, w_ref[:, 0:_NQ]))
    q_ref[...] = (yq * (NSA_D ** -0.5 * LOG2E)).astype(BF16)

    def per_head(y, o_ref):
        left = lax.broadcasted_iota(jnp.int32, (y.shape[0], LANES), 1) < NSA_D
        for c in range(y.shape[1] // LANES):
            two = y[:, c * LANES:(c + 1) * LANES]
            o_ref[2 * c] = jnp.where(left, two, 0.0)
            o_ref[2 * c + 1] = jnp.where(left, pltpu.roll(two, NSA_D, 1), 0.0)

    per_head(rope(_dot(x, w_ref[:, _O_KC:_O_KC + _NKC])), kc_ref)
    ks_ref[...] = rope(_dot(x, w_ref[:, _O_KS:_O_KS + _NSL])).astype(BF16) + e_ref[...]
    kw_ref[...] = rope(_dot(x, w_ref[:, _O_KW:_O_KW + _NSL])).astype(BF16)
    per_head(_dot(x, w_ref[:, _O_VC:_O_VC + _NKC]), vc_ref)
    vs_ref[...] = _dot(x, w_ref[:, _O_VS:_O_VS + _NSL]).astype(BF16)
    vw_ref[...] = _dot(x, w_ref[:, _O_VW:_O_VW + _NSL]).astype(BF16)
    gates = _sigmoid(_dot(x, w_ref[:, _O_GT:_O_GT + LANES]))
    g_hi = gates.astype(BF16).astype(F32)
    lane = lax.broadcasted_iota(jnp.int32, gates.shape, 1)
    gt_ref[...] = jnp.where(lane < LANES // 2, g_hi, pltpu.roll(gates - g_hi, LANES // 2, 1)).astype(BF16)


def _nsa_proj(x2, w_nsa, cos_t, sin_t, e_tab, tm, S):
    T = x2.shape[0]
    row = lambda i: (i, 0)
    tiles_per_seq = S // tm
    tok = lambda w: (pl.BlockSpec((tm, w), row), jax.ShapeDtypeStruct((T, w), BF16))
    heads = (pl.BlockSpec((NSA_KV, tm, LANES), lambda i: (0, i, 0)),
             jax.ShapeDtypeStruct((NSA_KV, T, LANES), F32))
    outs = [tok(_NQ), heads, tok(_NSL), tok(_NSL), heads, tok(_NSL), tok(_NSL), tok(LANES)]
    return pl.pallas_call(
        _nsa_proj_kernel,
        grid=(T // tm,),
        in_specs=[pl.BlockSpec((tm, D_MODEL), row), _const_spec((D_MODEL, _NSA_W)),
                  pl.BlockSpec((tm, LANES), row), pl.BlockSpec((tm, LANES), row),
                  pl.BlockSpec((tm, _NSL), lambda i: (i % tiles_per_seq, 0))],
        out_specs=[spec for spec, _ in outs],
        out_shape=[shape for _, shape in outs],
        compiler_params=_params(("parallel",)),
        name="nsa_proj",
    )(x2, w_nsa, cos_t, sin_t, e_tab)


def _hgrn_kernel(q_ref, f_ref, i_ref, g_ref, lb_ref, ng_ref, o_ref, *, n_chunks, heads, unroll):
    C = HG_CHUNK
    ng = ng_ref[...]
    r = lax.broadcasted_iota(jnp.int32, (C, C), 0)
    c = lax.broadcasted_iota(jnp.int32, (C, C), 1)
    tril = r >= c
    row = lax.broadcasted_iota(jnp.int32, (C, HG_DK), 0)

    def chunk_cumsum(x):
        d = 1
        while d < C:
            x = x + jnp.where(row >= d, pltpu.roll(x, d, 0), 0.0)
            d *= 2
        return x

    def body(cj, states):
        states = list(states)
        items = [(u, h) for u in range(unroll) for h in range(heads)]
        rows = {it: pl.ds(pl.multiple_of((cj * unroll + it[0]) * C, C), C) for it in items}
        cols = {it: slice(it[1] * HG_DK, (it[1] + 1) * HG_DK) for it in items}
        f, eb, q_dec, k_inv, k_end, attn, out = {}, {}, {}, {}, {}, {}, {}
        for it in items:
            lb = lb_ref[:, cols[it]]
            f[it] = lb + (1.0 - lb) * _sigmoid(f_ref[rows[it], cols[it]])
        for it in items:
            eb[it] = jnp.exp(chunk_cumsum(jnp.log(f[it])))
        for it in items:
            q_dec[it] = (q_ref[rows[it], cols[it]].astype(F32) * eb[it]).astype(BF16)
            k_inv_f = (1.0 - f[it]) * (1.0 / eb[it])
            k_inv[it] = k_inv_f.astype(BF16)
            k_end[it] = (k_inv_f * eb[it][C - 1:C, :]).astype(BF16)
        for it in items:
            attn[it] = jnp.where(tril, _dot_nt(q_dec[it], k_inv[it]), 0.0).astype(BF16)
        for it in items:
            h = it[1]
            v = i_ref[rows[it], cols[it]]
            out[it] = _dot(attn[it], v) + _dot_nt(q_dec[it], states[h].astype(BF16))
            states[h] = states[h] * eb[it][C - 1:C, :] + _dot_tn(v, k_end[it])
        for it in items:
            o = out[it]
            on = o * lax.rsqrt(jnp.mean(o * o, -1, keepdims=True) + RMS_EPS) * ng
            o_ref[rows[it], cols[it]] = (on * g_ref[rows[it], cols[it]].astype(F32)).astype(BF16)
        return tuple(states)

    init = tuple(jnp.zeros((HG_DV, HG_DK), F32) for _ in range(heads))
    lax.fori_loop(0, n_chunks // unroll, body, init)


def _hgrn(q, f, i, g, lb, ng, B, S, heads=8, unroll=4):
    T = B * S
    w = heads * HG_DK
    blk = pl.BlockSpec((S, w), lambda b, h: (b, h))
    return pl.pallas_call(
        functools.partial(_hgrn_kernel, n_chunks=S // HG_CHUNK, heads=heads, unroll=unroll),
        grid=(B, HG_HEADS // heads),
        in_specs=[blk, blk, blk, blk,
                  pl.BlockSpec((1, w), lambda b, h: (0, h)),
                  pl.BlockSpec((1, HG_DV), lambda b, h: (0, 0))],
        out_specs=blk,
        out_shape=jax.ShapeDtypeStruct((T, HG_W), BF16),
        compiler_params=_params(("parallel", "parallel")),
        name="hgrn",
    )(q, f, i, g, lb, ng)


def _compress_kernel(tk_ref, tv_ref, pk_ref, w1k_ref, b1k_ref, w2k_ref, b2k_ref,
                     pv_ref, w1v_ref, b1v_ref, w2v_ref, b2v_ref, tab_ref, ko_ref, vo_ref, u_ref, *, n_half):
    half = CMP_STRIDE * LANES

    def mlp(t_ref, pos_ref, w1_ref, b1_ref, w2_ref, b2_ref, o_ref, tab=None):
        w_top = w1_ref[0:half, :]
        w_bot = w1_ref[half:2 * half, :]
        base = _dot(pos_ref[...], w1_ref[...])[0:1, :] + b1_ref[...]
        for h in range(NSA_KV):
            for l in range(CMP_STRIDE):
                u_ref[h * n_half:(h + 1) * n_half, l * LANES:(l + 1) * LANES] = (
                    t_ref[h, pl.ds(l, n_half, stride=CMP_STRIDE), :].astype(BF16))
        u = u_ref[...]
        top = _dot(u, w_top)
        bot = _dot(u, w_bot)
        hids = []
        for h in range(NSA_KV):
            r0 = h * n_half
            bot_next = jnp.concatenate([bot[r0 + 1:r0 + n_half, :], jnp.zeros((1, CMP_HIDDEN), F32)], axis=0)
            hids.append(top[r0:r0 + n_half, :] + bot_next + base)
        hid = jnp.concatenate(hids, axis=0)
        hid = hid * _sigmoid(hid)
        out = _dot(hid.astype(BF16), w2_ref[...]) + b2_ref[...]
        for h in range(NSA_KV):
            out_h = out[h * n_half:(h + 1) * n_half, :]
            o_ref[0, h] = (out_h if tab is None else out_h + tab).astype(BF16)

    mlp(tk_ref, pk_ref, w1k_ref, b1k_ref, w2k_ref, b2k_ref, ko_ref)
    mlp(tv_ref, pv_ref, w1v_ref, b1v_ref, w2v_ref, b2v_ref, vo_ref, tab_ref[...])


def _compress(tk, tv, pk, w1k, b1k, w2k, b2k, pv, w1v, b1v, w2v, b2v, tab, B, S):
    nh = S // CMP_STRIDE
    t_spec = pl.BlockSpec((NSA_KV, S, LANES), lambda b: (0, b, 0))
    o_spec = lambda w: pl.BlockSpec((1, NSA_KV, nh, w), lambda b: (b, 0, 0, 0))
    specs = lambda arrs: [_const_spec(a.shape) for a in arrs]
    return pl.pallas_call(
        functools.partial(_compress_kernel, n_half=nh),
        grid=(B,),
        in_specs=([t_spec, t_spec] + specs((pk, w1k, b1k, w2k, b2k)) + specs((pv, w1v, b1v, w2v, b2v))
                  + [_const_spec(tab.shape)]),
        out_specs=[o_spec(LANES), o_spec(VSLOT)],
        out_shape=[jax.ShapeDtypeStruct((B, NSA_KV, nh, LANES), BF16),
                   jax.ShapeDtypeStruct((B, NSA_KV, nh, VSLOT), BF16)],
        scratch_shapes=[pltpu.VMEM((NSA_KV * nh, CMP_STRIDE * LANES), BF16)],
        compiler_params=_params(("parallel",)),
        name="compress",
    )(tk, tv, pk, w1k, b1k, w2k, b2k, pv, w1v, b1v, w2v, b2v, tab)


_L_OVL = LANES - 32


def _nsa_kernel(q_ref, ks_ref, vs_ref, kw_ref, vw_ref, kc_ref, vc_ref, gt_ref, gx_ref,
                o_ref, sbuf, qpl_ref, qaug_ref, acc_ref, accw_ref, mt_ref, cmp_ref, gexp_ref,
                *, TQ, CK, NW, n_cmp, n_blk, top_n):
    R = NSA_G * TQ
    NH = CK // LANES
    q0 = pl.program_id(1) * TQ
    lane_q = lax.broadcasted_iota(jnp.int32, (TQ, LANES), 1)
    t_q = q0 + lax.broadcasted_iota(jnp.int32, (TQ, 1), 0)

    def add_bias(s, bias):
        return jnp.concatenate([s[g * TQ:(g + 1) * TQ] + bias for g in range(NSA_G)], axis=0)

    def fold_max(s):
        return functools.reduce(jnp.maximum, [s[:, u * LANES:(u + 1) * LANES] for u in range(NH)])

    def lanes_ck(m):
        return jnp.concatenate([m] * NH, axis=1)

    bias_c = jnp.where((lane_q * CMP_STRIDE + (CMP_LEN - 1) <= t_q) & (lane_q < n_cmp), 0.0, NEG)
    key_l = lax.broadcasted_iota(jnp.int32, (TQ, CK), 1)
    n_sel = (q0 + TQ + CK - 1) // CK
    last = n_sel - 1
    bias_diag = jnp.where(last * CK + key_l <= t_q, 0.0, NEG)

    j_blk = lax.broadcasted_iota(jnp.int32, (n_blk, TQ), 0)
    cur = (q0 + lax.broadcasted_iota(jnp.int32, (n_blk, TQ), 1)) // SEL_BLOCK
    blk_ok = j_blk <= cur
    forced = (j_blk == 0) | (j_blk == cur) | (j_blk == cur - 1)

    gexp_ref[...] = _dot(gt_ref[...], gx_ref[...])

    def q_slabs(kvh):
        slabs = []
        for pair in range(NSA_G // 2):
            col = (kvh * NSA_G + 2 * pair) * NSA_D
            x = q_ref[:, col:col + LANES].astype(F32)
            slabs += [x, pltpu.roll(x, NSA_D, 1)]
        return slabs

    for kvh in range(NSA_KV):
        qpl_ref[kvh] = jnp.concatenate(
            [jnp.where(lane_q < NSA_D, x, 0.0).astype(BF16) for x in q_slabs(kvh)], axis=0)

    KV = range(NSA_KV)
    s_c = [add_bias(_dot_nt(qpl_ref[h], kc_ref[0, h]), bias_c) for h in KV]
    m_c = [jnp.maximum(jnp.max(s, -1, keepdims=True), 0.1 * NEG) for s in s_c]
    acc_c = [_dot(jnp.exp2(s_c[h] - m_c[h]).astype(BF16), vc_ref[0, h]) for h in KV]
    p_n = [a[:, 0:LANES] * (1.0 / jnp.maximum(a[:, LANES:VSLOT], 1e-30)) for a in acc_c]
    for h in KV:
        cmp_ref[h] = p_n[h]
    imp = [functools.reduce(jnp.add, [p[g * TQ:(g + 1) * TQ] for g in range(NSA_G)]).T[_L_OVL:_L_OVL + n_blk, :]
           for p in p_n]
    score = [jnp.where(blk_ok, x + jnp.where(forced, SEL_FORCE, 0.0), -1.0) for x in imp]
    rank = [jnp.zeros((n_blk, TQ), jnp.int32) for _ in KV]
    for i in range(n_blk):
        for h in KV:
            ri = score[h][i:i + 1, :]
            beats = (ri > score[h]) | ((ri == score[h]) & (j_blk > i))
            rank[h] = rank[h] + beats.astype(jnp.int32)
    for h in KV:
        bias_t = jnp.where((rank[h] < top_n) & blk_ok, 0.0, NEG)
        bias_q = jnp.concatenate([jnp.zeros((NSA_D, TQ), F32), bias_t,
                                  jnp.zeros((LANES - NSA_D - n_blk, TQ), F32)], axis=0).T
        qaug_ref[h] = jnp.concatenate(
            [jnp.where(lane_q < NSA_D, x, bias_q).astype(BF16) for x in q_slabs(h)], axis=0)

    def chunk(ref, start, kvh, slot=LANES):
        return ref[pl.ds(start, CK), kvh * slot:(kvh + 1) * slot]

    def score_pass(qs_ref, k_ref, ci, slot, bias, m_ref):
        start = pl.multiple_of(ci * CK, CK)
        for kvh in range(NSA_KV):
            s = _dot_nt(qs_ref[kvh], chunk(k_ref, start, kvh))
            if bias is not None:
                s = add_bias(s, bias)
            sbuf[kvh, slot] = s
            m_ref[kvh] = jnp.maximum(m_ref[kvh], fold_max(s))

    def value_pass(v_ref, ci, slot, m_ref, out_ref):
        start = pl.multiple_of(ci * CK, CK)
        for kvh in range(NSA_KV):
            p = jnp.exp2(sbuf[kvh, slot] - lanes_ck(m_ref[kvh]))
            v1 = jnp.concatenate([chunk(v_ref, start, kvh), jnp.ones((CK, LANES), BF16)], axis=1)
            out_ref[kvh] += _dot(p.astype(BF16), v1)

    def begin_max(m_ref):
        for kvh in range(NSA_KV):
            m_ref[kvh] = jnp.full((R, LANES), -3e38, F32)

    def finish_max(m_ref, out_ref):
        for kvh in range(NSA_KV):
            m_ref[kvh] = jnp.broadcast_to(jnp.max(m_ref[kvh], -1, keepdims=True), (R, LANES))
            out_ref[kvh] = jnp.zeros((R, VSLOT), F32)

    def loop(n, body):
        lax.fori_loop(0, n, lambda i, c: (body(i), c)[1], 0)

    def win_chunk(k):
        cw = last - (NW - 1) + k
        return cw, jnp.maximum(cw, 0)

    def win_scores(k):
        cw, cidx = win_chunk(k)
        key = cidx * CK + key_l
        ok = (key <= t_q) & (key > t_q - WINDOW) & (cw >= 0)
        score_pass(qpl_ref, kw_ref, cidx, k, jnp.where(ok, 0.0, NEG), mt_ref)

    sel_scores = lambda ci, bias=None: score_pass(qaug_ref, ks_ref, ci, ci, bias, mt_ref)
    sel_values = lambda ci: value_pass(vs_ref, ci, ci, mt_ref, acc_ref)

    def chunks(n, one):
        loop(n // 4, lambda j: [one(4 * j + u) for u in range(4)])
        loop((n % 4) // 2, lambda j: [one(n - (n % 4) + u) for u in range(2)])
        loop(n % 2, lambda j: one(n - 1))

    begin_max(mt_ref)
    chunks(last, sel_scores)
    sel_scores(last, bias_diag)
    finish_max(mt_ref, acc_ref)
    chunks(n_sel, sel_values)

    begin_max(mt_ref)
    for k in range(NW):
        win_scores(k)
    finish_max(mt_ref, accw_ref)
    for k in range(NW):
        value_pass(vw_ref, win_chunk(k)[1], k, mt_ref, accw_ref)

    left = lane_q < NSA_D
    for kvh in range(NSA_KV):
        for pair in range(NSA_G // 2):
            col = (kvh * NSA_G + 2 * pair) * NSA_D
            r_e = pl.ds(2 * pair * TQ, TQ)
            r_o = pl.ds((2 * pair + 1) * TQ, TQ)

            def packed(ref, lanes):
                return jnp.where(left, ref[kvh, r_e, lanes], pltpu.roll(ref[kvh, r_o, lanes], NSA_D, 1))

            out = packed(cmp_ref, slice(0, LANES)) * gexp_ref[:, col:col + LANES]
            for br, ref in ((1, acc_ref), (2, accw_ref)):
                den = jnp.where(left, ref[kvh, r_e, LANES:VSLOT], ref[kvh, r_o, LANES:VSLOT])
                gate = gexp_ref[:, br * NSA_QW + col:br * NSA_QW + col + LANES]
                out = out + packed(ref, slice(0, LANES)) * (gate / den)
            o_ref[:, col:col + LANES] = out.astype(BF16)


def _window_chunks(S, TQ, CK):
    return max((q0 + TQ - 1) // CK - max(q0 - WINDOW + 1, 0) // CK + 1 for q0 in range(0, S, TQ))


def _nsa(q, ks, vs, kw, vw, kc, vc, gt, B, S, TQ, CK):
    T = B * S
    n_blk = S // SEL_BLOCK
    n_cmp = (S - CMP_LEN) // CMP_STRIDE + 1
    R = NSA_G * TQ
    NW = _window_chunks(S, TQ, CK)
    assert kc.shape[2] == LANES and n_blk <= LANES - _L_OVL and S // CK >= NW
    seq = lambda w: pl.BlockSpec((S, w), lambda b, i: (b, 0))
    qrow = lambda w: pl.BlockSpec((TQ, w), lambda b, i: (b * (S // TQ) + i, 0))
    cmp_spec = lambda w: pl.BlockSpec((1, NSA_KV, LANES, w), lambda b, i: (b, 0, 0, 0))
    kern = functools.partial(_nsa_kernel, TQ=TQ, CK=CK, NW=NW, n_cmp=n_cmp, n_blk=n_blk,
                             top_n=min(SEL_TOPN, n_blk))
    per_head = lambda w, dt: pltpu.VMEM((NSA_KV, R, w), dt)
    gx = jnp.asarray(_gate_expand_table(), BF16)
    return pl.pallas_call(
        kern,
        grid=(B, S // TQ),
        in_specs=[qrow(NSA_QW), seq(SLOT_W), seq(SLOT_W), seq(SLOT_W), seq(SLOT_W),
                  cmp_spec(LANES), cmp_spec(VSLOT), qrow(LANES), _const_spec(gx.shape)],
        out_specs=qrow(NSA_QW),
        out_shape=jax.ShapeDtypeStruct((T, NSA_QW), BF16),
        scratch_shapes=[pltpu.VMEM((NSA_KV, S // CK, R, CK), F32),
                        per_head(LANES, BF16), per_head(LANES, BF16),
                        per_head(VSLOT, F32), per_head(VSLOT, F32),
                        per_head(LANES, F32),
                        per_head(LANES, F32),
                        pltpu.VMEM((TQ, 3 * NSA_QW), F32)],
        compiler_params=_params(("parallel", "parallel")),
        name="nsa",
    )(q, ks, vs, kw, vw, kc, vc, gt, gx)


def _merge_kernel(x_ref, a_ref, b_ref, wga_ref, wgb_ref, wua_ref, wub_ref, wo_ref, g_ref, bb_ref, o_ref):
    x = x_ref[...]
    xb = x.astype(BF16)
    m = (_sigmoid(_dot(xb, wga_ref[...])) * _dot(a_ref[...], wua_ref[...])
         + _sigmoid(_dot(xb, wgb_ref[...])) * _dot(b_ref[...], wub_ref[...]))
    mix = _dot(m.astype(BF16), wo_ref[...])
    o_ref[...] = _layer_norm(DN_ALPHA * x + mix, g_ref[...], bb_ref[...])


def _merge(x2, a, b, wga, wgb, wua, wub, wo, g, bb, tm):
    T = x2.shape[0]
    row = pl.BlockSpec((tm, D_MODEL), lambda i: (i, 0))
    wsp = _const_spec((D_MODEL, D_MODEL))
    vec = _const_spec((1, D_MODEL))
    return pl.pallas_call(
        _merge_kernel,
        grid=(T // tm,),
        in_specs=[row, row, row, wsp, wsp, wsp, wsp, wsp, vec, vec],
        out_specs=row,
        out_shape=jax.ShapeDtypeStruct((T, D_MODEL), F32),
        compiler_params=_params(("parallel",)),
        name="merge",
    )(x2, a, b, wga, wgb, wua, wub, wo, g, bb)


def _ffn_kernel(h_ref, wg_ref, wu_ref, wd_ref, g_ref, b_ref, o_ref):
    h = h_ref[...]
    hb = h.astype(BF16)
    gate = _dot(hb, wg_ref[...])
    act = (gate * _sigmoid(gate) * _dot(hb, wu_ref[...])).astype(BF16)
    ffn = _dot(act, wd_ref[...])
    o_ref[...] = _layer_norm(DN_ALPHA * h + ffn, g_ref[...], b_ref[...])


def _ffn(h, wg, wu, wd, g, b, tm):
    T = h.shape[0]
    d_ff = wg.shape[1]
    row = pl.BlockSpec((tm, D_MODEL), lambda i: (i, 0))
    vec = _const_spec((1, D_MODEL))
    once = pl.Buffered(1)
    return pl.pallas_call(
        _ffn_kernel,
        grid=(T // tm,),
        in_specs=[row,
                  pl.BlockSpec((D_MODEL, d_ff), lambda i: (0, 0), pipeline_mode=once),
                  pl.BlockSpec((D_MODEL, d_ff), lambda i: (0, 0), pipeline_mode=once),
                  pl.BlockSpec((d_ff, D_MODEL), lambda i: (0, 0), pipeline_mode=once),
                  vec, vec],
        out_specs=row,
        out_shape=jax.ShapeDtypeStruct((T, D_MODEL), F32),
        compiler_params=_params(("parallel",)),
        name="ffn",
    )(h, wg, wu, wd, g, b)


def _pad_slots(w):
    d = w.shape[0]
    w = w.reshape(d, NSA_KV, NSA_D)
    return jnp.pad(w, ((0, 0), (0, 0), (0, LANES - NSA_D))).reshape(d, SLOT_W)


def _cmp_value_table(n_half, n_cmp, n_blk):
    cs = np.arange(n_cmp)[:, None] * CMP_STRIDE
    bs = np.arange(n_blk)[None, :] * SEL_BLOCK
    ov = np.minimum(cs + CMP_LEN, bs + SEL_BLOCK) - np.maximum(cs, bs)
    out = np.zeros((n_half, VSLOT), np.float32)
    out[:, LANES:] = 1.0
    out[:n_cmp, _L_OVL:_L_OVL + n_blk] = np.clip(ov, 0, None) / CMP_LEN
    return out


def _gate_expand_table():
    out = np.zeros((LANES, 3 * NSA_QW), np.float32)
    for br in range(3):
        for h in range(NSA_HEADS):
            c = br * NSA_QW + h * NSA_D
            out[br * NSA_HEADS + h, c:c + NSA_D] = 1.0
            out[LANES // 2 + br * NSA_HEADS + h, c:c + NSA_D] = 1.0
    return out


def kernel(x, positions, w_in, hg_lb_logits, hg_norm_g, cmp_k_pos, cmp_k_w1, cmp_k_b1, cmp_k_w2, cmp_k_b2, cmp_v_pos, cmp_v_w1, cmp_v_b1, cmp_v_w2, cmp_v_b2, w_up_hg, w_up_nsa, w_o, ln1_g, ln1_b, w_ffn_gate, w_ffn_up, w_ffn_down, ln2_g, ln2_b):
    B, S, _ = x.shape
    T = B * S
    tm = min(512, S)
    TQ = 128
    CK = min(256, S)
    n_blk = S // SEL_BLOCK
    n_cmp = (S - CMP_LEN) // CMP_STRIDE + 1
    n_half = S // CMP_STRIDE

    lb_table = jnp.cumsum(jax.nn.softmax(hg_lb_logits.astype(F32), axis=0), axis=0)
    x2 = x.reshape(T, D_MODEL)

    w = w_in[0]
    o = 0
    o += 4 * HG_W
    wq = w[:, o:o + NSA_QW]; o += NSA_QW
    kv = []
    for _ in range(6):
        kv.append(w[:, o:o + NSA_KVW]); o += NSA_KVW
    w_kc, w_vc, w_ks, w_vs, w_kw, w_vw = kv
    w_gt = jnp.pad(w[:, o:o + 3 * NSA_HEADS], ((0, 0), (0, LANES - 3 * NSA_HEADS))); o += 3 * NSA_HEADS
    w_ga = w[:, o:o + D_MODEL]; o += D_MODEL
    w_gb = w[:, o:o + D_MODEL]
    w_nsa = jnp.concatenate([wq, w_kc, _pad_slots(w_ks), _pad_slots(w_kw),
                             w_vc, _pad_slots(w_vs), _pad_slots(w_vw), w_gt], axis=1).astype(BF16)

    half = NSA_D // 2
    inv_freq = ROPE_THETA ** (-jnp.arange(half, dtype=F32) / half)
    ang = positions.astype(F32).reshape(T, 1) * inv_freq[None, :]
    cos_t = jnp.tile(jnp.cos(ang), (1, LANES // half))
    sin_h = jnp.sin(ang)
    sin_t = jnp.tile(jnp.concatenate([-sin_h, sin_h], axis=1), (1, LANES // NSA_D))

    e_np = np.zeros((S, NSA_KV, LANES), np.float32)
    e_np[np.arange(S), :, NSA_D + np.arange(S) // SEL_BLOCK] = 1.0
    e_tab = jnp.asarray(e_np.reshape(S, SLOT_W), BF16)

    hq, hf, hi, hg = _hg_proj(x2, w.astype(BF16), tm)
    a = _hgrn(hq, hf, hi, hg, lb_table[0:1], hg_norm_g[0:1].astype(F32), B, S)

    q, kc_tok, ks, kw, vc_tok, vs, vw, gt = _nsa_proj(x2, w_nsa, cos_t, sin_t, e_tab, tm, S)

    def slot_rows(m):
        m = m.reshape(CMP_LEN, NSA_D, -1)
        return jnp.pad(m, ((0, 0), (0, LANES - NSA_D), (0, 0))).reshape(CMP_LEN * LANES, -1)

    def pos_rows(p):
        flat = slot_rows(p.reshape(CMP_LEN * NSA_D, 1)).reshape(1, CMP_LEN * LANES)
        return jnp.broadcast_to(flat, (8, CMP_LEN * LANES)).astype(BF16)

    def pad_out(w2, b2, width):
        return (jnp.pad(w2, ((0, 0), (0, width - NSA_D))).astype(BF16),
                jnp.pad(b2, (0, width - NSA_D)).reshape(1, width).astype(F32))

    w2k, b2k = pad_out(cmp_k_w2[0], cmp_k_b2[0], LANES)
    w2v, b2v = pad_out(cmp_v_w2[0], cmp_v_b2[0], VSLOT)
    kc, vc = _compress(
        kc_tok, vc_tok,
        pos_rows(cmp_k_pos[0]), slot_rows(cmp_k_w1[0]).astype(BF16), cmp_k_b1[0].reshape(1, -1).astype(F32),
        w2k, b2k,
        pos_rows(cmp_v_pos[0]), slot_rows(cmp_v_w1[0]).astype(BF16), cmp_v_b1[0].reshape(1, -1).astype(F32),
        w2v, b2v,
        jnp.asarray(_cmp_value_table(n_half, n_cmp, n_blk)), B, S)

    b_out = _nsa(q, ks, vs, kw, vw, kc, vc, gt, B, S, TQ, CK)

    h1 = _merge(x2, a, b_out, w_ga.astype(BF16), w_gb.astype(BF16),
                w_up_hg[0].astype(BF16), w_up_nsa[0].astype(BF16), w_o[0].astype(BF16),
                ln1_g[0].reshape(1, -1).astype(F32), ln1_b[0].reshape(1, -1).astype(F32), tm)
    out = _ffn(h1, w_ffn_gate[0].astype(BF16), w_ffn_up[0].astype(BF16), w_ffn_down[0].astype(BF16),
               ln2_g[0].reshape(1, -1).astype(F32), ln2_b[0].reshape(1, -1).astype(F32), tm)
    return out.reshape(B, S, D_MODEL)
```

```python
import functools

import numpy as np
import jax
import jax.numpy as jnp
from jax import lax
from jax.experimental import pallas as pl
from jax.experimental.pallas import tpu as pltpu

F32 = jnp.float32
BF16 = jnp.bfloat16

D_MODEL = 1024
HG_HEADS = 8
HG_DK = 128
HG_DV = 128
HG_W = HG_HEADS * HG_DK
HG_CHUNK = 64
NSA_HEADS = 16
NSA_KV = 4
NSA_G = 4
NSA_D = 64
NSA_QW = NSA_HEADS * NSA_D
NSA_KVW = NSA_KV * NSA_D
CMP_LEN = 32
CMP_STRIDE = 16
CMP_HIDDEN = 256
SEL_BLOCK = 64
SEL_TOPN = 8
SEL_FORCE = 1000.0
WINDOW = 512
ROPE_THETA = 10000.0
DEPTH = 1
DN_ALPHA = (2.0 * DEPTH) ** 0.25
LN_EPS = 1e-5
RMS_EPS = 1e-6
LOG2E = 1.4426950408889634
NEG = -1e30

LANES = 128
SLOT_W = NSA_KV * LANES
VSLOT = 2 * LANES
VMEM_LIMIT = 56 * 1024 * 1024


def _dot(a, b):
    return jnp.dot(a, b, preferred_element_type=F32)


def _dot_nt(a, b):
    return lax.dot_general(a, b, (((1,), (1,)), ((), ())), preferred_element_type=F32)


def _dot_tn(a, b):
    return lax.dot_general(a, b, (((0,), (0,)), ((), ())), preferred_element_type=F32)


def _sigmoid(x):
    return 1.0 / (1.0 + jnp.exp(-x))


def _layer_norm(x, g, b):
    mu = jnp.mean(x, -1, keepdims=True)
    xc = x - mu
    var = jnp.mean(xc * xc, -1, keepdims=True)
    return xc * lax.rsqrt(var + LN_EPS) * g + b


def _params(sem):
    return pltpu.CompilerParams(dimension_semantics=sem, vmem_limit_bytes=VMEM_LIMIT)


def _const_spec(shape):
    nd = len(shape)
    return pl.BlockSpec(shape, lambda *_: (0,) * nd)


def _hg_proj_kernel(x_ref, w_ref, q_ref, f_ref, i_ref, g_ref):
    x = x_ref[...].astype(BF16)
    q_ref[...] = _dot(x, w_ref[:, 0:HG_W]).astype(BF16)
    f_ref[...] = _dot(x, w_ref[:, HG_W:2 * HG_W])
    i_ref[...] = _dot(x, w_ref[:, 2 * HG_W:3 * HG_W]).astype(BF16)
    g = _dot(x, w_ref[:, 3 * HG_W:4 * HG_W])
    g_ref[...] = (g * _sigmoid(g)).astype(BF16)


def _hg_proj(x2, w_hg, tm):
    T = x2.shape[0]
    row = lambda i: (i, 0)
    out_spec = pl.BlockSpec((tm, HG_W), row)
    return pl.pallas_call(
        _hg_proj_kernel,
        grid=(T // tm,),
        in_specs=[pl.BlockSpec((tm, D_MODEL), row), _const_spec((D_MODEL, 4 * HG_W))],
        out_specs=[out_spec] * 4,
        out_shape=[jax.ShapeDtypeStruct((T, HG_W), BF16), jax.ShapeDtypeStruct((T, HG_W), F32),
                   jax.ShapeDtypeStruct((T, HG_W), BF16), jax.ShapeDtypeStruct((T, HG_W), BF16)],
        compiler_params=_params(("parallel",)),
        name="hg_proj",
    )(x2, w_hg)


_NQ = NSA_QW
_NKC = NSA_KVW
_NSL = SLOT_W
_ROPE_W = _NQ + _NKC + 2 * _NSL
_O_KC = _NQ
_O_KS = _NQ + _NKC
_O_KW = _O_KS + _NSL
_O_VC = _ROPE_W
_O_VS = _O_VC + _NKC
_O_VW = _O_VS + _NSL
_O_GT = _O_VW + _NSL
_NSA_W = _O_GT + LANES


def _nsa_proj_kernel(x_ref, w_ref, cos_ref, sin_ref, e_ref,
                     q_ref, kc_ref, ks_ref, kw_ref, vc_ref, vs_ref, vw_ref, gt_ref):
    x = x_ref[...].astype(BF16)
    cos = cos_ref[...]
    sin = sin_ref[...]
    half = NSA_D // 2

    def rope(y):
        w = y.shape[1]
        reps = w // LANES
        lane = lax.broadcasted_iota(jnp.int32, y.shape, 1)
        fwd = pltpu.roll(y, w - half, 1)
        bwd = pltpu.roll(y, half, 1)
        rot = jnp.where((lane % NSA_D) < half, fwd, bwd)
        return y * jnp.tile(cos, (1, reps)) + rot * jnp.tile(sin, (1, reps))

    yq = rope(_dot(x, w_ref[:, 0:_NQ]))
    q_ref[...] = (yq * (NSA_D ** -0.5 * LOG2E)).astype(BF16)

    def per_head(y, o_ref):
        left = lax.broadcasted_iota(jnp.int32, (y.shape[0], LANES), 1) < NSA_D
        for c in range(y.shape[1] // LANES):
            two = y[:, c * LANES:(c + 1) * LANES]
            o_ref[2 * c] = jnp.where(left, two, 0.0)
            o_ref[2 * c + 1] = jnp.where(left, pltpu.roll(two, NSA_D, 1), 0.0)

    per_head(rope(_dot(x, w_ref[:, _O_KC:_O_KC + _NKC])), kc_ref)
    ks_ref[...] = rope(_dot(x, w_ref[:, _O_KS:_O_KS + _NSL])).astype(BF16) + e_ref[...]
    kw_ref[...] = rope(_dot(x, w_ref[:, _O_KW:_O_KW + _NSL])).astype(BF16)
    per_head(_dot(x, w_ref[:, _O_VC:_O_VC + _NKC]), vc_ref)
    vs_ref[...] = _dot(x, w_ref[:, _O_VS:_O_VS + _NSL]).astype(BF16)
    vw_ref[...] = _dot(x, w_ref[:, _O_VW:_O_VW + _NSL]).astype(BF16)
    gates = _sigmoid(_dot(x, w_ref[:, _O_GT:_O_GT + LANES]))
    g_hi = gates.astype(BF16).astype(F32)
    lane = lax.broadcasted_iota(jnp.int32, gates.shape, 1)
    gt_ref[...] = jnp.where(lane < LANES // 2, g_hi, pltpu.roll(gates - g_hi, LANES // 2, 1)).astype(BF16)


def _nsa_proj(x2, w_nsa, cos_t, sin_t, e_tab, tm, S):
    T = x2.shape[0]
    row = lambda i: (i, 0)
    tiles_per_seq = S // tm
    tok = lambda w: (pl.BlockSpec((tm, w), row), jax.ShapeDtypeStruct((T, w), BF16))
    heads = (pl.BlockSpec((NSA_KV, tm, LANES), lambda i: (0, i, 0)),
             jax.ShapeDtypeStruct((NSA_KV, T, LANES), F32))
    outs = [tok(_NQ), heads, tok(_NSL), tok(_NSL), heads, tok(_NSL), tok(_NSL), tok(LANES)]
    return pl.pallas_call(
        _nsa_proj_kernel,
        grid=(T // tm,),
        in_specs=[pl.BlockSpec((tm, D_MODEL), row), _const_spec((D_MODEL, _NSA_W)),
                  pl.BlockSpec((tm, LANES), row), pl.BlockSpec((tm, LANES), row),
                  pl.BlockSpec((tm, _NSL), lambda i: (i % tiles_per_seq, 0))],
        out_specs=[spec for spec, _ in outs],
        out_shape=[shape for _, shape in outs],
        compiler_params=_params(("parallel",)),
        name="nsa_proj",
    )(x2, w_nsa, cos_t, sin_t, e_tab)


def _hgrn_kernel(q_ref, f_ref, i_ref, g_ref, lb_ref, ng_ref, o_ref, *, n_chunks, heads, unroll):
    C = HG_CHUNK
    ng = ng_ref[...]
    r = lax.broadcasted_iota(jnp.int32, (C, C), 0)
    c = lax.broadcasted_iota(jnp.int32, (C, C), 1)
    tril = r >= c
    tril_b = tril.astype(BF16)

    def chunk_cumsum(x):
        hi = x.astype(BF16)
        lo = (x - hi.astype(F32)).astype(BF16)
        return _dot(tril_b, hi) + _dot(tril_b, lo)

    def body(cj, states):
        states = list(states)
        items = [(u, h) for u in range(unroll) for h in range(heads)]
        rows = {it: pl.ds(pl.multiple_of((cj * unroll + it[0]) * C, C), C) for it in items}
        cols = {it: slice(it[1] * HG_DK, (it[1] + 1) * HG_DK) for it in items}
        f, eb, q_dec, k_inv, k_end, attn, out = {}, {}, {}, {}, {}, {}, {}
        for it in items:
            lb = lb_ref[:, cols[it]]
            f[it] = lb + (1.0 - lb) * _sigmoid(f_ref[rows[it], cols[it]])
        for it in items:
            eb[it] = jnp.exp(chunk_cumsum(jnp.log(f[it])))
        for it in items:
            q_dec[it] = (q_ref[rows[it], cols[it]].astype(F32) * eb[it]).astype(BF16)
            k_inv_f = (1.0 - f[it]) * (1.0 / eb[it])
            k_inv[it] = k_inv_f.astype(BF16)
            k_end[it] = (k_inv_f * eb[it][C - 1:C, :]).astype(BF16)
        for it in items:
            attn[it] = jnp.where(tril, _dot_nt(q_dec[it], k_inv[it]), 0.0).astype(BF16)
        for it in items:
            h = it[1]
            v = i_ref[rows[it], cols[it]]
            out[it] = _dot(attn[it], v) + _dot_nt(q_dec[it], states[h].astype(BF16))
            states[h] = states[h] * eb[it][C - 1:C, :] + _dot_tn(v, k_end[it])
        for it in items:
            o = out[it]
            on = o * lax.rsqrt(jnp.mean(o * o, -1, keepdims=True) + RMS_EPS) * ng
            o_ref[rows[it], cols[it]] = (on * g_ref[rows[it], cols[it]].astype(F32)).astype(BF16)
        return tuple(states)

    init = tuple(jnp.zeros((HG_DV, HG_DK), F32) for _ in range(heads))
    lax.fori_loop(0, n_chunks // unroll, body, init)


def _hgrn(q, f, i, g, lb, ng, B, S, heads=8, unroll=8):
    T = B * S
    w = heads * HG_DK
    blk = pl.BlockSpec((S, w), lambda b, h: (b, h))
    return pl.pallas_call(
        functools.partial(_hgrn_kernel, n_chunks=S // HG_CHUNK, heads=heads, unroll=unroll),
        grid=(B, HG_HEADS // heads),
        in_specs=[blk, blk, blk, blk,
                  pl.BlockSpec((1, w), lambda b, h: (0, h)),
                  pl.BlockSpec((1, HG_DV), lambda b, h: (0, 0))],
        out_specs=blk,
        out_shape=jax.ShapeDtypeStruct((T, HG_W), BF16),
        compiler_params=_params(("parallel", "parallel")),
        name="hgrn",
    )(q, f, i, g, lb, ng)


def _compress_kernel(tk_ref, tv_ref, pk_ref, w1k_ref, b1k_ref, w2k_ref, b2k_ref,
                     pv_ref, w1v_ref, b1v_ref, w2v_ref, b2v_ref, tab_ref, ko_ref, vo_ref, u_ref, *, n_half):
    half = CMP_STRIDE * LANES

    def mlp(t_ref, pos_ref, w1_ref, b1_ref, w2_ref, b2_ref, o_ref, tab=None):
        w_top = w1_ref[0:half, :]
        w_bot = w1_ref[half:2 * half, :]
        base = _dot(pos_ref[...], w1_ref[...])[0:1, :] + b1_ref[...]
        for h in range(NSA_KV):
            for l in range(CMP_STRIDE):
                u_ref[h * n_half:(h + 1) * n_half, l * LANES:(l + 1) * LANES] = (
                    t_ref[h, pl.ds(l, n_half, stride=CMP_STRIDE), :].astype(BF16))
        u = u_ref[...]
        top = _dot(u, w_top)
        bot = _dot(u, w_bot)
        hids = []
        for h in range(NSA_KV):
            r0 = h * n_half
            bot_next = jnp.concatenate([bot[r0 + 1:r0 + n_half, :], jnp.zeros((1, CMP_HIDDEN), F32)], axis=0)
            hids.append(top[r0:r0 + n_half, :] + bot_next + base)
        hid = jnp.concatenate(hids, axis=0)
        hid = hid * _sigmoid(hid)
        out = _dot(hid.astype(BF16), w2_ref[...]) + b2_ref[...]
        for h in range(NSA_KV):
            out_h = out[h * n_half:(h + 1) * n_half, :]
            o_ref[0, h] = (out_h if tab is None else out_h + tab).astype(BF16)

    mlp(tk_ref, pk_ref, w1k_ref, b1k_ref, w2k_ref, b2k_ref, ko_ref)
    mlp(tv_ref, pv_ref, w1v_ref, b1v_ref, w2v_ref, b2v_ref, vo_ref, tab_ref[...])


def _compress(tk, tv, pk, w1k, b1k, w2k, b2k, pv, w1v, b1v, w2v, b2v, tab, B, S):
    nh = S // CMP_STRIDE
    t_spec = pl.BlockSpec((NSA_KV, S, LANES), lambda b: (0, b, 0))
    o_spec = lambda w: pl.BlockSpec((1, NSA_KV, nh, w), lambda b: (b, 0, 0, 0))
    specs = lambda arrs: [_const_spec(a.shape) for a in arrs]
    return pl.pallas_call(
        functools.partial(_compress_kernel, n_half=nh),
        grid=(B,),
        in_specs=([t_spec, t_spec] + specs((pk, w1k, b1k, w2k, b2k)) + specs((pv, w1v, b1v, w2v, b2v))
                  + [_const_spec(tab.shape)]),
        out_specs=[o_spec(LANES), o_spec(VSLOT)],
        out_shape=[jax.ShapeDtypeStruct((B, NSA_KV, nh, LANES), BF16),
                   jax.ShapeDtypeStruct((B, NSA_KV, nh, VSLOT), BF16)],
        scratch_shapes=[pltpu.VMEM((NSA_KV * nh, CMP_STRIDE * LANES), BF16)],
        compiler_params=_params(("parallel",)),
        name="compress",
    )(tk, tv, pk, w1k, b1k, w2k, b2k, pv, w1v, b1v, w2v, b2v, tab)


_L_OVL = LANES - 32


def _nsa_kernel(q_ref, ks_ref, vs_ref, kw_ref, vw_ref, kc_ref, vc_ref, gt_ref, gx_ref,
                o_ref, sbuf, qpl_ref, qaug_ref, acc_ref, accw_ref, mt_ref, cmp_ref, gexp_ref,
                *, TQ, CK, NW, n_cmp, n_blk, top_n):
    R = NSA_G * TQ
    NH = CK // LANES
    q0 = pl.program_id(1) * TQ
    lane_q = lax.broadcasted_iota(jnp.int32, (TQ, LANES), 1)
    t_q = q0 + lax.broadcasted_iota(jnp.int32, (TQ, 1), 0)

    def add_bias(s, bias):
        return jnp.concatenate([s[g * TQ:(g + 1) * TQ] + bias for g in range(NSA_G)], axis=0)

    def fold_max(s):
        return functools.reduce(jnp.maximum, [s[:, u * LANES:(u + 1) * LANES] for u in range(NH)])

    def lanes_ck(m):
        return jnp.concatenate([m] * NH, axis=1)

    bias_c = jnp.where((lane_q * CMP_STRIDE + (CMP_LEN - 1) <= t_q) & (lane_q < n_cmp), 0.0, NEG)
    key_l = lax.broadcasted_iota(jnp.int32, (TQ, CK), 1)
    n_sel = (q0 + TQ + CK - 1) // CK
    last = n_sel - 1
    bias_diag = jnp.where(last * CK + key_l <= t_q, 0.0, NEG)

    j_blk = lax.broadcasted_iota(jnp.int32, (n_blk, TQ), 0)
    cur = (q0 + lax.broadcasted_iota(jnp.int32, (n_blk, TQ), 1)) // SEL_BLOCK
    blk_ok = j_blk <= cur
    forced = (j_blk == 0) | (j_blk == cur) | (j_blk == cur - 1)

    gexp_ref[...] = _dot(gt_ref[...], gx_ref[...])

    def q_slabs(kvh):
        slabs = []
        for pair in range(NSA_G // 2):
            col = (kvh * NSA_G + 2 * pair) * NSA_D
            x = q_ref[:, col:col + LANES].astype(F32)
            slabs += [x, pltpu.roll(x, NSA_D, 1)]
        return slabs

    for kvh in range(NSA_KV):
        qpl_ref[kvh] = jnp.concatenate(
            [jnp.where(lane_q < NSA_D, x, 0.0).astype(BF16) for x in q_slabs(kvh)], axis=0)

    KV = range(NSA_KV)
    s_c = [add_bias(_dot_nt(qpl_ref[h], kc_ref[0, h]), bias_c) for h in KV]
    m_c = [jnp.maximum(jnp.max(s, -1, keepdims=True), 0.1 * NEG) for s in s_c]
    acc_c = [_dot(jnp.exp2(s_c[h] - m_c[h]).astype(BF16), vc_ref[0, h]) for h in KV]
    p_n = [a[:, 0:LANES] * (1.0 / jnp.maximum(a[:, LANES:VSLOT], 1e-30)) for a in acc_c]
    for h in KV:
        cmp_ref[h] = p_n[h]
    imp = [functools.reduce(jnp.add, [p[g * TQ:(g + 1) * TQ] for g in range(NSA_G)]).T[_L_OVL:_L_OVL + n_blk, :]
           for p in p_n]
    score = [jnp.where(blk_ok, x + jnp.where(forced, SEL_FORCE, 0.0), -1.0) for x in imp]
    rank = [jnp.zeros((n_blk, TQ), jnp.int32) for _ in KV]
    for i in range(n_blk):
        for h in KV:
            ri = score[h][i:i + 1, :]
            beats = (ri > score[h]) | ((ri == score[h]) & (j_blk > i))
            rank[h] = rank[h] + beats.astype(jnp.int32)
    for h in KV:
        bias_t = jnp.where((rank[h] < top_n) & blk_ok, 0.0, NEG)
        bias_q = jnp.concatenate([jnp.zeros((NSA_D, TQ), F32), bias_t,
                                  jnp.zeros((LANES - NSA_D - n_blk, TQ), F32)], axis=0).T
        qaug_ref[h] = jnp.concatenate(
            [jnp.where(lane_q < NSA_D, x, bias_q).astype(BF16) for x in q_slabs(h)], axis=0)

    def chunk(ref, start, kvh, slot=LANES):
        return ref[pl.ds(start, CK), kvh * slot:(kvh + 1) * slot]

    def score_pass(qs_ref, k_ref, ci, slot, bias, m_ref):
        start = pl.multiple_of(ci * CK, CK)
        for kvh in range(NSA_KV):
            s = _dot_nt(qs_ref[kvh], chunk(k_ref, start, kvh))
            if bias is not None:
                s = add_bias(s, bias)
            sbuf[kvh, slot] = s
            m_ref[kvh] = jnp.maximum(m_ref[kvh], fold_max(s))

    def value_pass(v_ref, ci, slot, m_ref, out_ref):
        start = pl.multiple_of(ci * CK, CK)
        for kvh in range(NSA_KV):
            p = jnp.exp2(sbuf[kvh, slot] - lanes_ck(m_ref[kvh]))
            v1 = jnp.concatenate([chunk(v_ref, start, kvh), jnp.ones((CK, LANES), BF16)], axis=1)
            out_ref[kvh] += _dot(p.astype(BF16), v1)

    def begin_max(m_ref):
        for kvh in range(NSA_KV):
            m_ref[kvh] = jnp.full((R, LANES), -3e38, F32)

    def finish_max(m_ref, out_ref):
        for kvh in range(NSA_KV):
            m_ref[kvh] = jnp.broadcast_to(jnp.max(m_ref[kvh], -1, keepdims=True), (R, LANES))
            out_ref[kvh] = jnp.zeros((R, VSLOT), F32)

    def loop(n, body):
        lax.fori_loop(0, n, lambda i, c: (body(i), c)[1], 0)

    def win_chunk(k):
        cw = last - (NW - 1) + k
        return cw, jnp.maximum(cw, 0)

    def win_scores(k):
        cw, cidx = win_chunk(k)
        key = cidx * CK + key_l
        ok = (key <= t_q) & (key > t_q - WINDOW) & (cw >= 0)
        score_pass(qpl_ref, kw_ref, cidx, k, jnp.where(ok, 0.0, NEG), mt_ref)

    sel_scores = lambda ci, bias=None: score_pass(qaug_ref, ks_ref, ci, ci, bias, mt_ref)
    sel_values = lambda ci: value_pass(vs_ref, ci, ci, mt_ref, acc_ref)

    def chunks(n, one):
        loop(n // 4, lambda j: [one(4 * j + u) for u in range(4)])
        loop((n % 4) // 2, lambda j: [one(n - (n % 4) + u) for u in range(2)])
        loop(n % 2, lambda j: one(n - 1))

    begin_max(mt_ref)
    chunks(last, sel_scores)
    sel_scores(last, bias_diag)
    finish_max(mt_ref, acc_ref)
    chunks(n_sel, sel_values)

    begin_max(mt_ref)
    for k in range(NW):
        win_scores(k)
    finish_max(mt_ref, accw_ref)
    for k in range(NW):
        value_pass(vw_ref, win_chunk(k)[1], k, mt_ref, accw_ref)

    left = lane_q < NSA_D
    for kvh in range(NSA_KV):
        for pair in range(NSA_G // 2):
            col = (kvh * NSA_G + 2 * pair) * NSA_D
            r_e = pl.ds(2 * pair * TQ, TQ)
            r_o = pl.ds((2 * pair + 1) * TQ, TQ)

            def packed(ref, lanes):
                return jnp.where(left, ref[kvh, r_e, lanes], pltpu.roll(ref[kvh, r_o, lanes], NSA_D, 1))

            out = packed(cmp_ref, slice(0, LANES)) * gexp_ref[:, col:col + LANES]
            for br, ref in ((1, acc_ref), (2, accw_ref)):
                den = jnp.where(left, ref[kvh, r_e, LANES:VSLOT], ref[kvh, r_o, LANES:VSLOT])
                gate = gexp_ref[:, br * NSA_QW + col:br * NSA_QW + col + LANES]
                out = out + packed(ref, slice(0, LANES)) * (gate / den)
            o_ref[:, col:col + LANES] = out.astype(BF16)


def _window_chunks(S, TQ, CK):
    return max((q0 + TQ - 1) // CK - max(q0 - WINDOW + 1, 0) // CK + 1 for q0 in range(0, S, TQ))


def _nsa(q, ks, vs, kw, vw, kc, vc, gt, B, S, TQ, CK):
    T = B * S
    n_blk = S // SEL_BLOCK
    n_cmp = (S - CMP_LEN) // CMP_STRIDE + 1
    R = NSA_G * TQ
    NW = _window_chunks(S, TQ, CK)
    assert kc.shape[2] == LANES and n_blk <= LANES - _L_OVL and S // CK >= NW
    seq = lambda w: pl.BlockSpec((S, w), lambda b, i: (b, 0))
    qrow = lambda w: pl.BlockSpec((TQ, w), lambda b, i: (b * (S // TQ) + i, 0))
    cmp_spec = lambda w: pl.BlockSpec((1, NSA_KV, LANES, w), lambda b, i: (b, 0, 0, 0))
    kern = functools.partial(_nsa_kernel, TQ=TQ, CK=CK, NW=NW, n_cmp=n_cmp, n_blk=n_blk,
                             top_n=min(SEL_TOPN, n_blk))
    per_head = lambda w, dt: pltpu.VMEM((NSA_KV, R, w), dt)
    gx = jnp.asarray(_gate_expand_table(), BF16)
    return pl.pallas_call(
        kern,
        grid=(B, S // TQ),
        in_specs=[qrow(NSA_QW), seq(SLOT_W), seq(SLOT_W), seq(SLOT_W), seq(SLOT_W),
                  cmp_spec(LANES), cmp_spec(VSLOT), qrow(LANES), _const_spec(gx.shape)],
        out_specs=qrow(NSA_QW),
        out_shape=jax.ShapeDtypeStruct((T, NSA_QW), BF16),
        scratch_shapes=[pltpu.VMEM((NSA_KV, S // CK, R, CK), F32),
                        per_head(LANES, BF16), per_head(LANES, BF16),
                        per_head(VSLOT, F32), per_head(VSLOT, F32),
                        per_head(LANES, F32),
                        per_head(LANES, F32),
                        pltpu.VMEM((TQ, 3 * NSA_QW), F32)],
        compiler_params=_params(("parallel", "parallel")),
        name="nsa",
    )(q, ks, vs, kw, vw, kc, vc, gt, gx)


def _merge_kernel(x_ref, a_ref, b_ref, wga_ref, wgb_ref, wua_ref, wub_ref, wo_ref, g_ref, bb_ref, o_ref):
    x = x_ref[...]
    xb = x.astype(BF16)
    m = (_sigmoid(_dot(xb, wga_ref[...])) * _dot(a_ref[...], wua_ref[...])
         + _sigmoid(_dot(xb, wgb_ref[...])) * _dot(b_ref[...], wub_ref[...]))
    mix = _dot(m.astype(BF16), wo_ref[...])
    o_ref[...] = _layer_norm(DN_ALPHA * x + mix, g_ref[...], bb_ref[...])


def _merge(x2, a, b, wga, wgb, wua, wub, wo, g, bb, tm):
    T = x2.shape[0]
    row = pl.BlockSpec((tm, D_MODEL), lambda i: (i, 0))
    wsp = _const_spec((D_MODEL, D_MODEL))
    vec = _const_spec((1, D_MODEL))
    return pl.pallas_call(
        _merge_kernel,
        grid=(T // tm,),
        in_specs=[row, row, row, wsp, wsp, wsp, wsp, wsp, vec, vec],
        out_specs=row,
        out_shape=jax.ShapeDtypeStruct((T, D_MODEL), F32),
        compiler_params=_params(("parallel",)),
        name="merge",
    )(x2, a, b, wga, wgb, wua, wub, wo, g, bb)


def _ffn_kernel(h_ref, wg_ref, wu_ref, wd_ref, g_ref, b_ref, o_ref):
    h = h_ref[...]
    hb = h.astype(BF16)
    gate = _dot(hb, wg_ref[...])
    act = (gate * _sigmoid(gate) * _dot(hb, wu_ref[...])).astype(BF16)
    ffn = _dot(act, wd_ref[...])
    o_ref[...] = _layer_norm(DN_ALPHA * h + ffn, g_ref[...], b_ref[...])


def _ffn(h, wg, wu, wd, g, b, tm):
    T = h.shape[0]
    d_ff = wg.shape[1]
    row = pl.BlockSpec((tm, D_MODEL), lambda i: (i, 0))
    vec = _const_spec((1, D_MODEL))
    once = pl.Buffered(1)
    return pl.pallas_call(
        _ffn_kernel,
        grid=(T // tm,),
        in_specs=[row,
                  pl.BlockSpec((D_MODEL, d_ff), lambda i: (0, 0), pipeline_mode=once),
                  pl.BlockSpec((D_MODEL, d_ff), lambda i: (0, 0), pipeline_mode=once),
                  pl.BlockSpec((d_ff, D_MODEL), lambda i: (0, 0), pipeline_mode=once),
                  vec, vec],
        out_specs=row,
        out_shape=jax.ShapeDtypeStruct((T, D_MODEL), F32),
        compiler_params=_params(("parallel",)),
        name="ffn",
    )(h, wg, wu, wd, g, b)


def _pad_slots(w):
    d = w.shape[0]
    w = w.reshape(d, NSA_KV, NSA_D)
    return jnp.pad(w, ((0, 0), (0, 0), (0, LANES - NSA_D))).reshape(d, SLOT_W)


def _cmp_value_table(n_half, n_cmp, n_blk):
    cs = np.arange(n_cmp)[:, None] * CMP_STRIDE
    bs = np.arange(n_blk)[None, :] * SEL_BLOCK
    ov = np.minimum(cs + CMP_LEN, bs + SEL_BLOCK) - np.maximum(cs, bs)
    out = np.zeros((n_half, VSLOT), np.float32)
    out[:, LANES:] = 1.0
    out[:n_cmp, _L_OVL:_L_OVL + n_blk] = np.clip(ov, 0, None) / CMP_LEN
    return out


def _gate_expand_table():
    out = np.zeros((LANES, 3 * NSA_QW), np.float32)
    for br in range(3):
        for h in range(NSA_HEADS):
            c = br * NSA_QW + h * NSA_D
            out[br * NSA_HEADS + h, c:c + NSA_D] = 1.0
            out[LANES // 2 + br * NSA_HEADS + h, c:c + NSA_D] = 1.0
    return out


def kernel(x, positions, w_in, hg_lb_logits, hg_norm_g, cmp_k_pos, cmp_k_w1, cmp_k_b1, cmp_k_w2, cmp_k_b2, cmp_v_pos, cmp_v_w1, cmp_v_b1, cmp_v_w2, cmp_v_b2, w_up_hg, w_up_nsa, w_o, ln1_g, ln1_b, w_ffn_gate, w_ffn_up, w_ffn_down, ln2_g, ln2_b):
    B, S, _ = x.shape
    T = B * S
    tm = min(512, S)
    TQ = 128
    CK = min(256, S)
    n_blk = S // SEL_BLOCK
    n_cmp = (S - CMP_LEN) // CMP_STRIDE + 1
    n_half = S // CMP_STRIDE

    lb_table = jnp.cumsum(jax.nn.softmax(hg_lb_logits.astype(F32), axis=0), axis=0)
    x2 = x.reshape(T, D_MODEL)

    w = w_in[0]
    o = 0
    o += 4 * HG_W
    wq = w[:, o:o + NSA_QW]; o += NSA_QW
    kv = []
    for _ in range(6):
        kv.append(w[:, o:o + NSA_KVW]); o += NSA_KVW
    w_kc, w_vc, w_ks, w_vs, w_kw, w_vw = kv
    w_gt = jnp.pad(w[:, o:o + 3 * NSA_HEADS], ((0, 0), (0, LANES - 3 * NSA_HEADS))); o += 3 * NSA_HEADS
    w_ga = w[:, o:o + D_MODEL]; o += D_MODEL
    w_gb = w[:, o:o + D_MODEL]
    w_nsa = jnp.concatenate([wq, w_kc, _pad_slots(w_ks), _pad_slots(w_kw),
                             w_vc, _pad_slots(w_vs), _pad_slots(w_vw), w_gt], axis=1).astype(BF16)

    half = NSA_D // 2
    inv_freq = ROPE_THETA ** (-jnp.arange(half, dtype=F32) / half)
    ang = positions.astype(F32).reshape(T, 1) * inv_freq[None, :]
    cos_t = jnp.tile(jnp.cos(ang), (1, LANES // half))
    sin_h = jnp.sin(ang)
    sin_t = jnp.tile(jnp.concatenate([-sin_h, sin_h], axis=1), (1, LANES // NSA_D))

    e_np = np.zeros((S, NSA_KV, LANES), np.float32)
    e_np[np.arange(S), :, NSA_D + np.arange(S) // SEL_BLOCK] = 1.0
    e_tab = jnp.asarray(e_np.reshape(S, SLOT_W), BF16)

    hq, hf, hi, hg = _hg_proj(x2, w.astype(BF16), tm)
    a = _hgrn(hq, hf, hi, hg, lb_table[0:1], hg_norm_g[0:1].astype(F32), B, S)

    q, kc_tok, ks, kw, vc_tok, vs, vw, gt = _nsa_proj(x2, w_nsa, cos_t, sin_t, e_tab, tm, S)

    def slot_rows(m):
        m = m.reshape(CMP_LEN, NSA_D, -1)
        return jnp.pad(m, ((0, 0), (0, LANES - NSA_D), (0, 0))).reshape(CMP_LEN * LANES, -1)

    def pos_rows(p):
        flat = slot_rows(p.reshape(CMP_LEN * NSA_D, 1)).reshape(1, CMP_LEN * LANES)
        return jnp.broadcast_to(flat, (8, CMP_LEN * LANES)).astype(BF16)

    def pad_out(w2, b2, width):
        return (jnp.pad(w2, ((0, 0), (0, width - NSA_D))).astype(BF16),
                jnp.pad(b2, (0, width - NSA_D)).reshape(1, width).astype(F32))

    w2k, b2k = pad_out(cmp_k_w2[0], cmp_k_b2[0], LANES)
    w2v, b2v = pad_out(cmp_v_w2[0], cmp_v_b2[0], VSLOT)
    kc, vc = _compress(
        kc_tok, vc_tok,
        pos_rows(cmp_k_pos[0]), slot_rows(cmp_k_w1[0]).astype(BF16), cmp_k_b1[0].reshape(1, -1).astype(F32),
        w2k, b2k,
        pos_rows(cmp_v_pos[0]), slot_rows(cmp_v_w1[0]).astype(BF16), cmp_v_b1[0].reshape(1, -1).astype(F32),
        w2v, b2v,
        jnp.asarray(_cmp_value_table(n_half, n_cmp, n_blk)), B, S)

    b_out = _nsa(q, ks, vs, kw, vw, kc, vc, gt, B, S, TQ, CK)

    h1 = _merge(x2, a, b_out, w_ga.astype(BF16), w_gb.astype(BF16),
                w_up_hg[0].astype(BF16), w_up_nsa[0].astype(BF16), w_o[0].astype(BF16),
                ln1_g[0].reshape(1, -1).astype(F32), ln1_b[0].reshape(1, -1).astype(F32), tm)
    out = _ffn(h1, w_ffn_gate[0].astype(BF16), w_ffn_up[0].astype(BF16), w_ffn_down[0].astype(BF16),
               ln2_g[0].reshape(1, -1).astype(F32), ln2_b[0].reshape(1, -1).astype(F32), tm)
    return out.reshape(B, S, D_MODEL)
```

```python
import functools

import numpy as np
import jax
import jax.numpy as jnp
from jax import lax
from jax.experimental import pallas as pl
from jax.experimental.pallas import tpu as pltpu

F32 = jnp.float32
BF16 = jnp.bfloat16

D_MODEL = 1024
HG_HEADS = 8
HG_DK = 128
HG_DV = 128
HG_W = HG_HEADS * HG_DK
HG_CHUNK = 64
NSA_HEADS = 16
NSA_KV = 4
NSA_G = 4
NSA_D = 64
NSA_QW = NSA_HEADS * NSA_D
NSA_KVW = NSA_KV * NSA_D
CMP_LEN = 32
CMP_STRIDE = 16
CMP_HIDDEN = 256
SEL_BLOCK = 64
SEL_TOPN = 8
SEL_FORCE = 1000.0
WINDOW = 512
ROPE_THETA = 10000.0
DEPTH = 1
DN_ALPHA = (2.0 * DEPTH) ** 0.25
LN_EPS = 1e-5
RMS_EPS = 1e-6
LOG2E = 1.4426950408889634
NEG = -1e30

LANES = 128
SLOT_W = NSA_KV * LANES
VSLOT = 2 * LANES
VMEM_LIMIT = 56 * 1024 * 1024


def _dot(a, b):
    return jnp.dot(a, b, preferred_element_type=F32)


def _dot_nt(a, b):
    return lax.dot_general(a, b, (((1,), (1,)), ((), ())), preferred_element_type=F32)


def _dot_tn(a, b):
    return lax.dot_general(a, b, (((0,), (0,)), ((), ())), preferred_element_type=F32)


def _sigmoid(x):
    return 1.0 / (1.0 + jnp.exp(-x))


def _layer_norm(x, g, b):
    mu = jnp.mean(x, -1, keepdims=True)
    xc = x - mu
    var = jnp.mean(xc * xc, -1, keepdims=True)
    return xc * lax.rsqrt(var + LN_EPS) * g + b


def _params(sem):
    return pltpu.CompilerParams(dimension_semantics=sem, vmem_limit_bytes=VMEM_LIMIT)


def _const_spec(shape):
    nd = len(shape)
    return pl.BlockSpec(shape, lambda *_: (0,) * nd)


def _hg_proj_kernel(x_ref, w_ref, q_ref, f_ref, i_ref, g_ref):
    x = x_ref[...].astype(BF16)
    q_ref[...] = _dot(x, w_ref[:, 0:HG_W]).astype(BF16)
    f_ref[...] = _dot(x, w_ref[:, HG_W:2 * HG_W])
    i_ref[...] = _dot(x, w_ref[:, 2 * HG_W:3 * HG_W]).astype(BF16)
    g = _dot(x, w_ref[:, 3 * HG_W:4 * HG_W])
    g_ref[...] = (g * _sigmoid(g)).astype(BF16)


def _hg_proj(x2, w_hg, tm):
    T = x2.shape[0]
    row = lambda i: (i, 0)
    out_spec = pl.BlockSpec((tm, HG_W), row)
    return pl.pallas_call(
        _hg_proj_kernel,
        grid=(T // tm,),
        in_specs=[pl.BlockSpec((tm, D_MODEL), row), _const_spec((D_MODEL, 4 * HG_W))],
        out_specs=[out_spec] * 4,
        out_shape=[jax.ShapeDtypeStruct((T, HG_W), BF16), jax.ShapeDtypeStruct((T, HG_W), F32),
                   jax.ShapeDtypeStruct((T, HG_W), BF16), jax.ShapeDtypeStruct((T, HG_W), BF16)],
        compiler_params=_params(("parallel",)),
        name="hg_proj",
    )(x2, w_hg)


_NQ = NSA_QW
_NKC = NSA_KVW
_NSL = SLOT_W
_O_K = _NQ
_O_V = _O_K + 3 * _NKC
_O_GT = _O_V + 3 * _NKC
_NSA_W = _O_GT + LANES


def _nsa_proj_kernel(x_ref, w_ref, cos_ref, sin_ref, e_ref,
                     q_ref, kc_ref, ks_ref, kw_ref, vc_ref, vs_ref, vw_ref, gt_ref):
    x = x_ref[...].astype(BF16)
    cos = cos_ref[...]
    sin = sin_ref[...]
    half = NSA_D // 2

    def rope(y):
        w = y.shape[1]
        reps = w // LANES
        lane = lax.broadcasted_iota(jnp.int32, y.shape, 1)
        fwd = pltpu.roll(y, w - half, 1)
        bwd = pltpu.roll(y, half, 1)
        rot = jnp.where((lane % NSA_D) < half, fwd, bwd)
        return y * jnp.tile(cos, (1, reps)) + rot * jnp.tile(sin, (1, reps))

    yq = rope(_dot(x, w_ref[:, 0:_NQ]))
    q_ref[...] = (yq * (NSA_D ** -0.5 * LOG2E)).astype(BF16)

    def slots(y):
        left = lax.broadcasted_iota(jnp.int32, (y.shape[0], LANES), 1) < NSA_D
        out = []
        for c in range(y.shape[1] // LANES):
            two = y[:, c * LANES:(c + 1) * LANES]
            out += [jnp.where(left, two, 0.0), jnp.where(left, pltpu.roll(two, NSA_D, 1), 0.0)]
        return out

    def per_head(y, o_ref):
        for h, slab in enumerate(slots(y)):
            o_ref[h] = slab

    def slot_row(y):
        return jnp.concatenate(slots(y), axis=1).astype(BF16)

    yk = rope(_dot(x, w_ref[:, _O_K:_O_K + 3 * _NKC]))
    per_head(yk[:, 0:_NKC], kc_ref)
    ks_ref[...] = slot_row(yk[:, _NKC:2 * _NKC]) + e_ref[...]
    kw_ref[...] = slot_row(yk[:, 2 * _NKC:3 * _NKC])
    yv = _dot(x, w_ref[:, _O_V:_O_V + 3 * _NKC])
    per_head(yv[:, 0:_NKC], vc_ref)
    vs_ref[...] = slot_row(yv[:, _NKC:2 * _NKC])
    vw_ref[...] = slot_row(yv[:, 2 * _NKC:3 * _NKC])
    gates = _sigmoid(_dot(x, w_ref[:, _O_GT:_O_GT + LANES]))
    g_hi = gates.astype(BF16).astype(F32)
    lane = lax.broadcasted_iota(jnp.int32, gates.shape, 1)
    gt_ref[...] = jnp.where(lane < LANES // 2, g_hi, pltpu.roll(gates - g_hi, LANES // 2, 1)).astype(BF16)


def _nsa_proj(x2, w_nsa, cos_t, sin_t, e_tab, tm, S):
    T = x2.shape[0]
    row = lambda i: (i, 0)
    tiles_per_seq = S // tm
    tok = lambda w: (pl.BlockSpec((tm, w), row), jax.ShapeDtypeStruct((T, w), BF16))
    heads = (pl.BlockSpec((NSA_KV, tm, LANES), lambda i: (0, i, 0)),
             jax.ShapeDtypeStruct((NSA_KV, T, LANES), F32))
    outs = [tok(_NQ), heads, tok(_NSL), tok(_NSL), heads, tok(_NSL), tok(_NSL), tok(LANES)]
    return pl.pallas_call(
        _nsa_proj_kernel,
        grid=(T // tm,),
        in_specs=[pl.BlockSpec((tm, D_MODEL), row), _const_spec((D_MODEL, _NSA_W)),
                  pl.BlockSpec((tm, LANES), row), pl.BlockSpec((tm, LANES), row),
                  pl.BlockSpec((tm, _NSL), lambda i: (i % tiles_per_seq, 0))],
        out_specs=[spec for spec, _ in outs],
        out_shape=[shape for _, shape in outs],
        compiler_params=_params(("parallel",)),
        name="nsa_proj",
    )(x2, w_nsa, cos_t, sin_t, e_tab)


def _hgrn_kernel(q_ref, f_ref, i_ref, g_ref, lb_ref, ng_ref, o_ref, *, n_chunks, heads, unroll):
    C = HG_CHUNK
    ng = ng_ref[...]
    r = lax.broadcasted_iota(jnp.int32, (C, C), 0)
    c = lax.broadcasted_iota(jnp.int32, (C, C), 1)
    tril = r >= c
    tril_b = tril.astype(BF16)

    def chunk_cumsum(x):
        hi = x.astype(BF16)
        lo = (x - hi.astype(F32)).astype(BF16)
        return _dot(tril_b, hi) + _dot(tril_b, lo)

    def body(cj, states):
        states = list(states)
        items = [(u, h) for u in range(unroll) for h in range(heads)]
        rows = {it: pl.ds(pl.multiple_of((cj * unroll + it[0]) * C, C), C) for it in items}
        cols = {it: slice(it[1] * HG_DK, (it[1] + 1) * HG_DK) for it in items}
        f, eb, q_dec, k_inv, k_end, attn, out = {}, {}, {}, {}, {}, {}, {}
        for it in items:
            lb = lb_ref[:, cols[it]]
            f[it] = lb + (1.0 - lb) * _sigmoid(f_ref[rows[it], cols[it]])
        for it in items:
            eb[it] = jnp.exp(chunk_cumsum(jnp.log(f[it])))
        for it in items:
            q_dec[it] = (q_ref[rows[it], cols[it]].astype(F32) * eb[it]).astype(BF16)
            k_inv_f = (1.0 - f[it]) * (1.0 / eb[it])
            k_inv[it] = k_inv_f.astype(BF16)
            k_end[it] = (k_inv_f * eb[it][C - 1:C, :]).astype(BF16)
        for it in items:
            attn[it] = jnp.where(tril, _dot_nt(q_dec[it], k_inv[it]), 0.0).astype(BF16)
        for it in items:
            h = it[1]
            v = i_ref[rows[it], cols[it]]
            out[it] = _dot(attn[it], v) + _dot_nt(q_dec[it], states[h].astype(BF16))
            states[h] = states[h] * eb[it][C - 1:C, :] + _dot_tn(v, k_end[it])
        for it in items:
            o = out[it]
            on = o * lax.rsqrt(jnp.mean(o * o, -1, keepdims=True) + RMS_EPS) * ng
            o_ref[rows[it], cols[it]] = (on * g_ref[rows[it], cols[it]].astype(F32)).astype(BF16)
        return tuple(states)

    init = tuple(jnp.zeros((HG_DV, HG_DK), F32) for _ in range(heads))
    lax.fori_loop(0, n_chunks // unroll, body, init)


def _hgrn(q, f, i, g, lb, ng, B, S, heads=8, unroll=8):
    T = B * S
    w = heads * HG_DK
    blk = pl.BlockSpec((S, w), lambda b, h: (b, h))
    return pl.pallas_call(
        functools.partial(_hgrn_kernel, n_chunks=S // HG_CHUNK, heads=heads, unroll=unroll),
        grid=(B, HG_HEADS // heads),
        in_specs=[blk, blk, blk, blk,
                  pl.BlockSpec((1, w), lambda b, h: (0, h)),
                  pl.BlockSpec((1, HG_DV), lambda b, h: (0, 0))],
        out_specs=blk,
        out_shape=jax.ShapeDtypeStruct((T, HG_W), BF16),
        compiler_params=_params(("parallel", "parallel")),
        name="hgrn",
    )(q, f, i, g, lb, ng)


def _compress_kernel(tk_ref, tv_ref, pk_ref, w1k_ref, b1k_ref, w2k_ref, b2k_ref,
                     pv_ref, w1v_ref, b1v_ref, w2v_ref, b2v_ref, tab_ref, ko_ref, vo_ref, u_ref, *, n_half):
    half = CMP_STRIDE * LANES

    def mlp(t_ref, pos_ref, w1_ref, b1_ref, w2_ref, b2_ref, o_ref, tab=None):
        w_top = w1_ref[0:half, :]
        w_bot = w1_ref[half:2 * half, :]
        base = _dot(pos_ref[...], w1_ref[...])[0:1, :] + b1_ref[...]
        for h in range(NSA_KV):
            for l in range(CMP_STRIDE):
                u_ref[h * n_half:(h + 1) * n_half, l * LANES:(l + 1) * LANES] = (
                    t_ref[h, pl.ds(l, n_half, stride=CMP_STRIDE), :].astype(BF16))
        u = u_ref[...]
        top = _dot(u, w_top)
        bot = _dot(u, w_bot)
        hids = []
        for h in range(NSA_KV):
            r0 = h * n_half
            bot_next = jnp.concatenate([bot[r0 + 1:r0 + n_half, :], jnp.zeros((1, CMP_HIDDEN), F32)], axis=0)
            hids.append(top[r0:r0 + n_half, :] + bot_next + base)
        hid = jnp.concatenate(hids, axis=0)
        hid = hid * _sigmoid(hid)
        out = _dot(hid.astype(BF16), w2_ref[...]) + b2_ref[...]
        for h in range(NSA_KV):
            out_h = out[h * n_half:(h + 1) * n_half, :]
            o_ref[0, h] = (out_h if tab is None else out_h + tab).astype(BF16)

    mlp(tk_ref, pk_ref, w1k_ref, b1k_ref, w2k_ref, b2k_ref, ko_ref)
    mlp(tv_ref, pv_ref, w1v_ref, b1v_ref, w2v_ref, b2v_ref, vo_ref, tab_ref[...])


def _compress(tk, tv, pk, w1k, b1k, w2k, b2k, pv, w1v, b1v, w2v, b2v, tab, B, S):
    nh = S // CMP_STRIDE
    t_spec = pl.BlockSpec((NSA_KV, S, LANES), lambda b: (0, b, 0))
    o_spec = lambda w: pl.BlockSpec((1, NSA_KV, nh, w), lambda b: (b, 0, 0, 0))
    specs = lambda arrs: [_const_spec(a.shape) for a in arrs]
    return pl.pallas_call(
        functools.partial(_compress_kernel, n_half=nh),
        grid=(B,),
        in_specs=([t_spec, t_spec] + specs((pk, w1k, b1k, w2k, b2k)) + specs((pv, w1v, b1v, w2v, b2v))
                  + [_const_spec(tab.shape)]),
        out_specs=[o_spec(LANES), o_spec(VSLOT)],
        out_shape=[jax.ShapeDtypeStruct((B, NSA_KV, nh, LANES), BF16),
                   jax.ShapeDtypeStruct((B, NSA_KV, nh, VSLOT), BF16)],
        scratch_shapes=[pltpu.VMEM((NSA_KV * nh, CMP_STRIDE * LANES), BF16)],
        compiler_params=_params(("parallel",)),
        name="compress",
    )(tk, tv, pk, w1k, b1k, w2k, b2k, pv, w1v, b1v, w2v, b2v, tab)


_L_OVL = LANES - 32


def _nsa_kernel(q_ref, ks_ref, vs_ref, kw_ref, vw_ref, kc_ref, vc_ref, gt_ref, gx_ref,
                o_ref, sbuf, qpl_ref, qaug_ref, acc_ref, accw_ref, mt_ref, cmp_ref, gexp_ref,
                *, TQ, CK, NW, n_cmp, n_blk, top_n):
    R = NSA_G * TQ
    NH = CK // LANES
    q0 = pl.program_id(1) * TQ
    lane_q = lax.broadcasted_iota(jnp.int32, (TQ, LANES), 1)
    t_q = q0 + lax.broadcasted_iota(jnp.int32, (TQ, 1), 0)

    def add_bias(s, bias):
        return jnp.concatenate([s[g * TQ:(g + 1) * TQ] + bias for g in range(NSA_G)], axis=0)

    def fold_max(s):
        return functools.reduce(jnp.maximum, [s[:, u * LANES:(u + 1) * LANES] for u in range(NH)])

    def lanes_ck(m):
        return jnp.concatenate([m] * NH, axis=1)

    bias_c = jnp.where((lane_q * CMP_STRIDE + (CMP_LEN - 1) <= t_q) & (lane_q < n_cmp), 0.0, NEG)
    key_l = lax.broadcasted_iota(jnp.int32, (TQ, CK), 1)
    n_sel = (q0 + TQ + CK - 1) // CK
    last = n_sel - 1
    bias_diag = jnp.where(last * CK + key_l <= t_q, 0.0, NEG)

    j_blk = lax.broadcasted_iota(jnp.int32, (n_blk, TQ), 0)
    cur = (q0 + lax.broadcasted_iota(jnp.int32, (n_blk, TQ), 1)) // SEL_BLOCK
    blk_ok = j_blk <= cur
    forced = (j_blk == 0) | (j_blk == cur) | (j_blk == cur - 1)

    gexp_ref[...] = _dot(gt_ref[...], gx_ref[...])

    def q_slabs(kvh):
        slabs = []
        for pair in range(NSA_G // 2):
            col = (kvh * NSA_G + 2 * pair) * NSA_D
            x = q_ref[:, col:col + LANES].astype(F32)
            slabs += [x, pltpu.roll(x, NSA_D, 1)]
        return slabs

    for kvh in range(NSA_KV):
        qpl_ref[kvh] = jnp.concatenate(
            [jnp.where(lane_q < NSA_D, x, 0.0).astype(BF16) for x in q_slabs(kvh)], axis=0)

    KV = range(NSA_KV)
    s_c = [add_bias(_dot_nt(qpl_ref[h], kc_ref[0, h]), bias_c) for h in KV]
    m_c = [jnp.maximum(jnp.max(s, -1, keepdims=True), 0.1 * NEG) for s in s_c]
    acc_c = [_dot(jnp.exp2(s_c[h] - m_c[h]).astype(BF16), vc_ref[0, h]) for h in KV]
    p_n = [a[:, 0:LANES] * (1.0 / jnp.maximum(a[:, LANES:VSLOT], 1e-30)) for a in acc_c]
    for h in KV:
        cmp_ref[h] = p_n[h]
    imp = [functools.reduce(jnp.add, [p[g * TQ:(g + 1) * TQ] for g in range(NSA_G)]).T[_L_OVL:_L_OVL + n_blk, :]
           for p in p_n]
    score = [jnp.where(blk_ok, x + jnp.where(forced, SEL_FORCE, 0.0), -1.0) for x in imp]
    rank = [jnp.zeros((n_blk, TQ), jnp.int32) for _ in KV]
    for i in range(n_blk):
        for h in KV:
            ri = score[h][i:i + 1, :]
            beats = (ri > score[h]) | ((ri == score[h]) & (j_blk > i))
            rank[h] = rank[h] + beats.astype(jnp.int32)
    for h in KV:
        bias_t = jnp.where((rank[h] < top_n) & blk_ok, 0.0, NEG)
        bias_q = jnp.concatenate([jnp.zeros((NSA_D, TQ), F32), bias_t,
                                  jnp.zeros((LANES - NSA_D - n_blk, TQ), F32)], axis=0).T
        qaug_ref[h] = jnp.concatenate(
            [jnp.where(lane_q < NSA_D, x, bias_q).astype(BF16) for x in q_slabs(h)], axis=0)

    def chunk(ref, start, kvh, slot=LANES):
        return ref[pl.ds(start, CK), kvh * slot:(kvh + 1) * slot]

    def score_pass(qs_ref, k_ref, ci, slot, bias, m_ref):
        start = pl.multiple_of(ci * CK, CK)
        for kvh in range(NSA_KV):
            s = _dot_nt(qs_ref[kvh], chunk(k_ref, start, kvh))
            if bias is not None:
                s = add_bias(s, bias)
            sbuf[kvh, slot] = s
            m_ref[kvh] = jnp.maximum(m_ref[kvh], fold_max(s))

    def value_pass(v_ref, ci, slot, m_ref, out_ref):
        start = pl.multiple_of(ci * CK, CK)
        for kvh in range(NSA_KV):
            p = jnp.exp2(sbuf[kvh, slot] - lanes_ck(m_ref[kvh]))
            v1 = jnp.concatenate([chunk(v_ref, start, kvh), jnp.ones((CK, LANES), BF16)], axis=1)
            out_ref[kvh] += _dot(p.astype(BF16), v1)

    def begin_max(m_ref):
        for kvh in range(NSA_KV):
            m_ref[kvh] = jnp.full((R, LANES), -3e38, F32)

    def finish_max(m_ref, out_ref):
        for kvh in range(NSA_KV):
            m_ref[kvh] = jnp.broadcast_to(jnp.max(m_ref[kvh], -1, keepdims=True), (R, LANES))
            out_ref[kvh] = jnp.zeros((R, VSLOT), F32)

    def loop(n, body):
        lax.fori_loop(0, n, lambda i, c: (body(i), c)[1], 0)

    def win_chunk(k):
        cw = last - (NW - 1) + k
        return cw, jnp.maximum(cw, 0)

    def win_scores(k):
        cw, cidx = win_chunk(k)
        key = cidx * CK + key_l
        ok = (key <= t_q) & (key > t_q - WINDOW) & (cw >= 0)
        score_pass(qpl_ref, kw_ref, cidx, k, jnp.where(ok, 0.0, NEG), mt_ref)

    sel_scores = lambda ci, bias=None: score_pass(qaug_ref, ks_ref, ci, ci, bias, mt_ref)
    sel_values = lambda ci: value_pass(vs_ref, ci, ci, mt_ref, acc_ref)

    def chunks(n, one):
        loop(n // 4, lambda j: [one(4 * j + u) for u in range(4)])
        loop((n % 4) // 2, lambda j: [one(n - (n % 4) + u) for u in range(2)])
        loop(n % 2, lambda j: one(n - 1))

    begin_max(mt_ref)
    chunks(last, sel_scores)
    sel_scores(last, bias_diag)
    finish_max(mt_ref, acc_ref)
    chunks(n_sel, sel_values)

    begin_max(mt_ref)
    for k in range(NW):
        win_scores(k)
    finish_max(mt_ref, accw_ref)
    for k in range(NW):
        value_pass(vw_ref, win_chunk(k)[1], k, mt_ref, accw_ref)

    left = lane_q < NSA_D
    for kvh in range(NSA_KV):
        for pair in range(NSA_G // 2):
            col = (kvh * NSA_G + 2 * pair) * NSA_D
            r_e = pl.ds(2 * pair * TQ, TQ)
            r_o = pl.ds((2 * pair + 1) * TQ, TQ)

            def packed(ref, lanes):
                return jnp.where(left, ref[kvh, r_e, lanes], pltpu.roll(ref[kvh, r_o, lanes], NSA_D, 1))

            out = packed(cmp_ref, slice(0, LANES)) * gexp_ref[:, col:col + LANES]
            for br, ref in ((1, acc_ref), (2, accw_ref)):
                den = jnp.where(left, ref[kvh, r_e, LANES:VSLOT], ref[kvh, r_o, LANES:VSLOT])
                gate = gexp_ref[:, br * NSA_QW + col:br * NSA_QW + col + LANES]
                out = out + packed(ref, slice(0, LANES)) * (gate / den)
            o_ref[:, col:col + LANES] = out.astype(BF16)


def _window_chunks(S, TQ, CK):
    return max((q0 + TQ - 1) // CK - max(q0 - WINDOW + 1, 0) // CK + 1 for q0 in range(0, S, TQ))


def _nsa(q, ks, vs, kw, vw, kc, vc, gt, B, S, TQ, CK):
    T = B * S
    n_blk = S // SEL_BLOCK
    n_cmp = (S - CMP_LEN) // CMP_STRIDE + 1
    R = NSA_G * TQ
    NW = _window_chunks(S, TQ, CK)
    assert kc.shape[2] == LANES and n_blk <= LANES - _L_OVL and S // CK >= NW
    seq = lambda w: pl.BlockSpec((S, w), lambda b, i: (b, 0))
    qrow = lambda w: pl.BlockSpec((TQ, w), lambda b, i: (b * (S // TQ) + i, 0))
    cmp_spec = lambda w: pl.BlockSpec((1, NSA_KV, LANES, w), lambda b, i: (b, 0, 0, 0))
    kern = functools.partial(_nsa_kernel, TQ=TQ, CK=CK, NW=NW, n_cmp=n_cmp, n_blk=n_blk,
                             top_n=min(SEL_TOPN, n_blk))
    per_head = lambda w, dt: pltpu.VMEM((NSA_KV, R, w), dt)
    gx = jnp.asarray(_gate_expand_table(), BF16)
    return pl.pallas_call(
        kern,
        grid=(B, S // TQ),
        in_specs=[qrow(NSA_QW), seq(SLOT_W), seq(SLOT_W), seq(SLOT_W), seq(SLOT_W),
                  cmp_spec(LANES), cmp_spec(VSLOT), qrow(LANES), _const_spec(gx.shape)],
        out_specs=qrow(NSA_QW),
        out_shape=jax.ShapeDtypeStruct((T, NSA_QW), BF16),
        scratch_shapes=[pltpu.VMEM((NSA_KV, S // CK, R, CK), F32),
                        per_head(LANES, BF16), per_head(LANES, BF16),
                        per_head(VSLOT, F32), per_head(VSLOT, F32),
                        per_head(LANES, F32),
                        per_head(LANES, F32),
                        pltpu.VMEM((TQ, 3 * NSA_QW), F32)],
        compiler_params=_params(("parallel", "parallel")),
        name="nsa",
    )(q, ks, vs, kw, vw, kc, vc, gt, gx)


def _merge_kernel(x_ref, a_ref, b_ref, wga_ref, wgb_ref, wua_ref, wub_ref, wo_ref, g_ref, bb_ref, o_ref):
    x = x_ref[...]
    xb = x.astype(BF16)
    m = (_sigmoid(_dot(xb, wga_ref[...])) * _dot(a_ref[...], wua_ref[...])
         + _sigmoid(_dot(xb, wgb_ref[...])) * _dot(b_ref[...], wub_ref[...]))
    mix = _dot(m.astype(BF16), wo_ref[...])
    o_ref[...] = _layer_norm(DN_ALPHA * x + mix, g_ref[...], bb_ref[...])


def _merge(x2, a, b, wga, wgb, wua, wub, wo, g, bb, tm):
    T = x2.shape[0]
    row = pl.BlockSpec((tm, D_MODEL), lambda i: (i, 0))
    wsp = _const_spec((D_MODEL, D_MODEL))
    vec = _const_spec((1, D_MODEL))
    return pl.pallas_call(
        _merge_kernel,
        grid=(T // tm,),
        in_specs=[row, row, row, wsp, wsp, wsp, wsp, wsp, vec, vec],
        out_specs=row,
        out_shape=jax.ShapeDtypeStruct((T, D_MODEL), F32),
        compiler_params=_params(("parallel",)),
        name="merge",
    )(x2, a, b, wga, wgb, wua, wub, wo, g, bb)


def _ffn_kernel(h_ref, wg_ref, wu_ref, wd_ref, g_ref, b_ref, o_ref):
    h = h_ref[...]
    hb = h.astype(BF16)
    gate = _dot(hb, wg_ref[...])
    act = (gate * _sigmoid(gate) * _dot(hb, wu_ref[...])).astype(BF16)
    ffn = _dot(act, wd_ref[...])
    o_ref[...] = _layer_norm(DN_ALPHA * h + ffn, g_ref[...], b_ref[...])


def _ffn(h, wg, wu, wd, g, b, tm):
    T = h.shape[0]
    d_ff = wg.shape[1]
    row = pl.BlockSpec((tm, D_MODEL), lambda i: (i, 0))
    vec = _const_spec((1, D_MODEL))
    once = pl.Buffered(1)
    return pl.pallas_call(
        _ffn_kernel,
        grid=(T // tm,),
        in_specs=[row,
                  pl.BlockSpec((D_MODEL, d_ff), lambda i: (0, 0), pipeline_mode=once),
                  pl.BlockSpec((D_MODEL, d_ff), lambda i: (0, 0), pipeline_mode=once),
                  pl.BlockSpec((d_ff, D_MODEL), lambda i: (0, 0), pipeline_mode=once),
                  vec, vec],
        out_specs=row,
        out_shape=jax.ShapeDtypeStruct((T, D_MODEL), F32),
        compiler_params=_params(("parallel",)),
        name="ffn",
    )(h, wg, wu, wd, g, b)


def _cmp_value_table(n_half, n_cmp, n_blk):
    cs = np.arange(n_cmp)[:, None] * CMP_STRIDE
    bs = np.arange(n_blk)[None, :] * SEL_BLOCK
    ov = np.minimum(cs + CMP_LEN, bs + SEL_BLOCK) - np.maximum(cs, bs)
    out = np.zeros((n_half, VSLOT), np.float32)
    out[:, LANES:] = 1.0
    out[:n_cmp, _L_OVL:_L_OVL + n_blk] = np.clip(ov, 0, None) / CMP_LEN
    return out


def _gate_expand_table():
    out = np.zeros((LANES, 3 * NSA_QW), np.float32)
    for br in range(3):
        for h in range(NSA_HEADS):
            c = br * NSA_QW + h * NSA_D
            out[br * NSA_HEADS + h, c:c + NSA_D] = 1.0
            out[LANES // 2 + br * NSA_HEADS + h, c:c + NSA_D] = 1.0
    return out


def kernel(x, positions, w_in, hg_lb_logits, hg_norm_g, cmp_k_pos, cmp_k_w1, cmp_k_b1, cmp_k_w2, cmp_k_b2, cmp_v_pos, cmp_v_w1, cmp_v_b1, cmp_v_w2, cmp_v_b2, w_up_hg, w_up_nsa, w_o, ln1_g, ln1_b, w_ffn_gate, w_ffn_up, w_ffn_down, ln2_g, ln2_b):
    B, S, _ = x.shape
    T = B * S
    tm = min(512, S)
    TQ = 128
    CK = min(256, S)
    n_blk = S // SEL_BLOCK
    n_cmp = (S - CMP_LEN) // CMP_STRIDE + 1
    n_half = S // CMP_STRIDE

    lb_table = jnp.cumsum(jax.nn.softmax(hg_lb_logits.astype(F32), axis=0), axis=0)
    x2 = x.reshape(T, D_MODEL)

    w = w_in[0]
    o = 0
    o += 4 * HG_W
    wq = w[:, o:o + NSA_QW]; o += NSA_QW
    kv = []
    for _ in range(6):
        kv.append(w[:, o:o + NSA_KVW]); o += NSA_KVW
    w_kc, w_vc, w_ks, w_vs, w_kw, w_vw = kv
    w_gt = jnp.pad(w[:, o:o + 3 * NSA_HEADS], ((0, 0), (0, LANES - 3 * NSA_HEADS))); o += 3 * NSA_HEADS
    w_ga = w[:, o:o + D_MODEL]; o += D_MODEL
    w_gb = w[:, o:o + D_MODEL]
    w_nsa = jnp.concatenate([wq, w_kc, w_ks, w_kw, w_vc, w_vs, w_vw, w_gt], axis=1).astype(BF16)

    half = NSA_D // 2
    inv_freq = ROPE_THETA ** (-jnp.arange(half, dtype=F32) / half)
    ang = positions.astype(F32).reshape(T, 1) * inv_freq[None, :]
    cos_t = jnp.tile(jnp.cos(ang), (1, LANES // half))
    sin_h = jnp.sin(ang)
    sin_t = jnp.tile(jnp.concatenate([-sin_h, sin_h], axis=1), (1, LANES // NSA_D))

    e_np = np.zeros((S, NSA_KV, LANES), np.float32)
    e_np[np.arange(S), :, NSA_D + np.arange(S) // SEL_BLOCK] = 1.0
    e_tab = jnp.asarray(e_np.reshape(S, SLOT_W), BF16)

    hq, hf, hi, hg = _hg_proj(x2, w.astype(BF16), tm)
    a = _hgrn(hq, hf, hi, hg, lb_table[0:1], hg_norm_g[0:1].astype(F32), B, S)

    q, kc_tok, ks, kw, vc_tok, vs, vw, gt = _nsa_proj(x2, w_nsa, cos_t, sin_t, e_tab, tm, S)

    def slot_rows(m):
        m = m.reshape(CMP_LEN, NSA_D, -1)
        return jnp.pad(m, ((0, 0), (0, LANES - NSA_D), (0, 0))).reshape(CMP_LEN * LANES, -1)

    def pos_rows(p):
        flat = slot_rows(p.reshape(CMP_LEN * NSA_D, 1)).reshape(1, CMP_LEN * LANES)
        return jnp.broadcast_to(flat, (8, CMP_LEN * LANES)).astype(BF16)

    def pad_out(w2, b2, width):
        return (jnp.pad(w2, ((0, 0), (0, width - NSA_D))).astype(BF16),
                jnp.pad(b2, (0, width - NSA_D)).reshape(1, width).astype(F32))

    w2k, b2k = pad_out(cmp_k_w2[0], cmp_k_b2[0], LANES)
    w2v, b2v = pad_out(cmp_v_w2[0], cmp_v_b2[0], VSLOT)
    kc, vc = _compress(
        kc_tok, vc_tok,
        pos_rows(cmp_k_pos[0]), slot_rows(cmp_k_w1[0]).astype(BF16), cmp_k_b1[0].reshape(1, -1).astype(F32),
        w2k, b2k,
        pos_rows(cmp_v_pos[0]), slot_rows(cmp_v_w1[0]).astype(BF16), cmp_v_b1[0].reshape(1, -1).astype(F32),
        w2v, b2v,
        jnp.asarray(_cmp_value_table(n_half, n_cmp, n_blk)), B, S)

    b_out = _nsa(q, ks, vs, kw, vw, kc, vc, gt, B, S, TQ, CK)

    h1 = _merge(x2, a, b_out, w_ga.astype(BF16), w_gb.astype(BF16),
                w_up_hg[0].astype(BF16), w_up_nsa[0].astype(BF16), w_o[0].astype(BF16),
                ln1_g[0].reshape(1, -1).astype(F32), ln1_b[0].reshape(1, -1).astype(F32), tm)
    out = _ffn(h1, w_ffn_gate[0].astype(BF16), w_ffn_up[0].astype(BF16), w_ffn_down[0].astype(BF16),
               ln2_g[0].reshape(1, -1).astype(F32), ln2_b[0].reshape(1, -1).astype(F32), tm)
    return out.reshape(B, S, D_MODEL)
```

```python
import functools

import numpy as np
import jax
import jax.numpy as jnp
from jax import lax
from jax.experimental import pallas as pl
from jax.experimental.pallas import tpu as pltpu

F32 = jnp.float32
BF16 = jnp.bfloat16

D_MODEL = 1024
HG_HEADS = 8
HG_DK = 128
HG_DV = 128
HG_W = HG_HEADS * HG_DK
HG_CHUNK = 64
NSA_HEADS = 16
NSA_KV = 4
NSA_G = 4
NSA_D = 64
NSA_QW = NSA_HEADS * NSA_D
NSA_KVW = NSA_KV * NSA_D
CMP_LEN = 32
CMP_STRIDE = 16
CMP_HIDDEN = 256
SEL_BLOCK = 64
SEL_TOPN = 8
SEL_FORCE = 1000.0
WINDOW = 512
ROPE_THETA = 10000.0
DEPTH = 1
DN_ALPHA = (2.0 * DEPTH) ** 0.25
LN_EPS = 1e-5
RMS_EPS = 1e-6
LOG2E = 1.4426950408889634
NEG = -1e30

LANES = 128
SLOT_W = NSA_KV * LANES
VSLOT = 2 * LANES
VMEM_LIMIT = 56 * 1024 * 1024


def _dot(a, b):
    return jnp.dot(a, b, preferred_element_type=F32)


def _dot_nt(a, b):
    return lax.dot_general(a, b, (((1,), (1,)), ((), ())), preferred_element_type=F32)


def _dot_tn(a, b):
    return lax.dot_general(a, b, (((0,), (0,)), ((), ())), preferred_element_type=F32)


def _sigmoid(x):
    return 1.0 / (1.0 + jnp.exp(-x))


def _layer_norm(x, g, b):
    mu = jnp.mean(x, -1, keepdims=True)
    xc = x - mu
    var = jnp.mean(xc * xc, -1, keepdims=True)
    return xc * lax.rsqrt(var + LN_EPS) * g + b


def _params(sem):
    return pltpu.CompilerParams(dimension_semantics=sem, vmem_limit_bytes=VMEM_LIMIT)


def _const_spec(shape):
    nd = len(shape)
    return pl.BlockSpec(shape, lambda *_: (0,) * nd)


def _hg_proj_kernel(x_ref, w_ref, q_ref, f_ref, i_ref, g_ref):
    x = x_ref[...].astype(BF16)
    q_ref[...] = _dot(x, w_ref[:, 0:HG_W]).astype(BF16)
    f_ref[...] = _dot(x, w_ref[:, HG_W:2 * HG_W])
    i_ref[...] = _dot(x, w_ref[:, 2 * HG_W:3 * HG_W]).astype(BF16)
    g = _dot(x, w_ref[:, 3 * HG_W:4 * HG_W])
    g_ref[...] = (g * _sigmoid(g)).astype(BF16)


def _hg_proj(x2, w_hg, tm):
    T = x2.shape[0]
    row = lambda i: (i, 0)
    out_spec = pl.BlockSpec((tm, HG_W), row)
    return pl.pallas_call(
        _hg_proj_kernel,
        grid=(T // tm,),
        in_specs=[pl.BlockSpec((tm, D_MODEL), row), _const_spec((D_MODEL, 4 * HG_W))],
        out_specs=[out_spec] * 4,
        out_shape=[jax.ShapeDtypeStruct((T, HG_W), BF16), jax.ShapeDtypeStruct((T, HG_W), F32),
                   jax.ShapeDtypeStruct((T, HG_W), BF16), jax.ShapeDtypeStruct((T, HG_W), BF16)],
        compiler_params=_params(("parallel",)),
        name="hg_proj",
    )(x2, w_hg)


_NQ = NSA_QW
_NKC = NSA_KVW
_NSL = SLOT_W
_O_K = _NQ
_O_V = _O_K + 3 * _NKC
_O_GT = _O_V + 3 * _NKC
_NSA_W = _O_GT + LANES


def _nsa_proj_kernel(x_ref, w_ref, cos_ref, sin_ref, e_ref,
                     q_ref, kc_ref, ks_ref, kw_ref, vc_ref, vs_ref, vw_ref, gt_ref):
    x = x_ref[...].astype(BF16)
    cos = cos_ref[...]
    sin = sin_ref[...]
    half = NSA_D // 2

    def rope(y):
        w = y.shape[1]
        reps = w // LANES
        lane = lax.broadcasted_iota(jnp.int32, y.shape, 1)
        fwd = pltpu.roll(y, w - half, 1)
        bwd = pltpu.roll(y, half, 1)
        rot = jnp.where((lane % NSA_D) < half, fwd, bwd)
        return y * jnp.tile(cos, (1, reps)) + rot * jnp.tile(sin, (1, reps))

    yq = rope(_dot(x, w_ref[:, 0:_NQ]))
    q_ref[...] = (yq * (NSA_D ** -0.5 * LOG2E)).astype(BF16)

    def slots(y):
        left = lax.broadcasted_iota(jnp.int32, (y.shape[0], LANES), 1) < NSA_D
        out = []
        for c in range(y.shape[1] // LANES):
            two = y[:, c * LANES:(c + 1) * LANES]
            out += [jnp.where(left, two, 0.0), jnp.where(left, pltpu.roll(two, NSA_D, 1), 0.0)]
        return out

    def per_head(y, o_ref):
        for h, slab in enumerate(slots(y)):
            o_ref[h] = slab

    def slot_row(y):
        return jnp.concatenate(slots(y), axis=1).astype(BF16)

    yk = rope(_dot(x, w_ref[:, _O_K:_O_K + 3 * _NKC]))
    per_head(yk[:, 0:_NKC], kc_ref)
    ks_ref[...] = slot_row(yk[:, _NKC:2 * _NKC]) + e_ref[...]
    kw_ref[...] = slot_row(yk[:, 2 * _NKC:3 * _NKC])
    yv = _dot(x, w_ref[:, _O_V:_O_V + 3 * _NKC])
    per_head(yv[:, 0:_NKC], vc_ref)
    vs_ref[...] = slot_row(yv[:, _NKC:2 * _NKC])
    vw_ref[...] = slot_row(yv[:, 2 * _NKC:3 * _NKC])
    gates = _sigmoid(_dot(x, w_ref[:, _O_GT:_O_GT + LANES]))
    g_hi = gates.astype(BF16).astype(F32)
    lane = lax.broadcasted_iota(jnp.int32, gates.shape, 1)
    gt_ref[...] = jnp.where(lane < LANES // 2, g_hi, pltpu.roll(gates - g_hi, LANES // 2, 1)).astype(BF16)


def _nsa_proj(x2, w_nsa, cos_t, sin_t, e_tab, tm, S):
    T = x2.shape[0]
    row = lambda i: (i, 0)
    tiles_per_seq = S // tm
    tok = lambda w: (pl.BlockSpec((tm, w), row), jax.ShapeDtypeStruct((T, w), BF16))
    heads = (pl.BlockSpec((NSA_KV, tm, LANES), lambda i: (0, i, 0)),
             jax.ShapeDtypeStruct((NSA_KV, T, LANES), F32))
    outs = [tok(_NQ), heads, tok(_NSL), tok(_NSL), heads, tok(_NSL), tok(_NSL), tok(LANES)]
    return pl.pallas_call(
        _nsa_proj_kernel,
        grid=(T // tm,),
        in_specs=[pl.BlockSpec((tm, D_MODEL), row), _const_spec((D_MODEL, _NSA_W)),
                  pl.BlockSpec((tm, LANES), row), pl.BlockSpec((tm, LANES), row),
                  pl.BlockSpec((tm, _NSL), lambda i: (i % tiles_per_seq, 0))],
        out_specs=[spec for spec, _ in outs],
        out_shape=[shape for _, shape in outs],
        compiler_params=_params(("parallel",)),
        name="nsa_proj",
    )(x2, w_nsa, cos_t, sin_t, e_tab)


def _hgrn_kernel(q_ref, f_ref, i_ref, g_ref, lb_ref, ng_ref, o_ref, *, n_chunks, heads, unroll):
    C = HG_CHUNK
    ng = ng_ref[...]
    r = lax.broadcasted_iota(jnp.int32, (C, C), 0)
    c = lax.broadcasted_iota(jnp.int32, (C, C), 1)
    tril = r >= c
    tril_b = tril.astype(BF16)

    def chunk_cumsum(x):
        hi = x.astype(BF16)
        lo = (x - hi.astype(F32)).astype(BF16)
        return _dot(tril_b, hi) + _dot(tril_b, lo)

    def body(cj, states):
        states = list(states)
        items = [(u, h) for u in range(unroll) for h in range(heads)]
        rows = {it: pl.ds(pl.multiple_of((cj * unroll + it[0]) * C, C), C) for it in items}
        cols = {it: slice(it[1] * HG_DK, (it[1] + 1) * HG_DK) for it in items}
        f, eb, q_dec, k_inv, k_end, attn, out = {}, {}, {}, {}, {}, {}, {}
        for it in items:
            lb = lb_ref[:, cols[it]]
            f[it] = lb + (1.0 - lb) * _sigmoid(f_ref[rows[it], cols[it]])
        for it in items:
            eb[it] = jnp.exp(chunk_cumsum(jnp.log(f[it])))
        for it in items:
            q_dec[it] = (q_ref[rows[it], cols[it]].astype(F32) * eb[it]).astype(BF16)
            k_inv_f = (1.0 - f[it]) * (1.0 / eb[it])
            k_inv[it] = k_inv_f.astype(BF16)
            k_end[it] = (k_inv_f * eb[it][C - 1:C, :]).astype(BF16)
        for it in items:
            attn[it] = jnp.where(tril, _dot_nt(q_dec[it], k_inv[it]), 0.0).astype(BF16)
        for it in items:
            h = it[1]
            v = i_ref[rows[it], cols[it]]
            out[it] = _dot(attn[it], v) + _dot_nt(q_dec[it], states[h].astype(BF16))
            states[h] = states[h] * eb[it][C - 1:C, :] + _dot_tn(v, k_end[it])
        for it in items:
            o = out[it]
            on = o * lax.rsqrt(jnp.mean(o * o, -1, keepdims=True) + RMS_EPS) * ng
            o_ref[rows[it], cols[it]] = (on * g_ref[rows[it], cols[it]].astype(F32)).astype(BF16)
        return tuple(states)

    init = tuple(jnp.zeros((HG_DV, HG_DK), F32) for _ in range(heads))
    lax.fori_loop(0, n_chunks // unroll, body, init)


def _hgrn(q, f, i, g, lb, ng, B, S, heads=8, unroll=8):
    T = B * S
    w = heads * HG_DK
    blk = pl.BlockSpec((S, w), lambda b, h: (b, h))
    return pl.pallas_call(
        functools.partial(_hgrn_kernel, n_chunks=S // HG_CHUNK, heads=heads, unroll=unroll),
        grid=(B, HG_HEADS // heads),
        in_specs=[blk, blk, blk, blk,
                  pl.BlockSpec((1, w), lambda b, h: (0, h)),
                  pl.BlockSpec((1, HG_DV), lambda b, h: (0, 0))],
        out_specs=blk,
        out_shape=jax.ShapeDtypeStruct((T, HG_W), BF16),
        compiler_params=_params(("parallel", "parallel")),
        name="hgrn",
    )(q, f, i, g, lb, ng)


def _compress_kernel(tk_ref, tv_ref, pk_ref, w1k_ref, b1k_ref, w2k_ref, b2k_ref,
                     pv_ref, w1v_ref, b1v_ref, w2v_ref, b2v_ref, tab_ref, ko_ref, vo_ref, u_ref, *, n_half):
    half = CMP_STRIDE * LANES

    def mlp(t_ref, pos_ref, w1_ref, b1_ref, w2_ref, b2_ref, o_ref, tab=None):
        w_top = w1_ref[0:half, :]
        w_bot = w1_ref[half:2 * half, :]
        base = _dot(pos_ref[...], w1_ref[...])[0:1, :] + b1_ref[...]
        for h in range(NSA_KV):
            for l in range(CMP_STRIDE):
                u_ref[h * n_half:(h + 1) * n_half, l * LANES:(l + 1) * LANES] = (
                    t_ref[h, pl.ds(l, n_half, stride=CMP_STRIDE), :].astype(BF16))
        u = u_ref[...]
        top = _dot(u, w_top)
        bot = _dot(u, w_bot)
        hids = []
        for h in range(NSA_KV):
            r0 = h * n_half
            bot_next = jnp.concatenate([bot[r0 + 1:r0 + n_half, :], jnp.zeros((1, CMP_HIDDEN), F32)], axis=0)
            hids.append(top[r0:r0 + n_half, :] + bot_next + base)
        hid = jnp.concatenate(hids, axis=0)
        hid = hid * _sigmoid(hid)
        out = _dot(hid.astype(BF16), w2_ref[...]) + b2_ref[...]
        for h in range(NSA_KV):
            out_h = out[h * n_half:(h + 1) * n_half, :]
            o_ref[0, h] = (out_h if tab is None else out_h + tab).astype(BF16)

    mlp(tk_ref, pk_ref, w1k_ref, b1k_ref, w2k_ref, b2k_ref, ko_ref)
    mlp(tv_ref, pv_ref, w1v_ref, b1v_ref, w2v_ref, b2v_ref, vo_ref, tab_ref[...])


def _compress(tk, tv, pk, w1k, b1k, w2k, b2k, pv, w1v, b1v, w2v, b2v, tab, B, S):
    nh = S // CMP_STRIDE
    t_spec = pl.BlockSpec((NSA_KV, S, LANES), lambda b: (0, b, 0))
    o_spec = lambda w: pl.BlockSpec((1, NSA_KV, nh, w), lambda b: (b, 0, 0, 0))
    specs = lambda arrs: [_const_spec(a.shape) for a in arrs]
    return pl.pallas_call(
        functools.partial(_compress_kernel, n_half=nh),
        grid=(B,),
        in_specs=([t_spec, t_spec] + specs((pk, w1k, b1k, w2k, b2k)) + specs((pv, w1v, b1v, w2v, b2v))
                  + [_const_spec(tab.shape)]),
        out_specs=[o_spec(LANES), o_spec(VSLOT)],
        out_shape=[jax.ShapeDtypeStruct((B, NSA_KV, nh, LANES), BF16),
                   jax.ShapeDtypeStruct((B, NSA_KV, nh, VSLOT), BF16)],
        scratch_shapes=[pltpu.VMEM((NSA_KV * nh, CMP_STRIDE * LANES), BF16)],
        compiler_params=_params(("parallel",)),
        name="compress",
    )(tk, tv, pk, w1k, b1k, w2k, b2k, pv, w1v, b1v, w2v, b2v, tab)


_L_OVL = LANES - 32


def _nsa_kernel(q_ref, ks_ref, vs_ref, kw_ref, vw_ref, kc_ref, vc_ref, gt_ref, gx_ref,
                o_ref, sbuf, qpl_ref, qaug_ref, acc_ref, accw_ref, mt_ref, cmp_ref, gexp_ref,
                *, TQ, CK, NW, n_cmp, n_blk, top_n):
    R = NSA_G * TQ
    NH = CK // LANES
    q0 = pl.program_id(1) * TQ
    lane_q = lax.broadcasted_iota(jnp.int32, (TQ, LANES), 1)
    t_q = q0 + lax.broadcasted_iota(jnp.int32, (TQ, 1), 0)

    def add_bias(s, bias):
        return jnp.concatenate([s[g * TQ:(g + 1) * TQ] + bias for g in range(NSA_G)], axis=0)

    def fold_max(s):
        return functools.reduce(jnp.maximum, [s[:, u * LANES:(u + 1) * LANES] for u in range(NH)])

    def lanes_ck(m):
        return jnp.concatenate([m] * NH, axis=1)

    bias_c = jnp.where((lane_q * CMP_STRIDE + (CMP_LEN - 1) <= t_q) & (lane_q < n_cmp), 0.0, NEG)
    key_l = lax.broadcasted_iota(jnp.int32, (TQ, CK), 1)
    n_sel = (q0 + TQ + CK - 1) // CK
    last = n_sel - 1
    bias_diag = jnp.where(last * CK + key_l <= t_q, 0.0, NEG)

    j_blk = lax.broadcasted_iota(jnp.int32, (n_blk, TQ), 0)
    cur = (q0 + lax.broadcasted_iota(jnp.int32, (n_blk, TQ), 1)) // SEL_BLOCK
    blk_ok = j_blk <= cur
    forced = (j_blk == 0) | (j_blk == cur) | (j_blk == cur - 1)

    gexp_ref[...] = _dot(gt_ref[...], gx_ref[...])

    def q_slabs(kvh):
        slabs = []
        for pair in range(NSA_G // 2):
            col = (kvh * NSA_G + 2 * pair) * NSA_D
            x = q_ref[:, col:col + LANES].astype(F32)
            slabs += [x, pltpu.roll(x, NSA_D, 1)]
        return slabs

    for kvh in range(NSA_KV):
        qpl_ref[kvh] = jnp.concatenate(
            [jnp.where(lane_q < NSA_D, x, 0.0).astype(BF16) for x in q_slabs(kvh)], axis=0)

    def chunk(ref, start, kvh, slot=LANES):
        return ref[pl.ds(start, CK), kvh * slot:(kvh + 1) * slot]

    def score_pass(qs_ref, k_ref, ci, slot, bias, m_ref):
        start = pl.multiple_of(ci * CK, CK)
        for kvh in range(NSA_KV):
            s = _dot_nt(qs_ref[kvh], chunk(k_ref, start, kvh))
            if bias is not None:
                s = add_bias(s, bias)
            sbuf[kvh, slot] = s
            m_ref[kvh] = jnp.maximum(m_ref[kvh], fold_max(s))

    def value_pass(v_ref, ci, slot, m_ref, out_ref):
        start = pl.multiple_of(ci * CK, CK)
        for kvh in range(NSA_KV):
            p = jnp.exp2(sbuf[kvh, slot] - lanes_ck(m_ref[kvh]))
            v1 = jnp.concatenate([chunk(v_ref, start, kvh), jnp.ones((CK, LANES), BF16)], axis=1)
            out_ref[kvh] += _dot(p.astype(BF16), v1)

    def begin_max(m_ref):
        for kvh in range(NSA_KV):
            m_ref[kvh] = jnp.full((R, LANES), -3e38, F32)

    def finish_max(m_ref, out_ref):
        for kvh in range(NSA_KV):
            m_ref[kvh] = jnp.broadcast_to(jnp.max(m_ref[kvh], -1, keepdims=True), (R, LANES))
            out_ref[kvh] = jnp.zeros((R, VSLOT), F32)

    def loop(n, body):
        lax.fori_loop(0, n, lambda i, c: (body(i), c)[1], 0)

    def win_chunk(k):
        cw = last - (NW - 1) + k
        return cw, jnp.maximum(cw, 0)

    def win_scores(k):
        cw, cidx = win_chunk(k)
        key = cidx * CK + key_l
        ok = (key <= t_q) & (key > t_q - WINDOW) & (cw >= 0)
        score_pass(qpl_ref, kw_ref, cidx, k, jnp.where(ok, 0.0, NEG), mt_ref)

    KV = range(NSA_KV)
    s_c = [add_bias(_dot_nt(qpl_ref[h], kc_ref[0, h]), bias_c) for h in KV]
    m_c = [jnp.maximum(jnp.max(s, -1, keepdims=True), 0.1 * NEG) for s in s_c]
    acc_c = [_dot(jnp.exp2(s_c[h] - m_c[h]).astype(BF16), vc_ref[0, h]) for h in KV]
    p_n = [a[:, 0:LANES] * (1.0 / jnp.maximum(a[:, LANES:VSLOT], 1e-30)) for a in acc_c]
    for h in KV:
        cmp_ref[h] = p_n[h]
    imp = [functools.reduce(jnp.add, [p[g * TQ:(g + 1) * TQ] for g in range(NSA_G)]).T[_L_OVL:_L_OVL + n_blk, :]
           for p in p_n]
    score = [jnp.where(blk_ok, x + jnp.where(forced, SEL_FORCE, 0.0), -1.0) for x in imp]
    rank = [jnp.zeros((n_blk, TQ), jnp.int32) for _ in KV]
    for i in range(n_blk):
        for h in KV:
            ri = score[h][i:i + 1, :]
            beats = (ri > score[h]) | ((ri == score[h]) & (j_blk > i))
            rank[h] = rank[h] + beats.astype(jnp.int32)
    for h in KV:
        bias_t = jnp.where((rank[h] < top_n) & blk_ok, 0.0, NEG)
        bias_q = jnp.concatenate([jnp.zeros((NSA_D, TQ), F32), bias_t,
                                  jnp.zeros((LANES - NSA_D - n_blk, TQ), F32)], axis=0).T
        qaug_ref[h] = jnp.concatenate(
            [jnp.where(lane_q < NSA_D, x, bias_q).astype(BF16) for x in q_slabs(h)], axis=0)

    sel_scores = lambda ci, bias=None: score_pass(qaug_ref, ks_ref, ci, ci, bias, mt_ref)
    sel_values = lambda ci: value_pass(vs_ref, ci, ci, mt_ref, acc_ref)

    def chunks(n, one):
        loop(n // 4, lambda j: [one(4 * j + u) for u in range(4)])
        loop((n % 4) // 2, lambda j: [one(n - (n % 4) + u) for u in range(2)])
        loop(n % 2, lambda j: one(n - 1))

    begin_max(mt_ref)
    chunks(last, sel_scores)
    sel_scores(last, bias_diag)
    finish_max(mt_ref, acc_ref)
    chunks(n_sel, sel_values)

    begin_max(mt_ref)
    for k in range(NW):
        win_scores(k)
    finish_max(mt_ref, accw_ref)
    for k in range(NW):
        value_pass(vw_ref, win_chunk(k)[1], k, mt_ref, accw_ref)

    left = lane_q < NSA_D
    for kvh in range(NSA_KV):
        for pair in range(NSA_G // 2):
            col = (kvh * NSA_G + 2 * pair) * NSA_D
            r_e = pl.ds(2 * pair * TQ, TQ)
            r_o = pl.ds((2 * pair + 1) * TQ, TQ)

            def packed(ref, lanes):
                return jnp.where(left, ref[kvh, r_e, lanes], pltpu.roll(ref[kvh, r_o, lanes], NSA_D, 1))

            out = packed(cmp_ref, slice(0, LANES)) * gexp_ref[:, col:col + LANES]
            for br, ref in ((1, acc_ref), (2, accw_ref)):
                den = jnp.where(left, ref[kvh, r_e, LANES:VSLOT], ref[kvh, r_o, LANES:VSLOT])
                gate = gexp_ref[:, br * NSA_QW + col:br * NSA_QW + col + LANES]
                out = out + packed(ref, slice(0, LANES)) * (gate / den)
            o_ref[:, col:col + LANES] = out.astype(BF16)


def _window_chunks(S, TQ, CK):
    return max((q0 + TQ - 1) // CK - max(q0 - WINDOW + 1, 0) // CK + 1 for q0 in range(0, S, TQ))


def _nsa(q, ks, vs, kw, vw, kc, vc, gt, B, S, TQ, CK):
    T = B * S
    n_blk = S // SEL_BLOCK
    n_cmp = (S - CMP_LEN) // CMP_STRIDE + 1
    R = NSA_G * TQ
    NW = _window_chunks(S, TQ, CK)
    assert kc.shape[2] == LANES and n_blk <= LANES - _L_OVL and S // CK >= NW
    seq = lambda w: pl.BlockSpec((S, w), lambda b, i: (b, 0))
    qrow = lambda w: pl.BlockSpec((TQ, w), lambda b, i: (b * (S // TQ) + i, 0))
    cmp_spec = lambda w: pl.BlockSpec((1, NSA_KV, LANES, w), lambda b, i: (b, 0, 0, 0))
    kern = functools.partial(_nsa_kernel, TQ=TQ, CK=CK, NW=NW, n_cmp=n_cmp, n_blk=n_blk,
                             top_n=min(SEL_TOPN, n_blk))
    per_head = lambda w, dt: pltpu.VMEM((NSA_KV, R, w), dt)
    gx = jnp.asarray(_gate_expand_table(), BF16)
    return pl.pallas_call(
        kern,
        grid=(B, S // TQ),
        in_specs=[qrow(NSA_QW), seq(SLOT_W), seq(SLOT_W), seq(SLOT_W), seq(SLOT_W),
                  cmp_spec(LANES), cmp_spec(VSLOT), qrow(LANES), _const_spec(gx.shape)],
        out_specs=qrow(NSA_QW),
        out_shape=jax.ShapeDtypeStruct((T, NSA_QW), BF16),
        scratch_shapes=[pltpu.VMEM((NSA_KV, S // CK, R, CK), F32),
                        per_head(LANES, BF16), per_head(LANES, BF16),
                        per_head(VSLOT, F32), per_head(VSLOT, F32),
                        per_head(LANES, F32),
                        per_head(LANES, F32),
                        pltpu.VMEM((TQ, 3 * NSA_QW), F32)],
        compiler_params=_params(("parallel", "parallel")),
        name="nsa",
    )(q, ks, vs, kw, vw, kc, vc, gt, gx)


def _merge_kernel(x_ref, a_ref, b_ref, wga_ref, wgb_ref, wua_ref, wub_ref, wo_ref, g_ref, bb_ref, o_ref):
    x = x_ref[...]
    xb = x.astype(BF16)
    m = (_sigmoid(_dot(xb, wga_ref[...])) * _dot(a_ref[...], wua_ref[...])
         + _sigmoid(_dot(xb, wgb_ref[...])) * _dot(b_ref[...], wub_ref[...]))
    mix = _dot(m.astype(BF16), wo_ref[...])
    o_ref[...] = _layer_norm(DN_ALPHA * x + mix, g_ref[...], bb_ref[...])


def _merge(x2, a, b, wga, wgb, wua, wub, wo, g, bb, tm):
    T = x2.shape[0]
    row = pl.BlockSpec((tm, D_MODEL), lambda i: (i, 0))
    wsp = _const_spec((D_MODEL, D_MODEL))
    vec = _const_spec((1, D_MODEL))
    return pl.pallas_call(
        _merge_kernel,
        grid=(T // tm,),
        in_specs=[row, row, row, wsp, wsp, wsp, wsp, wsp, vec, vec],
        out_specs=row,
        out_shape=jax.ShapeDtypeStruct((T, D_MODEL), F32),
        compiler_params=_params(("parallel",)),
        name="merge",
    )(x2, a, b, wga, wgb, wua, wub, wo, g, bb)


def _ffn_kernel(h_ref, wg_ref, wu_ref, wd_ref, g_ref, b_ref, o_ref):
    h = h_ref[...]
    hb = h.astype(BF16)
    gate = _dot(hb, wg_ref[...])
    act = (gate * _sigmoid(gate) * _dot(hb, wu_ref[...])).astype(BF16)
    ffn = _dot(act, wd_ref[...])
    o_ref[...] = _layer_norm(DN_ALPHA * h + ffn, g_ref[...], b_ref[...])


def _ffn(h, wg, wu, wd, g, b, tm):
    T = h.shape[0]
    d_ff = wg.shape[1]
    row = pl.BlockSpec((tm, D_MODEL), lambda i: (i, 0))
    vec = _const_spec((1, D_MODEL))
    once = pl.Buffered(1)
    return pl.pallas_call(
        _ffn_kernel,
        grid=(T // tm,),
        in_specs=[row,
                  pl.BlockSpec((D_MODEL, d_ff), lambda i: (0, 0), pipeline_mode=once),
                  pl.BlockSpec((D_MODEL, d_ff), lambda i: (0, 0), pipeline_mode=once),
                  pl.BlockSpec((d_ff, D_MODEL), lambda i: (0, 0), pipeline_mode=once),
                  vec, vec],
        out_specs=row,
        out_shape=jax.ShapeDtypeStruct((T, D_MODEL), F32),
        compiler_params=_params(("parallel",)),
        name="ffn",
    )(h, wg, wu, wd, g, b)


def _cmp_value_table(n_half, n_cmp, n_blk):
    cs = np.arange(n_cmp)[:, None] * CMP_STRIDE
    bs = np.arange(n_blk)[None, :] * SEL_BLOCK
    ov = np.minimum(cs + CMP_LEN, bs + SEL_BLOCK) - np.maximum(cs, bs)
    out = np.zeros((n_half, VSLOT), np.float32)
    out[:, LANES:] = 1.0
    out[:n_cmp, _L_OVL:_L_OVL + n_blk] = np.clip(ov, 0, None) / CMP_LEN
    return out


def _gate_expand_table():
    out = np.zeros((LANES, 3 * NSA_QW), np.float32)
    for br in range(3):
        for h in range(NSA_HEADS):
            c = br * NSA_QW + h * NSA_D
            out[br * NSA_HEADS + h, c:c + NSA_D] = 1.0
            out[LANES // 2 + br * NSA_HEADS + h, c:c + NSA_D] = 1.0
    return out


def kernel(x, positions, w_in, hg_lb_logits, hg_norm_g, cmp_k_pos, cmp_k_w1, cmp_k_b1, cmp_k_w2, cmp_k_b2, cmp_v_pos, cmp_v_w1, cmp_v_b1, cmp_v_w2, cmp_v_b2, w_up_hg, w_up_nsa, w_o, ln1_g, ln1_b, w_ffn_gate, w_ffn_up, w_ffn_down, ln2_g, ln2_b):
    B, S, _ = x.shape
    T = B * S
    tm = min(512, S)
    TQ = 128
    CK = min(256, S)
    n_blk = S // SEL_BLOCK
    n_cmp = (S - CMP_LEN) // CMP_STRIDE + 1
    n_half = S // CMP_STRIDE

    lb_table = jnp.cumsum(jax.nn.softmax(hg_lb_logits.astype(F32), axis=0), axis=0)
    x2 = x.reshape(T, D_MODEL)

    w = w_in[0]
    o = 0
    o += 4 * HG_W
    wq = w[:, o:o + NSA_QW]; o += NSA_QW
    kv = []
    for _ in range(6):
        kv.append(w[:, o:o + NSA_KVW]); o += NSA_KVW
    w_kc, w_vc, w_ks, w_vs, w_kw, w_vw = kv
    w_gt = jnp.pad(w[:, o:o + 3 * NSA_HEADS], ((0, 0), (0, LANES - 3 * NSA_HEADS))); o += 3 * NSA_HEADS
    w_ga = w[:, o:o + D_MODEL]; o += D_MODEL
    w_gb = w[:, o:o + D_MODEL]
    w_nsa = jnp.concatenate([wq, w_kc, w_ks, w_kw, w_vc, w_vs, w_vw, w_gt], axis=1).astype(BF16)

    half = NSA_D // 2
    inv_freq = ROPE_THETA ** (-jnp.arange(half, dtype=F32) / half)
    ang = positions.astype(F32).reshape(T, 1) * inv_freq[None, :]
    cos_t = jnp.tile(jnp.cos(ang), (1, LANES // half))
    sin_h = jnp.sin(ang)
    sin_t = jnp.tile(jnp.concatenate([-sin_h, sin_h], axis=1), (1, LANES // NSA_D))

    e_np = np.zeros((S, NSA_KV, LANES), np.float32)
    e_np[np.arange(S), :, NSA_D + np.arange(S) // SEL_BLOCK] = 1.0
    e_tab = jnp.asarray(e_np.reshape(S, SLOT_W), BF16)

    hq, hf, hi, hg = _hg_proj(x2, w.astype(BF16), tm)
    a = _hgrn(hq, hf, hi, hg, lb_table[0:1], hg_norm_g[0:1].astype(F32), B, S)

    q, kc_tok, ks, kw, vc_tok, vs, vw, gt = _nsa_proj(x2, w_nsa, cos_t, sin_t, e_tab, tm, S)

    def slot_rows(m):
        m = m.reshape(CMP_LEN, NSA_D, -1)
        return jnp.pad(m, ((0, 0), (0, LANES - NSA_D), (0, 0))).reshape(CMP_LEN * LANES, -1)

    def pos_rows(p):
        flat = slot_rows(p.reshape(CMP_LEN * NSA_D, 1)).reshape(1, CMP_LEN * LANES)
        return jnp.broadcast_to(flat, (8, CMP_LEN * LANES)).astype(BF16)

    def pad_out(w2, b2, width):
        return (jnp.pad(w2, ((0, 0), (0, width - NSA_D))).astype(BF16),
                jnp.pad(b2, (0, width - NSA_D)).reshape(1, width).astype(F32))

    w2k, b2k = pad_out(cmp_k_w2[0], cmp_k_b2[0], LANES)
    w2v, b2v = pad_out(cmp_v_w2[0], cmp_v_b2[0], VSLOT)
    kc, vc = _compress(
        kc_tok, vc_tok,
        pos_rows(cmp_k_pos[0]), slot_rows(cmp_k_w1[0]).astype(BF16), cmp_k_b1[0].reshape(1, -1).astype(F32),
        w2k, b2k,
        pos_rows(cmp_v_pos[0]), slot_rows(cmp_v_w1[0]).astype(BF16), cmp_v_b1[0].reshape(1, -1).astype(F32),
        w2v, b2v,
        jnp.asarray(_cmp_value_table(n_half, n_cmp, n_blk)), B, S)

    b_out = _nsa(q, ks, vs, kw, vw, kc, vc, gt, B, S, TQ, CK)

    h1 = _merge(x2, a, b_out, w_ga.astype(BF16), w_gb.astype(BF16),
                w_up_hg[0].astype(BF16), w_up_nsa[0].astype(BF16), w_o[0].astype(BF16),
                ln1_g[0].reshape(1, -1).astype(F32), ln1_b[0].reshape(1, -1).astype(F32), tm)
    out = _ffn(h1, w_ffn_gate[0].astype(BF16), w_ffn_up[0].astype(BF16), w_ffn_down[0].astype(BF16),
               ln2_g[0].reshape(1, -1).astype(F32), ln2_b[0].reshape(1, -1).astype(F32), tm)
    return out.reshape(B, S, D_MODEL)
```

```python
import functools

import numpy as np
import jax
import jax.numpy as jnp
from jax import lax
from jax.experimental import pallas as pl
from jax.experimental.pallas import tpu as pltpu

F32 = jnp.float32
BF16 = jnp.bfloat16

D_MODEL = 1024
HG_HEADS = 8
HG_DK = 128
HG_DV = 128
HG_W = HG_HEADS * HG_DK
HG_CHUNK = 64
NSA_HEADS = 16
NSA_KV = 4
NSA_G = 4
NSA_D = 64
NSA_QW = NSA_HEADS * NSA_D
NSA_KVW = NSA_KV * NSA_D
CMP_LEN = 32
CMP_STRIDE = 16
CMP_HIDDEN = 256
SEL_BLOCK = 64
SEL_TOPN = 8
SEL_FORCE = 1000.0
WINDOW = 512
ROPE_THETA = 10000.0
DEPTH = 1
DN_ALPHA = (2.0 * DEPTH) ** 0.25
LN_EPS = 1e-5
RMS_EPS = 1e-6
LOG2E = 1.4426950408889634
NEG = -1e30

LANES = 128
SLOT_W = NSA_KV * LANES
VSLOT = 2 * LANES
VMEM_LIMIT = 56 * 1024 * 1024


def _dot(a, b):
    return jnp.dot(a, b, preferred_element_type=F32)


def _dot_nt(a, b):
    return lax.dot_general(a, b, (((1,), (1,)), ((), ())), preferred_element_type=F32)


def _dot_tn(a, b):
    return lax.dot_general(a, b, (((0,), (0,)), ((), ())), preferred_element_type=F32)


def _sigmoid(x):
    return 1.0 / (1.0 + jnp.exp(-x))


def _layer_norm(x, g, b):
    mu = jnp.mean(x, -1, keepdims=True)
    xc = x - mu
    var = jnp.mean(xc * xc, -1, keepdims=True)
    return xc * lax.rsqrt(var + LN_EPS) * g + b


def _params(sem):
    return pltpu.CompilerParams(dimension_semantics=sem, vmem_limit_bytes=VMEM_LIMIT)


def _const_spec(shape):
    nd = len(shape)
    return pl.BlockSpec(shape, lambda *_: (0,) * nd)


def _hg_proj_kernel(x_ref, w_ref, q_ref, f_ref, i_ref, g_ref):
    x = x_ref[...].astype(BF16)
    q_ref[...] = _dot(x, w_ref[:, 0:HG_W]).astype(BF16)
    f_ref[...] = _dot(x, w_ref[:, HG_W:2 * HG_W])
    i_ref[...] = _dot(x, w_ref[:, 2 * HG_W:3 * HG_W]).astype(BF16)
    g = _dot(x, w_ref[:, 3 * HG_W:4 * HG_W])
    g_ref[...] = (g * _sigmoid(g)).astype(BF16)


def _hg_proj(x2, w_hg, tm):
    T = x2.shape[0]
    row = lambda i: (i, 0)
    out_spec = pl.BlockSpec((tm, HG_W), row)
    return pl.pallas_call(
        _hg_proj_kernel,
        grid=(T // tm,),
        in_specs=[pl.BlockSpec((tm, D_MODEL), row), _const_spec((D_MODEL, 4 * HG_W))],
        out_specs=[out_spec] * 4,
        out_shape=[jax.ShapeDtypeStruct((T, HG_W), BF16), jax.ShapeDtypeStruct((T, HG_W), F32),
                   jax.ShapeDtypeStruct((T, HG_W), BF16), jax.ShapeDtypeStruct((T, HG_W), BF16)],
        compiler_params=_params(("parallel",)),
        name="hg_proj",
    )(x2, w_hg)


_NQ = NSA_QW
_NKC = NSA_KVW
_NSL = SLOT_W
_O_K = _NQ
_O_V = _O_K + 3 * _NKC
_O_GT = _O_V + 3 * _NKC
_NSA_W = _O_GT + LANES


def _nsa_proj_kernel(x_ref, w_ref, cos_ref, sin_ref, e_ref,
                     q_ref, kc_ref, ks_ref, kw_ref, vc_ref, vs_ref, vw_ref, gt_ref):
    x = x_ref[...].astype(BF16)
    cos = cos_ref[...]
    sin = sin_ref[...]
    half = NSA_D // 2

    def rope(y):
        w = y.shape[1]
        reps = w // LANES
        lane = lax.broadcasted_iota(jnp.int32, y.shape, 1)
        fwd = pltpu.roll(y, w - half, 1)
        bwd = pltpu.roll(y, half, 1)
        rot = jnp.where((lane % NSA_D) < half, fwd, bwd)
        return y * jnp.tile(cos, (1, reps)) + rot * jnp.tile(sin, (1, reps))

    yq = rope(_dot(x, w_ref[:, 0:_NQ]))
    q_ref[...] = (yq * (NSA_D ** -0.5 * LOG2E)).astype(BF16)

    def slots(y):
        left = lax.broadcasted_iota(jnp.int32, (y.shape[0], LANES), 1) < NSA_D
        out = []
        for c in range(y.shape[1] // LANES):
            two = y[:, c * LANES:(c + 1) * LANES]
            out += [jnp.where(left, two, 0.0), jnp.where(left, pltpu.roll(two, NSA_D, 1), 0.0)]
        return out

    def head_pairs(y, o_ref):
        for c in range(y.shape[1] // LANES):
            o_ref[c] = y[:, c * LANES:(c + 1) * LANES]

    def slot_row(y):
        return jnp.concatenate(slots(y), axis=1).astype(BF16)

    yk = rope(_dot(x, w_ref[:, _O_K:_O_K + 3 * _NKC]))
    head_pairs(yk[:, 0:_NKC], kc_ref)
    ks_ref[...] = slot_row(yk[:, _NKC:2 * _NKC]) + e_ref[...]
    kw_ref[...] = slot_row(yk[:, 2 * _NKC:3 * _NKC])
    yv = _dot(x, w_ref[:, _O_V:_O_V + 3 * _NKC])
    head_pairs(yv[:, 0:_NKC], vc_ref)
    vs_ref[...] = slot_row(yv[:, _NKC:2 * _NKC])
    vw_ref[...] = slot_row(yv[:, 2 * _NKC:3 * _NKC])
    gates = _sigmoid(_dot(x, w_ref[:, _O_GT:_O_GT + LANES]))
    g_hi = gates.astype(BF16).astype(F32)
    lane = lax.broadcasted_iota(jnp.int32, gates.shape, 1)
    gt_ref[...] = jnp.where(lane < LANES // 2, g_hi, pltpu.roll(gates - g_hi, LANES // 2, 1)).astype(BF16)


def _nsa_proj(x2, w_nsa, cos_t, sin_t, e_tab, tm, S):
    T = x2.shape[0]
    row = lambda i: (i, 0)
    tiles_per_seq = S // tm
    tok = lambda w: (pl.BlockSpec((tm, w), row), jax.ShapeDtypeStruct((T, w), BF16))
    heads = (pl.BlockSpec((NSA_KV // 2, tm, LANES), lambda i: (0, i, 0)),
             jax.ShapeDtypeStruct((NSA_KV // 2, T, LANES), F32))
    outs = [tok(_NQ), heads, tok(_NSL), tok(_NSL), heads, tok(_NSL), tok(_NSL), tok(LANES)]
    return pl.pallas_call(
        _nsa_proj_kernel,
        grid=(T // tm,),
        in_specs=[pl.BlockSpec((tm, D_MODEL), row), _const_spec((D_MODEL, _NSA_W)),
                  pl.BlockSpec((tm, LANES), row), pl.BlockSpec((tm, LANES), row),
                  pl.BlockSpec((tm, _NSL), lambda i: (i % tiles_per_seq, 0))],
        out_specs=[spec for spec, _ in outs],
        out_shape=[shape for _, shape in outs],
        compiler_params=_params(("parallel",)),
        name="nsa_proj",
    )(x2, w_nsa, cos_t, sin_t, e_tab)


def _hgrn_kernel(q_ref, f_ref, i_ref, g_ref, lb_ref, ng_ref, o_ref, *, n_chunks, heads, unroll):
    C = HG_CHUNK
    ng = ng_ref[...]
    r = lax.broadcasted_iota(jnp.int32, (C, C), 0)
    c = lax.broadcasted_iota(jnp.int32, (C, C), 1)
    tril = r >= c
    tril_b = tril.astype(BF16)

    def chunk_cumsum(x):
        hi = x.astype(BF16)
        lo = (x - hi.astype(F32)).astype(BF16)
        return _dot(tril_b, hi) + _dot(tril_b, lo)

    def body(cj, states):
        states = list(states)
        items = [(u, h) for u in range(unroll) for h in range(heads)]
        rows = {it: pl.ds(pl.multiple_of((cj * unroll + it[0]) * C, C), C) for it in items}
        cols = {it: slice(it[1] * HG_DK, (it[1] + 1) * HG_DK) for it in items}
        f, eb, q_dec, k_inv, k_end, attn, out = {}, {}, {}, {}, {}, {}, {}
        for it in items:
            lb = lb_ref[:, cols[it]]
            f[it] = lb + (1.0 - lb) * _sigmoid(f_ref[rows[it], cols[it]])
        for it in items:
            eb[it] = jnp.exp(chunk_cumsum(jnp.log(f[it])))
        for it in items:
            q_dec[it] = (q_ref[rows[it], cols[it]].astype(F32) * eb[it]).astype(BF16)
            k_inv_f = (1.0 - f[it]) * (1.0 / eb[it])
            k_inv[it] = k_inv_f.astype(BF16)
            k_end[it] = (k_inv_f * eb[it][C - 1:C, :]).astype(BF16)
        for it in items:
            attn[it] = jnp.where(tril, _dot_nt(q_dec[it], k_inv[it]), 0.0).astype(BF16)
        for it in items:
            h = it[1]
            v = i_ref[rows[it], cols[it]]
            out[it] = _dot(attn[it], v) + _dot_nt(q_dec[it], states[h].astype(BF16))
            states[h] = states[h] * eb[it][C - 1:C, :] + _dot_tn(v, k_end[it])
        for it in items:
            o = out[it]
            on = o * lax.rsqrt(jnp.mean(o * o, -1, keepdims=True) + RMS_EPS) * ng
            o_ref[rows[it], cols[it]] = (on * g_ref[rows[it], cols[it]].astype(F32)).astype(BF16)
        return tuple(states)

    init = tuple(jnp.zeros((HG_DV, HG_DK), F32) for _ in range(heads))
    lax.fori_loop(0, n_chunks // unroll, body, init)


def _hgrn(q, f, i, g, lb, ng, B, S, heads=8, unroll=8):
    T = B * S
    w = heads * HG_DK
    blk = pl.BlockSpec((S, w), lambda b, h: (b, h))
    return pl.pallas_call(
        functools.partial(_hgrn_kernel, n_chunks=S // HG_CHUNK, heads=heads, unroll=unroll),
        grid=(B, HG_HEADS // heads),
        in_specs=[blk, blk, blk, blk,
                  pl.BlockSpec((1, w), lambda b, h: (0, h)),
                  pl.BlockSpec((1, HG_DV), lambda b, h: (0, 0))],
        out_specs=blk,
        out_shape=jax.ShapeDtypeStruct((T, HG_W), BF16),
        compiler_params=_params(("parallel", "parallel")),
        name="hgrn",
    )(q, f, i, g, lb, ng)


def _compress_kernel(tk_ref, tv_ref, pk_ref, w1k_ref, b1k_ref, w2k_ref, b2k_ref,
                     pv_ref, w1v_ref, b1v_ref, w2v_ref, b2v_ref, tab_ref, ko_ref, vo_ref, u_ref, *, n_half):
    half = CMP_STRIDE * LANES
    pairs = NSA_KV // 2

    def mlp(t_ref, pos_ref, w1_ref, b1_ref, w2_ref, b2_ref, o_ref, tab=None):
        base = _dot(pos_ref[...], w1_ref[0])[0:1, :] + b1_ref[...]
        for p in range(pairs):
            for l in range(CMP_STRIDE):
                u_ref[p * n_half:(p + 1) * n_half, l * LANES:(l + 1) * LANES] = (
                    t_ref[p, pl.ds(l, n_half, stride=CMP_STRIDE), :].astype(BF16))
        u = u_ref[...]
        top = [_dot(u, w1_ref[par, 0:half, :]) for par in range(2)]
        bot = [_dot(u, w1_ref[par, half:2 * half, :]) for par in range(2)]
        hids = []
        for h in range(NSA_KV):
            r0 = (h // 2) * n_half
            t_h, b_h = top[h % 2], bot[h % 2]
            bot_next = jnp.concatenate([b_h[r0 + 1:r0 + n_half, :], jnp.zeros((1, CMP_HIDDEN), F32)], axis=0)
            hids.append(t_h[r0:r0 + n_half, :] + bot_next + base)
        hid = jnp.concatenate(hids, axis=0)
        hid = hid * _sigmoid(hid)
        out = _dot(hid.astype(BF16), w2_ref[...]) + b2_ref[...]
        for h in range(NSA_KV):
            out_h = out[h * n_half:(h + 1) * n_half, :]
            o_ref[0, h] = (out_h if tab is None else out_h + tab).astype(BF16)

    mlp(tk_ref, pk_ref, w1k_ref, b1k_ref, w2k_ref, b2k_ref, ko_ref)
    mlp(tv_ref, pv_ref, w1v_ref, b1v_ref, w2v_ref, b2v_ref, vo_ref, tab_ref[...])


def _compress(tk, tv, pk, w1k, b1k, w2k, b2k, pv, w1v, b1v, w2v, b2v, tab, B, S):
    nh = S // CMP_STRIDE
    t_spec = pl.BlockSpec((NSA_KV // 2, S, LANES), lambda b: (0, b, 0))
    o_spec = lambda w: pl.BlockSpec((1, NSA_KV, nh, w), lambda b: (b, 0, 0, 0))
    specs = lambda arrs: [_const_spec(a.shape) for a in arrs]
    return pl.pallas_call(
        functools.partial(_compress_kernel, n_half=nh),
        grid=(B,),
        in_specs=([t_spec, t_spec] + specs((pk, w1k, b1k, w2k, b2k)) + specs((pv, w1v, b1v, w2v, b2v))
                  + [_const_spec(tab.shape)]),
        out_specs=[o_spec(LANES), o_spec(VSLOT)],
        out_shape=[jax.ShapeDtypeStruct((B, NSA_KV, nh, LANES), BF16),
                   jax.ShapeDtypeStruct((B, NSA_KV, nh, VSLOT), BF16)],
        scratch_shapes=[pltpu.VMEM((NSA_KV // 2 * nh, CMP_STRIDE * LANES), BF16)],
        compiler_params=_params(("parallel",)),
        name="compress",
    )(tk, tv, pk, w1k, b1k, w2k, b2k, pv, w1v, b1v, w2v, b2v, tab)


_L_OVL = LANES - 32


def _nsa_kernel(q_ref, ks_ref, vs_ref, kw_ref, vw_ref, kc_ref, vc_ref, gt_ref, gx_ref,
                o_ref, sbuf, qpl_ref, qaug_ref, acc_ref, accw_ref, mt_ref, cmp_ref, gexp_ref,
                *, TQ, CK, NW, n_cmp, n_blk, top_n):
    R = NSA_G * TQ
    NH = CK // LANES
    q0 = pl.program_id(1) * TQ
    lane_q = lax.broadcasted_iota(jnp.int32, (TQ, LANES), 1)
    t_q = q0 + lax.broadcasted_iota(jnp.int32, (TQ, 1), 0)

    def add_bias(s, bias):
        return jnp.concatenate([s[g * TQ:(g + 1) * TQ] + bias for g in range(NSA_G)], axis=0)

    def fold_max(s):
        return functools.reduce(jnp.maximum, [s[:, u * LANES:(u + 1) * LANES] for u in range(NH)])

    def lanes_ck(m):
        return jnp.concatenate([m] * NH, axis=1)

    bias_c = jnp.where((lane_q * CMP_STRIDE + (CMP_LEN - 1) <= t_q) & (lane_q < n_cmp), 0.0, NEG)
    key_l = lax.broadcasted_iota(jnp.int32, (TQ, CK), 1)
    n_sel = (q0 + TQ + CK - 1) // CK
    last = n_sel - 1
    bias_diag = jnp.where(last * CK + key_l <= t_q, 0.0, NEG)

    j_blk = lax.broadcasted_iota(jnp.int32, (n_blk, TQ), 0)
    cur = (q0 + lax.broadcasted_iota(jnp.int32, (n_blk, TQ), 1)) // SEL_BLOCK
    blk_ok = j_blk <= cur
    forced = (j_blk == 0) | (j_blk == cur) | (j_blk == cur - 1)

    gexp_ref[...] = _dot(gt_ref[...], gx_ref[...])

    def q_slabs(kvh):
        slabs = []
        for pair in range(NSA_G // 2):
            col = (kvh * NSA_G + 2 * pair) * NSA_D
            x = q_ref[:, col:col + LANES].astype(F32)
            slabs += [x, pltpu.roll(x, NSA_D, 1)]
        return slabs

    for kvh in range(NSA_KV):
        qpl_ref[kvh] = jnp.concatenate(
            [jnp.where(lane_q < NSA_D, x, 0.0).astype(BF16) for x in q_slabs(kvh)], axis=0)

    def chunk(ref, start, kvh, slot=LANES):
        return ref[pl.ds(start, CK), kvh * slot:(kvh + 1) * slot]

    def score_pass(qs_ref, k_ref, ci, slot, bias, m_ref):
        start = pl.multiple_of(ci * CK, CK)
        for kvh in range(NSA_KV):
            s = _dot_nt(qs_ref[kvh], chunk(k_ref, start, kvh))
            if bias is not None:
                s = add_bias(s, bias)
            sbuf[kvh, slot] = s
            m_ref[kvh] = jnp.maximum(m_ref[kvh], fold_max(s))

    def value_pass(v_ref, ci, slot, m_ref, out_ref):
        start = pl.multiple_of(ci * CK, CK)
        for kvh in range(NSA_KV):
            p = jnp.exp2(sbuf[kvh, slot] - lanes_ck(m_ref[kvh]))
            v1 = jnp.concatenate([chunk(v_ref, start, kvh), jnp.ones((CK, LANES), BF16)], axis=1)
            out_ref[kvh] += _dot(p.astype(BF16), v1)

    def begin_max(m_ref):
        for kvh in range(NSA_KV):
            m_ref[kvh] = jnp.full((R, LANES), -3e38, F32)

    def finish_max(m_ref, out_ref):
        for kvh in range(NSA_KV):
            m_ref[kvh] = jnp.broadcast_to(jnp.max(m_ref[kvh], -1, keepdims=True), (R, LANES))
            out_ref[kvh] = jnp.zeros((R, VSLOT), F32)

    def loop(n, body):
        lax.fori_loop(0, n, lambda i, c: (body(i), c)[1], 0)

    def win_chunk(k):
        cw = last - (NW - 1) + k
        return cw, jnp.maximum(cw, 0)

    def win_scores(k):
        cw, cidx = win_chunk(k)
        key = cidx * CK + key_l
        ok = (key <= t_q) & (key > t_q - WINDOW) & (cw >= 0)
        score_pass(qpl_ref, kw_ref, cidx, k, jnp.where(ok, 0.0, NEG), mt_ref)

    KV = range(NSA_KV)
    s_c = [add_bias(_dot_nt(qpl_ref[h], kc_ref[0, h]), bias_c) for h in KV]
    m_c = [jnp.maximum(jnp.max(s, -1, keepdims=True), 0.1 * NEG) for s in s_c]
    acc_c = [_dot(jnp.exp2(s_c[h] - m_c[h]).astype(BF16), vc_ref[0, h]) for h in KV]
    p_n = [a[:, 0:LANES] * (1.0 / jnp.maximum(a[:, LANES:VSLOT], 1e-30)) for a in acc_c]
    for h in KV:
        cmp_ref[h] = p_n[h]
    imp = [functools.reduce(jnp.add, [p[g * TQ:(g + 1) * TQ] for g in range(NSA_G)]).T[_L_OVL:_L_OVL + n_blk, :]
           for p in p_n]
    score = [jnp.where(blk_ok, x + jnp.where(forced, SEL_FORCE, 0.0), -1.0) for x in imp]
    rank = [jnp.zeros((n_blk, TQ), jnp.int32) for _ in KV]
    for i in range(n_blk):
        for h in KV:
            ri = score[h][i:i + 1, :]
            beats = (ri > score[h]) | ((ri == score[h]) & (j_blk > i))
            rank[h] = rank[h] + beats.astype(jnp.int32)
    for h in KV:
        bias_t = jnp.where((rank[h] < top_n) & blk_ok, 0.0, NEG)
        bias_q = jnp.concatenate([jnp.zeros((NSA_D, TQ), F32), bias_t,
                                  jnp.zeros((LANES - NSA_D - n_blk, TQ), F32)], axis=0).T
        qaug_ref[h] = jnp.concatenate(
            [jnp.where(lane_q < NSA_D, x, bias_q).astype(BF16) for x in q_slabs(h)], axis=0)

    sel_scores = lambda ci, bias=None: score_pass(qaug_ref, ks_ref, ci, ci, bias, mt_ref)
    sel_values = lambda ci: value_pass(vs_ref, ci, ci, mt_ref, acc_ref)

    def chunks(n, one):
        loop(n // 4, lambda j: [one(4 * j + u) for u in range(4)])
        loop((n % 4) // 2, lambda j: [one(n - (n % 4) + u) for u in range(2)])
        loop(n % 2, lambda j: one(n - 1))

    begin_max(mt_ref)
    chunks(last, sel_scores)
    sel_scores(last, bias_diag)
    finish_max(mt_ref, acc_ref)
    chunks(n_sel, sel_values)

    begin_max(mt_ref)
    for k in range(NW):
        win_scores(k)
    finish_max(mt_ref, accw_ref)
    for k in range(NW):
        value_pass(vw_ref, win_chunk(k)[1], k, mt_ref, accw_ref)

    left = lane_q < NSA_D
    for kvh in range(NSA_KV):
        for pair in range(NSA_G // 2):
            col = (kvh * NSA_G + 2 * pair) * NSA_D
            r_e = pl.ds(2 * pair * TQ, TQ)
            r_o = pl.ds((2 * pair + 1) * TQ, TQ)

            def packed(ref, lanes):
                return jnp.where(left, ref[kvh, r_e, lanes], pltpu.roll(ref[kvh, r_o, lanes], NSA_D, 1))

            out = packed(cmp_ref, slice(0, LANES)) * gexp_ref[:, col:col + LANES]
            for br, ref in ((1, acc_ref), (2, accw_ref)):
                den = jnp.where(left, ref[kvh, r_e, LANES:VSLOT], ref[kvh, r_o, LANES:VSLOT])
                gate = gexp_ref[:, br * NSA_QW + col:br * NSA_QW + col + LANES]
                out = out + packed(ref, slice(0, LANES)) * (gate / den)
            o_ref[:, col:col + LANES] = out.astype(BF16)


def _window_chunks(S, TQ, CK):
    return max((q0 + TQ - 1) // CK - max(q0 - WINDOW + 1, 0) // CK + 1 for q0 in range(0, S, TQ))


def _nsa(q, ks, vs, kw, vw, kc, vc, gt, B, S, TQ, CK):
    T = B * S
    n_blk = S // SEL_BLOCK
    n_cmp = (S - CMP_LEN) // CMP_STRIDE + 1
    R = NSA_G * TQ
    NW = _window_chunks(S, TQ, CK)
    assert kc.shape[2] == LANES and n_blk <= LANES - _L_OVL and S // CK >= NW
    seq = lambda w: pl.BlockSpec((S, w), lambda b, i: (b, 0))
    qrow = lambda w: pl.BlockSpec((TQ, w), lambda b, i: (b * (S // TQ) + i, 0))
    cmp_spec = lambda w: pl.BlockSpec((1, NSA_KV, LANES, w), lambda b, i: (b, 0, 0, 0))
    kern = functools.partial(_nsa_kernel, TQ=TQ, CK=CK, NW=NW, n_cmp=n_cmp, n_blk=n_blk,
                             top_n=min(SEL_TOPN, n_blk))
    per_head = lambda w, dt: pltpu.VMEM((NSA_KV, R, w), dt)
    gx = jnp.asarray(_gate_expand_table(), BF16)
    return pl.pallas_call(
        kern,
        grid=(B, S // TQ),
        in_specs=[qrow(NSA_QW), seq(SLOT_W), seq(SLOT_W), seq(SLOT_W), seq(SLOT_W),
                  cmp_spec(LANES), cmp_spec(VSLOT), qrow(LANES), _const_spec(gx.shape)],
        out_specs=qrow(NSA_QW),
        out_shape=jax.ShapeDtypeStruct((T, NSA_QW), BF16),
        scratch_shapes=[pltpu.VMEM((NSA_KV, S // CK, R, CK), F32),
                        per_head(LANES, BF16), per_head(LANES, BF16),
                        per_head(VSLOT, F32), per_head(VSLOT, F32),
                        per_head(LANES, F32),
                        per_head(LANES, F32),
                        pltpu.VMEM((TQ, 3 * NSA_QW), F32)],
        compiler_params=_params(("parallel", "parallel")),
        name="nsa",
    )(q, ks, vs, kw, vw, kc, vc, gt, gx)


def _merge_kernel(x_ref, a_ref, b_ref, wga_ref, wgb_ref, wua_ref, wub_ref, wo_ref, g_ref, bb_ref, o_ref):
    x = x_ref[...]
    xb = x.astype(BF16)
    m = (_sigmoid(_dot(xb, wga_ref[...])) * _dot(a_ref[...], wua_ref[...])
         + _sigmoid(_dot(xb, wgb_ref[...])) * _dot(b_ref[...], wub_ref[...]))
    mix = _dot(m.astype(BF16), wo_ref[...])
    o_ref[...] = _layer_norm(DN_ALPHA * x + mix, g_ref[...], bb_ref[...])


def _merge(x2, a, b, wga, wgb, wua, wub, wo, g, bb, tm):
    T = x2.shape[0]
    row = pl.BlockSpec((tm, D_MODEL), lambda i: (i, 0))
    wsp = _const_spec((D_MODEL, D_MODEL))
    vec = _const_spec((1, D_MODEL))
    return pl.pallas_call(
        _merge_kernel,
        grid=(T // tm,),
        in_specs=[row, row, row, wsp, wsp, wsp, wsp, wsp, vec, vec],
        out_specs=row,
        out_shape=jax.ShapeDtypeStruct((T, D_MODEL), F32),
        compiler_params=_params(("parallel",)),
        name="merge",
    )(x2, a, b, wga, wgb, wua, wub, wo, g, bb)


def _ffn_kernel(h_ref, wg_ref, wu_ref, wd_ref, g_ref, b_ref, o_ref):
    h = h_ref[...]
    hb = h.astype(BF16)
    gate = _dot(hb, wg_ref[...])
    act = (gate * _sigmoid(gate) * _dot(hb, wu_ref[...])).astype(BF16)
    ffn = _dot(act, wd_ref[...])
    o_ref[...] = _layer_norm(DN_ALPHA * h + ffn, g_ref[...], b_ref[...])


def _ffn(h, wg, wu, wd, g, b, tm):
    T = h.shape[0]
    d_ff = wg.shape[1]
    row = pl.BlockSpec((tm, D_MODEL), lambda i: (i, 0))
    vec = _const_spec((1, D_MODEL))
    once = pl.Buffered(1)
    return pl.pallas_call(
        _ffn_kernel,
        grid=(T // tm,),
        in_specs=[row,
                  pl.BlockSpec((D_MODEL, d_ff), lambda i: (0, 0), pipeline_mode=once),
                  pl.BlockSpec((D_MODEL, d_ff), lambda i: (0, 0), pipeline_mode=once),
                  pl.BlockSpec((d_ff, D_MODEL), lambda i: (0, 0), pipeline_mode=once),
                  vec, vec],
        out_specs=row,
        out_shape=jax.ShapeDtypeStruct((T, D_MODEL), F32),
        compiler_params=_params(("parallel",)),
        name="ffn",
    )(h, wg, wu, wd, g, b)


def _cmp_value_table(n_half, n_cmp, n_blk):
    cs = np.arange(n_cmp)[:, None] * CMP_STRIDE
    bs = np.arange(n_blk)[None, :] * SEL_BLOCK
    ov = np.minimum(cs + CMP_LEN, bs + SEL_BLOCK) - np.maximum(cs, bs)
    out = np.zeros((n_half, VSLOT), np.float32)
    out[:, LANES:] = 1.0
    out[:n_cmp, _L_OVL:_L_OVL + n_blk] = np.clip(ov, 0, None) / CMP_LEN
    return out


def _gate_expand_table():
    out = np.zeros((LANES, 3 * NSA_QW), np.float32)
    for br in range(3):
        for h in range(NSA_HEADS):
            c = br * NSA_QW + h * NSA_D
            out[br * NSA_HEADS + h, c:c + NSA_D] = 1.0
            out[LANES // 2 + br * NSA_HEADS + h, c:c + NSA_D] = 1.0
    return out


def kernel(x, positions, w_in, hg_lb_logits, hg_norm_g, cmp_k_pos, cmp_k_w1, cmp_k_b1, cmp_k_w2, cmp_k_b2, cmp_v_pos, cmp_v_w1, cmp_v_b1, cmp_v_w2, cmp_v_b2, w_up_hg, w_up_nsa, w_o, ln1_g, ln1_b, w_ffn_gate, w_ffn_up, w_ffn_down, ln2_g, ln2_b):
    B, S, _ = x.shape
    T = B * S
    tm = min(512, S)
    TQ = 128
    CK = min(256, S)
    n_blk = S // SEL_BLOCK
    n_cmp = (S - CMP_LEN) // CMP_STRIDE + 1
    n_half = S // CMP_STRIDE

    lb_table = jnp.cumsum(jax.nn.softmax(hg_lb_logits.astype(F32), axis=0), axis=0)
    x2 = x.reshape(T, D_MODEL)

    w = w_in[0]
    o = 0
    o += 4 * HG_W
    wq = w[:, o:o + NSA_QW]; o += NSA_QW
    kv = []
    for _ in range(6):
        kv.append(w[:, o:o + NSA_KVW]); o += NSA_KVW
    w_kc, w_vc, w_ks, w_vs, w_kw, w_vw = kv
    w_gt = jnp.pad(w[:, o:o + 3 * NSA_HEADS], ((0, 0), (0, LANES - 3 * NSA_HEADS))); o += 3 * NSA_HEADS
    w_ga = w[:, o:o + D_MODEL]; o += D_MODEL
    w_gb = w[:, o:o + D_MODEL]
    w_nsa = jnp.concatenate([wq, w_kc, w_ks, w_kw, w_vc, w_vs, w_vw, w_gt], axis=1).astype(BF16)

    half = NSA_D // 2
    inv_freq = ROPE_THETA ** (-jnp.arange(half, dtype=F32) / half)
    ang = positions.astype(F32).reshape(T, 1) * inv_freq[None, :]
    cos_t = jnp.tile(jnp.cos(ang), (1, LANES // half))
    sin_h = jnp.sin(ang)
    sin_t = jnp.tile(jnp.concatenate([-sin_h, sin_h], axis=1), (1, LANES // NSA_D))

    e_np = np.zeros((S, NSA_KV, LANES), np.float32)
    e_np[np.arange(S), :, NSA_D + np.arange(S) // SEL_BLOCK] = 1.0
    e_tab = jnp.asarray(e_np.reshape(S, SLOT_W), BF16)

    hq, hf, hi, hg = _hg_proj(x2, w.astype(BF16), tm)
    a = _hgrn(hq, hf, hi, hg, lb_table[0:1], hg_norm_g[0:1].astype(F32), B, S)

    q, kc_tok, ks, kw, vc_tok, vs, vw, gt = _nsa_proj(x2, w_nsa, cos_t, sin_t, e_tab, tm, S)

    def slab_rows(m, parity):
        m = m.reshape(CMP_LEN, NSA_D, -1)
        lo = parity * NSA_D
        return jnp.pad(m, ((0, 0), (lo, LANES - NSA_D - lo), (0, 0))).reshape(CMP_LEN * LANES, -1)

    def w1_pair(w1):
        return jnp.stack([slab_rows(w1, 0), slab_rows(w1, 1)]).astype(BF16)

    def pos_rows(p):
        flat = slab_rows(p.reshape(CMP_LEN * NSA_D, 1), 0).reshape(1, CMP_LEN * LANES)
        return jnp.broadcast_to(flat, (8, CMP_LEN * LANES)).astype(BF16)

    def pad_out(w2, b2, width):
        return (jnp.pad(w2, ((0, 0), (0, width - NSA_D))).astype(BF16),
                jnp.pad(b2, (0, width - NSA_D)).reshape(1, width).astype(F32))

    w2k, b2k = pad_out(cmp_k_w2[0], cmp_k_b2[0], LANES)
    w2v, b2v = pad_out(cmp_v_w2[0], cmp_v_b2[0], VSLOT)
    kc, vc = _compress(
        kc_tok, vc_tok,
        pos_rows(cmp_k_pos[0]), w1_pair(cmp_k_w1[0]), cmp_k_b1[0].reshape(1, -1).astype(F32),
        w2k, b2k,
        pos_rows(cmp_v_pos[0]), w1_pair(cmp_v_w1[0]), cmp_v_b1[0].reshape(1, -1).astype(F32),
        w2v, b2v,
        jnp.asarray(_cmp_value_table(n_half, n_cmp, n_blk)), B, S)

    b_out = _nsa(q, ks, vs, kw, vw, kc, vc, gt, B, S, TQ, CK)

    h1 = _merge(x2, a, b_out, w_ga.astype(BF16), w_gb.astype(BF16),
                w_up_hg[0].astype(BF16), w_up_nsa[0].astype(BF16), w_o[0].astype(BF16),
                ln1_g[0].reshape(1, -1).astype(F32), ln1_b[0].reshape(1, -1).astype(F32), tm)
    out = _ffn(h1, w_ffn_gate[0].astype(BF16), w_ffn_up[0].astype(BF16), w_ffn_down[0].astype(BF16),
               ln2_g[0].reshape(1, -1).astype(F32), ln2_b[0].reshape(1, -1).astype(F32), tm)
    return out.reshape(B, S, D_MODEL)
```

```python
import functools

import numpy as np
import jax
import jax.numpy as jnp
from jax import lax
from jax.experimental import pallas as pl
from jax.experimental.pallas import tpu as pltpu

F32 = jnp.float32
BF16 = jnp.bfloat16

D_MODEL = 1024
HG_HEADS = 8
HG_DK = 128
HG_DV = 128
HG_W = HG_HEADS * HG_DK
HG_CHUNK = 64
NSA_HEADS = 16
NSA_KV = 4
NSA_G = 4
NSA_D = 64
NSA_QW = NSA_HEADS * NSA_D
NSA_KVW = NSA_KV * NSA_D
CMP_LEN = 32
CMP_STRIDE = 16
CMP_HIDDEN = 256
SEL_BLOCK = 64
SEL_TOPN = 8
SEL_FORCE = 1000.0
WINDOW = 512
ROPE_THETA = 10000.0
DEPTH = 1
DN_ALPHA = (2.0 * DEPTH) ** 0.25
LN_EPS = 1e-5
RMS_EPS = 1e-6
LOG2E = 1.4426950408889634
NEG = -1e30

LANES = 128
SLOT_W = NSA_KV * LANES
VSLOT = 2 * LANES
VMEM_LIMIT = 56 * 1024 * 1024


def _dot(a, b):
    return jnp.dot(a, b, preferred_element_type=F32)


def _dot_nt(a, b):
    return lax.dot_general(a, b, (((1,), (1,)), ((), ())), preferred_element_type=F32)


def _dot_tn(a, b):
    return lax.dot_general(a, b, (((0,), (0,)), ((), ())), preferred_element_type=F32)


def _sigmoid(x):
    return 1.0 / (1.0 + jnp.exp(-x))


def _layer_norm(x, g, b):
    mu = jnp.mean(x, -1, keepdims=True)
    xc = x - mu
    var = jnp.mean(xc * xc, -1, keepdims=True)
    return xc * lax.rsqrt(var + LN_EPS) * g + b


def _params(sem):
    return pltpu.CompilerParams(dimension_semantics=sem, vmem_limit_bytes=VMEM_LIMIT)


def _const_spec(shape):
    nd = len(shape)
    return pl.BlockSpec(shape, lambda *_: (0,) * nd)


def _hg_proj_kernel(x_ref, w_ref, q_ref, f_ref, i_ref, g_ref):
    x = x_ref[...].astype(BF16)
    q_ref[...] = _dot(x, w_ref[:, 0:HG_W]).astype(BF16)
    f_ref[...] = _dot(x, w_ref[:, HG_W:2 * HG_W])
    i_ref[...] = _dot(x, w_ref[:, 2 * HG_W:3 * HG_W]).astype(BF16)
    g = _dot(x, w_ref[:, 3 * HG_W:4 * HG_W])
    g_ref[...] = (g * _sigmoid(g)).astype(BF16)


def _hg_proj(x2, w_hg, tm):
    T = x2.shape[0]
    row = lambda i: (i, 0)
    out_spec = pl.BlockSpec((tm, HG_W), row)
    return pl.pallas_call(
        _hg_proj_kernel,
        grid=(T // tm,),
        in_specs=[pl.BlockSpec((tm, D_MODEL), row), _const_spec((D_MODEL, 4 * HG_W))],
        out_specs=[out_spec] * 4,
        out_shape=[jax.ShapeDtypeStruct((T, HG_W), BF16), jax.ShapeDtypeStruct((T, HG_W), F32),
                   jax.ShapeDtypeStruct((T, HG_W), BF16), jax.ShapeDtypeStruct((T, HG_W), BF16)],
        compiler_params=_params(("parallel",)),
        name="hg_proj",
    )(x2, w_hg)


_NQ = NSA_QW
_NKC = NSA_KVW
_NSL = SLOT_W
_O_K = _NQ
_O_V = _O_K + 3 * _NKC
_O_GT = _O_V + 3 * _NKC
_NSA_W = _O_GT + LANES


def _nsa_proj_kernel(x_ref, w_ref, cos_ref, sin_ref, e_ref,
                     q_ref, kc_ref, ks_ref, kw_ref, vc_ref, vs_ref, vw_ref, gt_ref):
    x = x_ref[...].astype(BF16)
    cos = cos_ref[...]
    sin = sin_ref[...]
    half = NSA_D // 2

    def rope(y):
        w = y.shape[1]
        reps = w // LANES
        lane = lax.broadcasted_iota(jnp.int32, y.shape, 1)
        fwd = pltpu.roll(y, w - half, 1)
        bwd = pltpu.roll(y, half, 1)
        rot = jnp.where((lane % NSA_D) < half, fwd, bwd)
        return y * jnp.tile(cos, (1, reps)) + rot * jnp.tile(sin, (1, reps))

    yq = rope(_dot(x, w_ref[:, 0:_NQ]))
    q_ref[...] = (yq * (NSA_D ** -0.5 * LOG2E)).astype(BF16)

    def slots(y):
        left = lax.broadcasted_iota(jnp.int32, (y.shape[0], LANES), 1) < NSA_D
        out = []
        for c in range(y.shape[1] // LANES):
            two = y[:, c * LANES:(c + 1) * LANES]
            out += [jnp.where(left, two, 0.0), jnp.where(left, pltpu.roll(two, NSA_D, 1), 0.0)]
        return out

    def head_pairs(y, o_ref):
        for c in range(y.shape[1] // LANES):
            o_ref[c] = y[:, c * LANES:(c + 1) * LANES]

    def slot_row(y):
        return jnp.concatenate(slots(y), axis=1).astype(BF16)

    yk = rope(_dot(x, w_ref[:, _O_K:_O_K + 3 * _NKC]))
    head_pairs(yk[:, 0:_NKC], kc_ref)
    ks_ref[...] = slot_row(yk[:, _NKC:2 * _NKC]) + e_ref[...]
    kw_ref[...] = slot_row(yk[:, 2 * _NKC:3 * _NKC])
    yv = _dot(x, w_ref[:, _O_V:_O_V + 3 * _NKC])
    head_pairs(yv[:, 0:_NKC], vc_ref)
    vs_ref[...] = slot_row(yv[:, _NKC:2 * _NKC])
    vw_ref[...] = slot_row(yv[:, 2 * _NKC:3 * _NKC])
    gates = _sigmoid(_dot(x, w_ref[:, _O_GT:_O_GT + LANES]))
    g_hi = gates.astype(BF16).astype(F32)
    lane = lax.broadcasted_iota(jnp.int32, gates.shape, 1)
    gt_ref[...] = jnp.where(lane < LANES // 2, g_hi, pltpu.roll(gates - g_hi, LANES // 2, 1)).astype(BF16)


def _nsa_proj(x2, w_nsa, cos_t, sin_t, e_tab, tm, S):
    T = x2.shape[0]
    row = lambda i: (i, 0)
    tiles_per_seq = S // tm
    tok = lambda w: (pl.BlockSpec((tm, w), row), jax.ShapeDtypeStruct((T, w), BF16))
    heads = (pl.BlockSpec((NSA_KV // 2, tm, LANES), lambda i: (0, i, 0)),
             jax.ShapeDtypeStruct((NSA_KV // 2, T, LANES), F32))
    outs = [tok(_NQ), heads, tok(_NSL), tok(_NSL), heads, tok(_NSL), tok(_NSL), tok(LANES)]
    return pl.pallas_call(
        _nsa_proj_kernel,
        grid=(T // tm,),
        in_specs=[pl.BlockSpec((tm, D_MODEL), row), _const_spec((D_MODEL, _NSA_W)),
                  pl.BlockSpec((tm, LANES), row), pl.BlockSpec((tm, LANES), row),
                  pl.BlockSpec((tm, _NSL), lambda i: (i % tiles_per_seq, 0))],
        out_specs=[spec for spec, _ in outs],
        out_shape=[shape for _, shape in outs],
        compiler_params=_params(("parallel",)),
        name="nsa_proj",
    )(x2, w_nsa, cos_t, sin_t, e_tab)


def _hgrn_kernel(q_ref, f_ref, i_ref, g_ref, lb_ref, ng_ref, o_ref, *, n_chunks, heads, unroll):
    C = HG_CHUNK
    ng = ng_ref[...]
    r = lax.broadcasted_iota(jnp.int32, (C, C), 0)
    c = lax.broadcasted_iota(jnp.int32, (C, C), 1)
    tril = r >= c
    tril_b = tril.astype(BF16)

    def chunk_cumsum(x):
        hi = x.astype(BF16)
        lo = (x - hi.astype(F32)).astype(BF16)
        return _dot(tril_b, hi) + _dot(tril_b, lo)

    def body(cj, states):
        states = list(states)
        items = [(u, h) for u in range(unroll) for h in range(heads)]
        rows = {it: pl.ds(pl.multiple_of((cj * unroll + it[0]) * C, C), C) for it in items}
        cols = {it: slice(it[1] * HG_DK, (it[1] + 1) * HG_DK) for it in items}
        f, eb, q_dec, k_inv, k_end, attn, out = {}, {}, {}, {}, {}, {}, {}
        for it in items:
            lb = lb_ref[:, cols[it]]
            f[it] = lb + (1.0 - lb) * _sigmoid(f_ref[rows[it], cols[it]])
        for it in items:
            eb[it] = jnp.exp(chunk_cumsum(jnp.log(f[it])))
        for it in items:
            q_dec[it] = (q_ref[rows[it], cols[it]].astype(F32) * eb[it]).astype(BF16)
            k_inv_f = (1.0 - f[it]) * (1.0 / eb[it])
            k_inv[it] = k_inv_f.astype(BF16)
            k_end[it] = (k_inv_f * eb[it][C - 1:C, :]).astype(BF16)
        for it in items:
            attn[it] = jnp.where(tril, _dot_nt(q_dec[it], k_inv[it]), 0.0).astype(BF16)
        for it in items:
            h = it[1]
            v = i_ref[rows[it], cols[it]]
            out[it] = _dot(attn[it], v) + _dot_nt(q_dec[it], states[h].astype(BF16))
            states[h] = states[h] * eb[it][C - 1:C, :] + _dot_tn(v, k_end[it])
        for it in items:
            o = out[it]
            on = o * lax.rsqrt(jnp.mean(o * o, -1, keepdims=True) + RMS_EPS) * ng
            o_ref[rows[it], cols[it]] = (on * g_ref[rows[it], cols[it]].astype(F32)).astype(BF16)
        return tuple(states)

    init = tuple(jnp.zeros((HG_DV, HG_DK), F32) for _ in range(heads))
    lax.fori_loop(0, n_chunks // unroll, body, init)


def _hgrn(q, f, i, g, lb, ng, B, S, heads=8, unroll=8):
    T = B * S
    w = heads * HG_DK
    blk = pl.BlockSpec((S, w), lambda b, h: (b, h))
    return pl.pallas_call(
        functools.partial(_hgrn_kernel, n_chunks=S // HG_CHUNK, heads=heads, unroll=unroll),
        grid=(B, HG_HEADS // heads),
        in_specs=[blk, blk, blk, blk,
                  pl.BlockSpec((1, w), lambda b, h: (0, h)),
                  pl.BlockSpec((1, HG_DV), lambda b, h: (0, 0))],
        out_specs=blk,
        out_shape=jax.ShapeDtypeStruct((T, HG_W), BF16),
        compiler_params=_params(("parallel", "parallel")),
        name="hgrn",
    )(q, f, i, g, lb, ng)


def _compress_kernel(tk_ref, tv_ref, pk_ref, w1k_ref, b1k_ref, w2k_ref, b2k_ref,
                     pv_ref, w1v_ref, b1v_ref, w2v_ref, b2v_ref, tab_ref, ko_ref, vo_ref, u_ref, *, n_half):
    half = CMP_STRIDE * LANES
    pairs = NSA_KV // 2

    def mlp(t_ref, pos_ref, w1_ref, b1_ref, w2_ref, b2_ref, o_ref, tab=None):
        base = _dot(pos_ref[...], w1_ref[0])[0:1, :] + b1_ref[...]
        for p in range(pairs):
            for l in range(CMP_STRIDE):
                u_ref[p * n_half:(p + 1) * n_half, l * LANES:(l + 1) * LANES] = (
                    t_ref[p, pl.ds(l, n_half, stride=CMP_STRIDE), :].astype(BF16))
        u = u_ref[...]
        top = [_dot(u, w1_ref[par, 0:half, :]) for par in range(2)]
        bot = [_dot(u, w1_ref[par, half:2 * half, :]) for par in range(2)]
        hids = []
        for h in range(NSA_KV):
            r0 = (h // 2) * n_half
            t_h, b_h = top[h % 2], bot[h % 2]
            bot_next = jnp.concatenate([b_h[r0 + 1:r0 + n_half, :], jnp.zeros((1, CMP_HIDDEN), F32)], axis=0)
            hids.append(t_h[r0:r0 + n_half, :] + bot_next + base)
        hid = jnp.concatenate(hids, axis=0)
        hid = hid * _sigmoid(hid)
        out = _dot(hid.astype(BF16), w2_ref[...]) + b2_ref[...]
        for h in range(NSA_KV):
            out_h = out[h * n_half:(h + 1) * n_half, :]
            o_ref[0, h] = (out_h if tab is None else out_h + tab).astype(BF16)

    mlp(tk_ref, pk_ref, w1k_ref, b1k_ref, w2k_ref, b2k_ref, ko_ref)
    mlp(tv_ref, pv_ref, w1v_ref, b1v_ref, w2v_ref, b2v_ref, vo_ref, tab_ref[...])


def _compress(tk, tv, pk, w1k, b1k, w2k, b2k, pv, w1v, b1v, w2v, b2v, tab, B, S):
    nh = S // CMP_STRIDE
    t_spec = pl.BlockSpec((NSA_KV // 2, S, LANES), lambda b: (0, b, 0))
    o_spec = lambda w: pl.BlockSpec((1, NSA_KV, nh, w), lambda b: (b, 0, 0, 0))
    specs = lambda arrs: [_const_spec(a.shape) for a in arrs]
    return pl.pallas_call(
        functools.partial(_compress_kernel, n_half=nh),
        grid=(B,),
        in_specs=([t_spec, t_spec] + specs((pk, w1k, b1k, w2k, b2k)) + specs((pv, w1v, b1v, w2v, b2v))
                  + [_const_spec(tab.shape)]),
        out_specs=[o_spec(LANES), o_spec(VSLOT)],
        out_shape=[jax.ShapeDtypeStruct((B, NSA_KV, nh, LANES), BF16),
                   jax.ShapeDtypeStruct((B, NSA_KV, nh, VSLOT), BF16)],
        scratch_shapes=[pltpu.VMEM((NSA_KV // 2 * nh, CMP_STRIDE * LANES), BF16)],
        compiler_params=_params(("parallel",)),
        name="compress",
    )(tk, tv, pk, w1k, b1k, w2k, b2k, pv, w1v, b1v, w2v, b2v, tab)


_L_OVL = LANES - 32


def _nsa_kernel(q_ref, ks_ref, vs_ref, kw_ref, vw_ref, kc_ref, vc_ref, gt_ref, gx_ref,
                o_ref, sbuf, qpl_ref, qaug_ref, acc_ref, accw_ref, mt_ref, cmp_ref, gexp_ref,
                *, TQ, CK, NW, n_cmp, n_blk, top_n):
    R = NSA_G * TQ
    NH = CK // LANES
    q0 = pl.program_id(1) * TQ
    lane_q = lax.broadcasted_iota(jnp.int32, (TQ, LANES), 1)
    t_q = q0 + lax.broadcasted_iota(jnp.int32, (TQ, 1), 0)

    def add_bias(s, bias):
        return jnp.concatenate([s[g * TQ:(g + 1) * TQ] + bias for g in range(NSA_G)], axis=0)

    def fold_max(s):
        return functools.reduce(jnp.maximum, [s[:, u * LANES:(u + 1) * LANES] for u in range(NH)])

    def lanes_ck(m):
        return jnp.concatenate([m] * NH, axis=1)

    bias_c = jnp.where((lane_q * CMP_STRIDE + (CMP_LEN - 1) <= t_q) & (lane_q < n_cmp), 0.0, NEG)
    key_l = lax.broadcasted_iota(jnp.int32, (TQ, CK), 1)
    n_sel = (q0 + TQ + CK - 1) // CK
    last = n_sel - 1
    bias_diag = jnp.where(last * CK + key_l <= t_q, 0.0, NEG)

    j_blk = lax.broadcasted_iota(jnp.int32, (n_blk, TQ), 0)
    cur = (q0 + lax.broadcasted_iota(jnp.int32, (n_blk, TQ), 1)) // SEL_BLOCK
    blk_ok = j_blk <= cur
    forced = (j_blk == 0) | (j_blk == cur) | (j_blk == cur - 1)

    gexp_ref[...] = _dot(gt_ref[...], gx_ref[...])

    def q_slabs(kvh):
        slabs = []
        for pair in range(NSA_G // 2):
            col = (kvh * NSA_G + 2 * pair) * NSA_D
            x = q_ref[:, col:col + LANES].astype(F32)
            slabs += [x, pltpu.roll(x, NSA_D, 1)]
        return slabs

    for kvh in range(NSA_KV):
        qpl_ref[kvh] = jnp.concatenate(
            [jnp.where(lane_q < NSA_D, x, 0.0).astype(BF16) for x in q_slabs(kvh)], axis=0)

    def chunk(ref, start, kvh, slot=LANES):
        return ref[pl.ds(start, CK), kvh * slot:(kvh + 1) * slot]

    def score_pass(qs_ref, k_ref, ci, slot, bias, m_ref):
        start = pl.multiple_of(ci * CK, CK)
        for kvh in range(NSA_KV):
            s = _dot_nt(qs_ref[kvh], chunk(k_ref, start, kvh))
            if bias is not None:
                s = add_bias(s, bias)
            sbuf[kvh, slot] = s
            m_ref[kvh] = jnp.maximum(m_ref[kvh], fold_max(s))

    def value_pass(v_ref, ci, slot, m_ref, out_ref):
        start = pl.multiple_of(ci * CK, CK)
        for kvh in range(NSA_KV):
            p = jnp.exp2(sbuf[kvh, slot] - lanes_ck(m_ref[kvh]))
            v1 = jnp.concatenate([chunk(v_ref, start, kvh), jnp.ones((CK, LANES), BF16)], axis=1)
            out_ref[kvh] += _dot(p.astype(BF16), v1)

    def begin_max(m_ref):
        for kvh in range(NSA_KV):
            m_ref[kvh] = jnp.full((R, LANES), -3e38, F32)

    def finish_max(m_ref, out_ref):
        for kvh in range(NSA_KV):
            m_ref[kvh] = jnp.broadcast_to(jnp.max(m_ref[kvh], -1, keepdims=True), (R, LANES))
            out_ref[kvh] = jnp.zeros((R, VSLOT), F32)

    def loop(n, body):
        lax.fori_loop(0, n, lambda i, c: (body(i), c)[1], 0)

    def win_chunk(k):
        cw = last - (NW - 1) + k
        return cw, jnp.maximum(cw, 0)

    def win_scores(k):
        cw, cidx = win_chunk(k)
        key = cidx * CK + key_l
        ok = (key <= t_q) & (key > t_q - WINDOW) & (cw >= 0)
        score_pass(qpl_ref, kw_ref, cidx, k, jnp.where(ok, 0.0, NEG), mt_ref)

    KV = range(NSA_KV)
    s_c = [add_bias(_dot_nt(qpl_ref[h], kc_ref[0, h]), bias_c) for h in KV]
    m_c = [jnp.maximum(jnp.max(s, -1, keepdims=True), 0.1 * NEG) for s in s_c]
    acc_c = [_dot(jnp.exp2(s_c[h] - m_c[h]).astype(BF16), vc_ref[0, h]) for h in KV]
    p_n = [a[:, 0:LANES] * (1.0 / jnp.maximum(a[:, LANES:VSLOT], 1e-30)) for a in acc_c]
    for h in KV:
        cmp_ref[h] = p_n[h]
    imp = [functools.reduce(jnp.add, [p[g * TQ:(g + 1) * TQ] for g in range(NSA_G)]).T[_L_OVL:_L_OVL + n_blk, :]
           for p in p_n]
    score = [jnp.where(blk_ok, x + jnp.where(forced, SEL_FORCE, 0.0), -1.0) for x in imp]
    rank = [jnp.zeros((n_blk, TQ), jnp.int32) for _ in KV]
    for i in range(n_blk):
        for h in KV:
            ri = score[h][i:i + 1, :]
            beats = (ri > score[h]) | ((ri == score[h]) & (j_blk > i))
            rank[h] = rank[h] + beats.astype(jnp.int32)
    for h in KV:
        bias_t = jnp.where((rank[h] < top_n) & blk_ok, 0.0, NEG)
        bias_q = jnp.concatenate([jnp.zeros((NSA_D, TQ), F32), bias_t,
                                  jnp.zeros((LANES - NSA_D - n_blk, TQ), F32)], axis=0).T
        qaug_ref[h] = jnp.concatenate(
            [jnp.where(lane_q < NSA_D, x, bias_q).astype(BF16) for x in q_slabs(h)], axis=0)

    sel_scores = lambda ci, bias=None: score_pass(qaug_ref, ks_ref, ci, ci, bias, mt_ref)
    sel_values = lambda ci: value_pass(vs_ref, ci, ci, mt_ref, acc_ref)

    def chunks(n, one):
        loop(n // 4, lambda j: [one(4 * j + u) for u in range(4)])
        loop((n % 4) // 2, lambda j: [one(n - (n % 4) + u) for u in range(2)])
        loop(n % 2, lambda j: one(n - 1))

    begin_max(mt_ref)
    chunks(last, sel_scores)
    sel_scores(last, bias_diag)
    finish_max(mt_ref, acc_ref)
    chunks(n_sel, sel_values)

    begin_max(mt_ref)
    for k in range(NW):
        win_scores(k)
    finish_max(mt_ref, accw_ref)
    for k in range(NW):
        value_pass(vw_ref, win_chunk(k)[1], k, mt_ref, accw_ref)

    left = lane_q < NSA_D
    for kvh in range(NSA_KV):
        for pair in range(NSA_G // 2):
            col = (kvh * NSA_G + 2 * pair) * NSA_D
            r_e = pl.ds(2 * pair * TQ, TQ)
            r_o = pl.ds((2 * pair + 1) * TQ, TQ)

            def packed(ref, lanes):
                return jnp.where(left, ref[kvh, r_e, lanes], pltpu.roll(ref[kvh, r_o, lanes], NSA_D, 1))

            out = packed(cmp_ref, slice(0, LANES)) * gexp_ref[:, col:col + LANES]
            for br, ref in ((1, acc_ref), (2, accw_ref)):
                den = jnp.where(left, ref[kvh, r_e, LANES:VSLOT], ref[kvh, r_o, LANES:VSLOT])
                gate = gexp_ref[:, br * NSA_QW + col:br * NSA_QW + col + LANES]
                out = out + packed(ref, slice(0, LANES)) * (gate / den)
            o_ref[:, col:col + LANES] = out.astype(BF16)


def _window_chunks(S, TQ, CK):
    return max((q0 + TQ - 1) // CK - max(q0 - WINDOW + 1, 0) // CK + 1 for q0 in range(0, S, TQ))


def _nsa(q, ks, vs, kw, vw, kc, vc, gt, B, S, TQ, CK):
    T = B * S
    n_blk = S // SEL_BLOCK
    n_cmp = (S - CMP_LEN) // CMP_STRIDE + 1
    R = NSA_G * TQ
    NW = _window_chunks(S, TQ, CK)
    assert kc.shape[2] == LANES and n_blk <= LANES - _L_OVL and S // CK >= NW
    seq = lambda w: pl.BlockSpec((S, w), lambda b, i: (b, 0))
    qrow = lambda w: pl.BlockSpec((TQ, w), lambda b, i: (b * (S // TQ) + i, 0))
    cmp_spec = lambda w: pl.BlockSpec((1, NSA_KV, LANES, w), lambda b, i: (b, 0, 0, 0))
    kern = functools.partial(_nsa_kernel, TQ=TQ, CK=CK, NW=NW, n_cmp=n_cmp, n_blk=n_blk,
                             top_n=min(SEL_TOPN, n_blk))
    per_head = lambda w, dt: pltpu.VMEM((NSA_KV, R, w), dt)
    gx = jnp.asarray(_gate_expand_table(), BF16)
    return pl.pallas_call(
        kern,
        grid=(B, S // TQ),
        in_specs=[qrow(NSA_QW), seq(SLOT_W), seq(SLOT_W), seq(SLOT_W), seq(SLOT_W),
                  cmp_spec(LANES), cmp_spec(VSLOT), qrow(LANES), _const_spec(gx.shape)],
        out_specs=qrow(NSA_QW),
        out_shape=jax.ShapeDtypeStruct((T, NSA_QW), BF16),
        scratch_shapes=[pltpu.VMEM((NSA_KV, S // CK, R, CK), F32),
                        per_head(LANES, BF16), per_head(LANES, BF16),
                        per_head(VSLOT, F32), per_head(VSLOT, F32),
                        per_head(LANES, F32),
                        per_head(LANES, F32),
                        pltpu.VMEM((TQ, 3 * NSA_QW), F32)],
        compiler_params=_params(("parallel", "parallel")),
        name="nsa",
    )(q, ks, vs, kw, vw, kc, vc, gt, gx)


def _merge_kernel(x_ref, a_ref, b_ref, wga_ref, wgb_ref, wua_ref, wub_ref, wo_ref, g_ref, bb_ref, o_ref):
    half = x_ref.shape[0] // 2
    for r in (pl.ds(0, half), pl.ds(half, half)):
        x = x_ref[r, :]
        xb = x.astype(BF16)
        m = (_sigmoid(_dot(xb, wga_ref[...])) * _dot(a_ref[r, :], wua_ref[...])
             + _sigmoid(_dot(xb, wgb_ref[...])) * _dot(b_ref[r, :], wub_ref[...]))
        mix = _dot(m.astype(BF16), wo_ref[...])
        o_ref[r, :] = _layer_norm(DN_ALPHA * x + mix, g_ref[...], bb_ref[...])


def _merge(x2, a, b, wga, wgb, wua, wub, wo, g, bb, tm):
    T = x2.shape[0]
    row = pl.BlockSpec((tm, D_MODEL), lambda i: (i, 0))
    wsp = _const_spec((D_MODEL, D_MODEL))
    vec = _const_spec((1, D_MODEL))
    return pl.pallas_call(
        _merge_kernel,
        grid=(T // tm,),
        in_specs=[row, row, row, wsp, wsp, wsp, wsp, wsp, vec, vec],
        out_specs=row,
        out_shape=jax.ShapeDtypeStruct((T, D_MODEL), F32),
        compiler_params=_params(("parallel",)),
        name="merge",
    )(x2, a, b, wga, wgb, wua, wub, wo, g, bb)


def _ffn_kernel(h_ref, wg_ref, wu_ref, wd_ref, g_ref, b_ref, o_ref):
    half = h_ref.shape[0] // 2
    for r in (pl.ds(0, half), pl.ds(half, half)):
        h = h_ref[r, :]
        hb = h.astype(BF16)
        gate = _dot(hb, wg_ref[...])
        act = (gate * _sigmoid(gate) * _dot(hb, wu_ref[...])).astype(BF16)
        ffn = _dot(act, wd_ref[...])
        o_ref[r, :] = _layer_norm(DN_ALPHA * h + ffn, g_ref[...], b_ref[...])


def _ffn(h, wg, wu, wd, g, b, tm):
    T = h.shape[0]
    d_ff = wg.shape[1]
    row = pl.BlockSpec((tm, D_MODEL), lambda i: (i, 0))
    vec = _const_spec((1, D_MODEL))
    once = pl.Buffered(1)
    return pl.pallas_call(
        _ffn_kernel,
        grid=(T // tm,),
        in_specs=[row,
                  pl.BlockSpec((D_MODEL, d_ff), lambda i: (0, 0), pipeline_mode=once),
                  pl.BlockSpec((D_MODEL, d_ff), lambda i: (0, 0), pipeline_mode=once),
                  pl.BlockSpec((d_ff, D_MODEL), lambda i: (0, 0), pipeline_mode=once),
                  vec, vec],
        out_specs=row,
        out_shape=jax.ShapeDtypeStruct((T, D_MODEL), F32),
        compiler_params=_params(("parallel",)),
        name="ffn",
    )(h, wg, wu, wd, g, b)


def _cmp_value_table(n_half, n_cmp, n_blk):
    cs = np.arange(n_cmp)[:, None] * CMP_STRIDE
    bs = np.arange(n_blk)[None, :] * SEL_BLOCK
    ov = np.minimum(cs + CMP_LEN, bs + SEL_BLOCK) - np.maximum(cs, bs)
    out = np.zeros((n_half, VSLOT), np.float32)
    out[:, LANES:] = 1.0
    out[:n_cmp, _L_OVL:_L_OVL + n_blk] = np.clip(ov, 0, None) / CMP_LEN
    return out


def _gate_expand_table():
    out = np.zeros((LANES, 3 * NSA_QW), np.float32)
    for br in range(3):
        for h in range(NSA_HEADS):
            c = br * NSA_QW + h * NSA_D
            out[br * NSA_HEADS + h, c:c + NSA_D] = 1.0
            out[LANES // 2 + br * NSA_HEADS + h, c:c + NSA_D] = 1.0
    return out


def kernel(x, positions, w_in, hg_lb_logits, hg_norm_g, cmp_k_pos, cmp_k_w1, cmp_k_b1, cmp_k_w2, cmp_k_b2, cmp_v_pos, cmp_v_w1, cmp_v_b1, cmp_v_w2, cmp_v_b2, w_up_hg, w_up_nsa, w_o, ln1_g, ln1_b, w_ffn_gate, w_ffn_up, w_ffn_down, ln2_g, ln2_b):
    B, S, _ = x.shape
    T = B * S
    tm = min(512, S)
    TQ = 128
    CK = min(256, S)
    n_blk = S // SEL_BLOCK
    n_cmp = (S - CMP_LEN) // CMP_STRIDE + 1
    n_half = S // CMP_STRIDE

    lb_table = jnp.cumsum(jax.nn.softmax(hg_lb_logits.astype(F32), axis=0), axis=0)
    x2 = x.reshape(T, D_MODEL)

    w = w_in[0]
    o = 0
    o += 4 * HG_W
    wq = w[:, o:o + NSA_QW]; o += NSA_QW
    kv = []
    for _ in range(6):
        kv.append(w[:, o:o + NSA_KVW]); o += NSA_KVW
    w_kc, w_vc, w_ks, w_vs, w_kw, w_vw = kv
    w_gt = jnp.pad(w[:, o:o + 3 * NSA_HEADS], ((0, 0), (0, LANES - 3 * NSA_HEADS))); o += 3 * NSA_HEADS
    w_ga = w[:, o:o + D_MODEL]; o += D_MODEL
    w_gb = w[:, o:o + D_MODEL]
    w_nsa = jnp.concatenate([wq, w_kc, w_ks, w_kw, w_vc, w_vs, w_vw, w_gt], axis=1).astype(BF16)

    half = NSA_D // 2
    inv_freq = ROPE_THETA ** (-jnp.arange(half, dtype=F32) / half)
    ang = positions.astype(F32).reshape(T, 1) * inv_freq[None, :]
    cos_t = jnp.tile(jnp.cos(ang), (1, LANES // half))
    sin_h = jnp.sin(ang)
    sin_t = jnp.tile(jnp.concatenate([-sin_h, sin_h], axis=1), (1, LANES // NSA_D))

    e_np = np.zeros((S, NSA_KV, LANES), np.float32)
    e_np[np.arange(S), :, NSA_D + np.arange(S) // SEL_BLOCK] = 1.0
    e_tab = jnp.asarray(e_np.reshape(S, SLOT_W), BF16)

    hq, hf, hi, hg = _hg_proj(x2, w.astype(BF16), tm)
    a = _hgrn(hq, hf, hi, hg, lb_table[0:1], hg_norm_g[0:1].astype(F32), B, S)

    q, kc_tok, ks, kw, vc_tok, vs, vw, gt = _nsa_proj(x2, w_nsa, cos_t, sin_t, e_tab, tm, S)

    def slab_rows(m, parity):
        m = m.reshape(CMP_LEN, NSA_D, -1)
        lo = parity * NSA_D
        return jnp.pad(m, ((0, 0), (lo, LANES - NSA_D - lo), (0, 0))).reshape(CMP_LEN * LANES, -1)

    def w1_pair(w1):
        return jnp.stack([slab_rows(w1, 0), slab_rows(w1, 1)]).astype(BF16)

    def pos_rows(p):
        flat = slab_rows(p.reshape(CMP_LEN * NSA_D, 1), 0).reshape(1, CMP_LEN * LANES)
        return jnp.broadcast_to(flat, (8, CMP_LEN * LANES)).astype(BF16)

    def pad_out(w2, b2, width):
        return (jnp.pad(w2, ((0, 0), (0, width - NSA_D))).astype(BF16),
                jnp.pad(b2, (0, width - NSA_D)).reshape(1, width).astype(F32))

    w2k, b2k = pad_out(cmp_k_w2[0], cmp_k_b2[0], LANES)
    w2v, b2v = pad_out(cmp_v_w2[0], cmp_v_b2[0], VSLOT)
    kc, vc = _compress(
        kc_tok, vc_tok,
        pos_rows(cmp_k_pos[0]), w1_pair(cmp_k_w1[0]), cmp_k_b1[0].reshape(1, -1).astype(F32),
        w2k, b2k,
        pos_rows(cmp_v_pos[0]), w1_pair(cmp_v_w1[0]), cmp_v_b1[0].reshape(1, -1).astype(F32),
        w2v, b2v,
        jnp.asarray(_cmp_value_table(n_half, n_cmp, n_blk)), B, S)

    b_out = _nsa(q, ks, vs, kw, vw, kc, vc, gt, B, S, TQ, CK)

    h1 = _merge(x2, a, b_out, w_ga.astype(BF16), w_gb.astype(BF16),
                w_up_hg[0].astype(BF16), w_up_nsa[0].astype(BF16), w_o[0].astype(BF16),
                ln1_g[0].reshape(1, -1).astype(F32), ln1_b[0].reshape(1, -1).astype(F32), tm)
    out = _ffn(h1, w_ffn_gate[0].astype(BF16), w_ffn_up[0].astype(BF16), w_ffn_down[0].astype(BF16),
               ln2_g[0].reshape(1, -1).astype(F32), ln2_b[0].reshape(1, -1).astype(F32), tm)
    return out.reshape(B, S, D_MODEL)
```
